```python
import math
import jax, jax.numpy as jnp
from jax import lax
import numpy as np

D_MODEL = 4096
BATCH = 4
SEQ = 2048
DEPTH = 1

MEM_TOKENS = 256
FFN_DIM = 256 * ((8 * D_MODEL // 3 + 255) // 256)
CONV_WIDTH = D_MODEL // 2
CONV_K = 3
DN_HEAD_DIM = 128
DN_HEADS = (D_MODEL // 2) // DN_HEAD_DIM
DN_WIDTH = DN_HEADS * DN_HEAD_DIM
DN_CONV_K = 4
DN_CHUNK = 64
XATTN_HEADS = 4
XATTN_HEAD_DIM = 128
XATTN_WIDTH = XATTN_HEADS * XATTN_HEAD_DIM
EPS = 1e-6
IN_SPLIT_SIZES = (CONV_WIDTH, CONV_WIDTH, CONV_WIDTH, 3 * DN_WIDTH, DN_WIDTH, DN_HEADS, DN_HEADS, D_MODEL, D_MODEL)
IN_PROJ_DIM = 3 * CONV_WIDTH + 4 * DN_WIDTH + 2 * DN_HEADS + 2 * D_MODEL

kernel_name = "hybrid_conv_deltanet_macaron_memxattn"


def rms_norm(x, w):
    xf = x.astype(jnp.float32)
    y = xf * lax.rsqrt(jnp.mean(xf * xf, axis=-1, keepdims=True) + EPS)
    return (y * w.astype(jnp.float32)).astype(x.dtype)


def l2_norm(x):
    xf = x.astype(jnp.float32)
    return xf * lax.rsqrt(jnp.sum(xf * xf, axis=-1, keepdims=True) + EPS)


def swiglu(x, w_gate, w_up, w_down):
    return (jax.nn.silu(x @ w_gate) * (x @ w_up)) @ w_down


def causal_dwconv(x, w):
    K = w.shape[0]
    S = x.shape[1]
    xp = jnp.pad(x, ((0, 0), (K - 1, 0), (0, 0)))
    y = xp[:, 0:S, :] * w[0]
    for i in range(1, K):
        y = y + xp[:, i:i + S, :] * w[i]
    return y


def gated_delta_rule_chunked(q, k, v, g, beta):
    Bsz, S, H, DK = q.shape
    DV = v.shape[-1]
    C = DN_CHUNK
    NC = S // C

    def chunk(t):
        t = t.astype(jnp.float32).reshape((Bsz, NC, C, H) + t.shape[3:])
        return jnp.moveaxis(t, 3, 1)

    q, k, v, g, beta = chunk(q), chunk(k), chunk(v), chunk(g), chunk(beta)
    gc = jnp.cumsum(g, axis=-1)
    idx = jnp.arange(C)
    incl = idx[:, None] >= idx[None, :]
    strict = idx[:, None] > idx[None, :]
    decay = jnp.exp(jnp.where(incl, gc[..., :, None] - gc[..., None, :], -jnp.inf))
    k_beta = k * beta[..., None]
    v_beta = v * beta[..., None]
    L = jnp.where(strict, jnp.einsum('bhnid,bhnjd->bhnij', k_beta, k) * decay, 0.0)
    rhs = jnp.concatenate([v_beta, k_beta * jnp.exp(gc)[..., None]], axis=-1)
    sol = lax.linalg.triangular_solve(L + jnp.eye(C, dtype=jnp.float32), rhs,
                                      left_side=True, lower=True, unit_diagonal=True)
    u_base, w_dec = sol[..., :DV], sol[..., DV:]
    a_qk = jnp.einsum('bhnid,bhnjd->bhnij', q, k) * decay
    q_dec = q * jnp.exp(gc)[..., None]
    k_tail = k * jnp.exp(gc[..., -1:] - gc)[..., None]
    g_last = jnp.exp(gc[..., -1])

    xs = tuple(jnp.moveaxis(t, 2, 0) for t in (u_base, w_dec, a_qk, q_dec, k_tail, g_last))

    def step(state, inp):
        ub, wc, aqk, qd, kt, gl = inp
        u = ub - jnp.einsum('bhck,bhkv->bhcv', wc, state)
        o = jnp.einsum('bhck,bhkv->bhcv', qd, state) + jnp.einsum('bhij,bhjv->bhiv', aqk, u)
        state = state * gl[..., None, None] + jnp.einsum('bhck,bhcv->bhkv', kt, u)
        return state, o

    s0 = jnp.zeros((Bsz, H, DK, DV), jnp.float32)
    _, o = lax.scan(step, s0, xs)
    o = jnp.moveaxis(o, 0, 2).reshape(Bsz, H, S, DV)
    return jnp.transpose(o, (0, 2, 1, 3))


def memory_cross_attention(hn, mn, wq, wk, wv, wo):
    Bsz, S, _ = hn.shape
    M = mn.shape[1]
    qh = (hn @ wq).reshape(Bsz, S, XATTN_HEADS, XATTN_HEAD_DIM)
    kh = (mn @ wk).reshape(Bsz, M, XATTN_HEADS, XATTN_HEAD_DIM)
    vh = (mn @ wv).reshape(Bsz, M, XATTN_HEADS, XATTN_HEAD_DIM)
    scores = jnp.einsum('bshd,bmhd->bhsm', qh.astype(jnp.float32), kh.astype(jnp.float32))
    p = jax.nn.softmax(scores * (XATTN_HEAD_DIM ** -0.5), axis=-1)
    o = jnp.einsum('bhsm,bmhd->bshd', p, vh.astype(jnp.float32)).astype(hn.dtype)
    return o.reshape(Bsz, S, XATTN_WIDTH) @ wo


def setup_inputs(seed: int = 0) -> dict:
    key = jax.random.key(seed)
    ks = jax.random.split(key, 32)

    def dense(k, shape, fan_in):
        return jax.random.normal(k, (DEPTH,) + shape, jnp.float32) * (fan_in ** -0.5)

    def gain(k, n):
        return 1.0 + 0.02 * jax.random.normal(k, (DEPTH, n), jnp.float32)

    dt = jnp.exp(jax.random.uniform(ks[14], (DEPTH, DN_HEADS), jnp.float32,
                                    math.log(1e-3), math.log(1e-1)))
    return {
        "x": jax.random.normal(ks[0], (BATCH, SEQ, D_MODEL), jnp.float32),
        "mem": jax.random.normal(ks[1], (BATCH, MEM_TOKENS, D_MODEL), jnp.float32),
        "ffn1_norm": gain(ks[2], D_MODEL),
        "ffn1_w_gate": dense(ks[3], (D_MODEL, FFN_DIM), D_MODEL),
        "ffn1_w_up": dense(ks[4], (D_MODEL, FFN_DIM), D_MODEL),
        "ffn1_w_down": dense(ks[5], (FFN_DIM, D_MODEL), FFN_DIM),
        "mix_norm": gain(ks[6], D_MODEL),
        "w_in": dense(ks[7], (D_MODEL, IN_PROJ_DIM), D_MODEL),
        "conv_w": dense(ks[8], (CONV_K, CONV_WIDTH), CONV_K),
        "qkv_conv_w": dense(ks[9], (DN_CONV_K, 3 * DN_WIDTH), DN_CONV_K),
        "a_log": jnp.log(jax.random.uniform(ks[10], (DEPTH, DN_HEADS), jnp.float32, 1.0, 16.0)),
        "dt_bias": dt + jnp.log(-jnp.expm1(-dt)),
        "dn_out_norm": gain(ks[11], DN_HEAD_DIM),
        "w_out_conv": dense(ks[12], (CONV_WIDTH, D_MODEL), CONV_WIDTH),
        "w_out_delta": dense(ks[13], (DN_WIDTH, D_MODEL), DN_WIDTH),
        "w_o": dense(ks[15], (D_MODEL, D_MODEL), D_MODEL),
        "xattn_norm": gain(ks[16], D_MODEL),
        "mem_norm": gain(ks[17], D_MODEL),
        "xattn_wq": dense(ks[18], (D_MODEL, XATTN_WIDTH), D_MODEL),
        "xattn_wk": dense(ks[19], (D_MODEL, XATTN_WIDTH), D_MODEL),
        "xattn_wv": dense(ks[20], (D_MODEL, XATTN_WIDTH), D_MODEL),
        "xattn_wo": dense(ks[21], (XATTN_WIDTH, D_MODEL), XATTN_WIDTH),
        "ffn2_norm": gain(ks[22], D_MODEL),
        "ffn2_w_gate": dense(ks[23], (D_MODEL, FFN_DIM), D_MODEL),
        "ffn2_w_up": dense(ks[24], (D_MODEL, FFN_DIM), D_MODEL),
        "ffn2_w_down": dense(ks[25], (FFN_DIM, D_MODEL), FFN_DIM),
        "final_norm": 1.0 + 0.02 * jax.random.normal(ks[26], (D_MODEL,), jnp.float32),
    }


def reference(x, mem, ffn1_norm, ffn1_w_gate, ffn1_w_up, ffn1_w_down, mix_norm, w_in,
              conv_w, qkv_conv_w, a_log, dt_bias, dn_out_norm, w_out_conv, w_out_delta, w_o,
              xattn_norm, mem_norm, xattn_wq, xattn_wk, xattn_wv, xattn_wo,
              ffn2_norm, ffn2_w_gate, ffn2_w_up, ffn2_w_down, final_norm):
    Bsz, S, _ = x.shape
    split_points = [int(v) for v in np.cumsum(IN_SPLIT_SIZES)[:-1]]
    h = x
    for l in range(DEPTH):
        h = h + 0.5 * swiglu(rms_norm(h, ffn1_norm[l]), ffn1_w_gate[l], ffn1_w_up[l], ffn1_w_down[l])

        u = rms_norm(h, mix_norm[l])
        proj = u @ w_in[l]
        c_x, c_c, c_b, qkv, z, b_logit, a_logit, gate_a, gate_b = jnp.split(proj, split_points, axis=-1)

        y_conv = (c_b * causal_dwconv(c_c * c_x, conv_w[l])) @ w_out_conv[l]

        qkv = jax.nn.silu(causal_dwconv(qkv, qkv_conv_w[l]))
        q, k, v = jnp.split(qkv, 3, axis=-1)
        q = l2_norm(q.reshape(Bsz, S, DN_HEADS, DN_HEAD_DIM)) * (DN_HEAD_DIM ** -0.5)
        k = l2_norm(k.reshape(Bsz, S, DN_HEADS, DN_HEAD_DIM))
        v = v.reshape(Bsz, S, DN_HEADS, DN_HEAD_DIM)
        beta = jax.nn.sigmoid(b_logit.astype(jnp.float32))
        g = -jnp.exp(a_log[l].astype(jnp.float32)) * jax.nn.softplus(
            a_logit.astype(jnp.float32) + dt_bias[l].astype(jnp.float32))
        o = gated_delta_rule_chunked(q, k, v, g, beta)
        zf = z.astype(jnp.float32).reshape(Bsz, S, DN_HEADS, DN_HEAD_DIM)
        o = o * lax.rsqrt(jnp.mean(o * o, axis=-1, keepdims=True) + EPS) \
            * dn_out_norm[l].astype(jnp.float32) * jax.nn.silu(zf)
        y_delta = o.reshape(Bsz, S, DN_WIDTH).astype(x.dtype) @ w_out_delta[l]

        merged = jax.nn.sigmoid(gate_a) * y_conv + jax.nn.sigmoid(gate_b) * y_delta
        h = h + merged @ w_o[l]

        h = h + memory_cross_attention(rms_norm(h, xattn_norm[l]), rms_norm(mem, mem_norm[l]),
                                       xattn_wq[l], xattn_wk[l], xattn_wv[l], xattn_wo[l])

        h = h + 0.5 * swiglu(rms_norm(h, ffn2_norm[l]), ffn2_w_gate[l], ffn2_w_up[l], ffn2_w_down[l])
    return rms_norm(h, final_norm)
```

```python
import functools

import jax
import jax.numpy as jnp
from jax import lax
from jax.experimental import pallas as pl
from jax.experimental.pallas import tpu as pltpu

EPS = 1e-6
BF = jnp.bfloat16
F32 = jnp.float32
HI = lax.Precision.HIGHEST

DN_HEAD_DIM = 128
DN_CHUNK = 64
XATTN_HEAD_DIM = 128
HALO = 16
VMEM_LIMIT = 60 * 1024 * 1024


def _cparams(*sem):
    return pltpu.CompilerParams(dimension_semantics=sem, vmem_limit_bytes=VMEM_LIMIT)


def _tile(dim, pref):
    t = min(pref, dim)
    while dim % t:
        t //= 2
    return t


def _rms(x, w):
    return x * lax.rsqrt(jnp.mean(x * x, axis=-1, keepdims=True) + EPS) * w


def _silu(x):
    return x * jax.nn.sigmoid(x)


def _dot(a, b):
    return jnp.dot(a, b, preferred_element_type=F32)


def _dot_nt(a, b, precision=None):
    return lax.dot_general(a, b, (((1,), (1,)), ((), ())), preferred_element_type=F32,
                           precision=precision)


def _ffn_kernel(x_ref, nw_ref, wg_ref, wu_ref, wd_ref, ow_ref, *refs, n_f, emit_h):
    if emit_h:
        h_ref, on_ref, xn_ref = refs
        acc_ref = h_ref
    else:
        on_ref, xn_ref = refs
        acc_ref = on_ref
    f = pl.program_id(1)

    @pl.when(f == 0)
    def _():
        xn_ref[...] = _rms(x_ref[...], nw_ref[...]).astype(BF)
        acc_ref[...] = jnp.zeros_like(acc_ref)

    xn = xn_ref[...]
    g = _dot(xn, wg_ref[...])
    u = _dot(xn, wu_ref[...])
    a = (_silu(g) * u).astype(BF)
    acc_ref[...] += _dot(a, wd_ref[...])

    @pl.when(f == n_f - 1)
    def _():
        h = x_ref[...] + 0.5 * acc_ref[...]
        if emit_h:
            h_ref[...] = h
        on_ref[...] = _rms(h, ow_ref[...]).astype(on_ref.dtype)


def _ffn(x, nw, wg, wu, wd, ow, *, emit_h, out_dtype, tm_pref=512, tf_pref=256):
    M, D = x.shape
    F = wg.shape[1]
    tm, tf = _tile(M, tm_pref), _tile(F, tf_pref)
    n_f = F // tf
    row = lambda i, f: (i, 0)
    vec = pl.BlockSpec((1, D), lambda i, f: (0, 0))
    out_shape = [jax.ShapeDtypeStruct((M, D), out_dtype)]
    out_specs = [pl.BlockSpec((tm, D), row)]
    if emit_h:
        out_shape.insert(0, jax.ShapeDtypeStruct((M, D), F32))
        out_specs.insert(0, pl.BlockSpec((tm, D), row))
    return pl.pallas_call(
        functools.partial(_ffn_kernel, n_f=n_f, emit_h=emit_h),
        grid=(M // tm, n_f),
        in_specs=[
            pl.BlockSpec((tm, D), row, pipeline_mode=pl.Buffered(1)),
            vec,
            pl.BlockSpec((D, tf), lambda i, f: (0, f)),
            pl.BlockSpec((D, tf), lambda i, f: (0, f)),
            pl.BlockSpec((tf, D), lambda i, f: (f, 0)),
            vec,
        ],
        out_specs=out_specs,
        out_shape=out_shape,
        scratch_shapes=[pltpu.VMEM((tm, D), BF)],
        compiler_params=_cparams("parallel", "arbitrary"),
        name="ffn_h" if emit_h else "ffn_out",
    )(x, nw, wg, wu, wd, ow)


def _mm_kernel(a_ref, w_ref, o_ref):
    o_ref[...] = _dot(a_ref[...], w_ref[...]).astype(o_ref.dtype)


def _matmul(a, w, out_dtype, *, tm_pref=1024, tn_pref=512, name="matmul"):
    M, K = a.shape
    N = w.shape[1]
    tm, tn = _tile(M, tm_pref), _tile(N, tn_pref)
    return pl.pallas_call(
        _mm_kernel,
        grid=(M // tm, N // tn),
        in_specs=[pl.BlockSpec((tm, K), lambda i, j: (i, 0)),
                  pl.BlockSpec((K, tn), lambda i, j: (0, j))],
        out_specs=pl.BlockSpec((tm, tn), lambda i, j: (i, j)),
        out_shape=jax.ShapeDtypeStruct((M, N), out_dtype),
        compiler_params=_cparams("parallel", "arbitrary"),
        name=name,
    )(a, w)


def _gates_kernel(ba_ref, eb_ref, ea_ref, alog_ref, dtb_ref, beta_ref, gc_ref):
    R = ba_ref.shape[0]
    ba = ba_ref[...]
    b = jnp.dot(ba, eb_ref[...], preferred_element_type=F32, precision=HI)
    a = jnp.dot(ba, ea_ref[...], preferred_element_type=F32, precision=HI)
    beta_ref[...] = jax.nn.sigmoid(b)
    x = a + dtb_ref[...]
    softplus = jnp.maximum(x, 0.0) + jnp.log1p(jnp.exp(-jnp.abs(x)))
    g = -jnp.exp(alog_ref[...]) * softplus
    ri = lax.broadcasted_iota(jnp.int32, (R, R), 0)
    ci = lax.broadcasted_iota(jnp.int32, (R, R), 1)
    tri = ((ri // DN_CHUNK == ci // DN_CHUNK) & (ri >= ci)).astype(F32)
    gc_ref[...] = jnp.dot(tri, g, preferred_element_type=F32, precision=HI)


def _gates(ba, a_log, dt_bias, n_heads, tile_rows):
    M = ba.shape[0]
    W = n_heads * DN_HEAD_DIM
    lane_head = jnp.arange(W) // DN_HEAD_DIM
    rows = jnp.arange(128)[:, None]
    eb = (rows == lane_head[None, :]).astype(F32)
    ea = (rows == lane_head[None, :] + n_heads).astype(F32)
    rep = lambda v: jnp.repeat(v.astype(F32), DN_HEAD_DIM)[None, :]
    R = tile_rows
    full = lambda shape: pl.BlockSpec(shape, lambda i: (0, 0))
    return pl.pallas_call(
        _gates_kernel,
        grid=(M // R,),
        in_specs=[pl.BlockSpec((R, 128), lambda i: (i, 0)), full((128, W)), full((128, W)),
                  full((1, W)), full((1, W))],
        out_specs=[pl.BlockSpec((R, W), lambda i: (i, 0))] * 2,
        out_shape=[jax.ShapeDtypeStruct((M, W), F32)] * 2,
        compiler_params=_cparams("parallel"),
        name="dn_gates",
    )(ba, eb, ea, rep(a_log), rep(dt_bias))


def _shift_rows(t, halo, k):
    r = pltpu.roll(t, k, 0)
    hr = pltpu.roll(halo, k, 0)[:8]
    rows = lax.broadcasted_iota(jnp.int32, (8, t.shape[1]), 0)
    first = jnp.where(rows < k, hr, r[:8])
    return jnp.concatenate([first, r[8:]], axis=0)


def _causal_conv(t, halo, w):
    K = w.shape[0]
    y = t * w[K - 1:K]
    for d in range(1, K):
        y = y + _shift_rows(t, halo, d) * w[K - 1 - d:K - d]
    return y


def _halo_spec(rows_per_tile, width, col_block, n_lead):
    per = rows_per_tile // HALO
    if n_lead == 1:
        return pl.BlockSpec((HALO, width), lambda i: (jnp.maximum(i * per - 1, 0), col_block))
    return pl.BlockSpec((HALO, width), lambda i, h: (jnp.maximum(i * per - 1, 0), col_block(h)))


def _conv_a_kernel(cx_ref, cc_ref, cb_ref, hx_ref, hc_ref, w_ref, o_ref, *, tiles_per_seq):
    first = (pl.program_id(0) % tiles_per_seq == 0)
    keep = jnp.where(first, 0.0, 1.0)
    t = cx_ref[...].astype(F32) * cc_ref[...].astype(F32)
    halo = hx_ref[...].astype(F32) * hc_ref[...].astype(F32) * keep
    y = _causal_conv(t, halo, w_ref[...])
    o_ref[...] = (cb_ref[...].astype(F32) * y).astype(o_ref.dtype)


def _conv_a(p, conv_w, seq, width, tr_pref=256):
    M = p.shape[0]
    tr = _tile(seq, tr_pref)
    blk = lambda c: pl.BlockSpec((tr, width), lambda i: (i, c))
    return pl.pallas_call(
        functools.partial(_conv_a_kernel, tiles_per_seq=seq // tr),
        grid=(M // tr,),
        in_specs=[blk(0), blk(1), blk(2), _halo_spec(tr, width, 0, 1), _halo_spec(tr, width, 1, 1),
                  pl.BlockSpec(conv_w.shape, lambda i: (0, 0))],
        out_specs=pl.BlockSpec((tr, width), lambda i: (i, 0)),
        out_shape=jax.ShapeDtypeStruct((M, width), BF),
        compiler_params=_cparams("parallel"),
        name="conv_a",
    )(p, p, p, p, p, conv_w)


def _dn_prep_kernel(q_ref, k_ref, v_ref, hq_ref, hk_ref, hv_ref, wq_ref, wk_ref, wv_ref,
                    beta_ref, gc_ref, qeff_ref, wdec_ref, ktail_ref, ubase_ref, obase_ref,
                    *, tiles_per_seq):
    R, DK = q_ref.shape
    C = DN_CHUNK
    first = (pl.program_id(0) % tiles_per_seq == 0)
    keep = jnp.where(first, 0.0, 1.0)

    def conv_silu(t_ref, h_ref, w_ref):
        y = _causal_conv(t_ref[...].astype(F32), h_ref[...].astype(F32) * keep, w_ref[...])
        return _silu(y)

    def l2n(x):
        return x * lax.rsqrt(jnp.sum(x * x, axis=-1, keepdims=True) + EPS)

    q = l2n(conv_silu(q_ref, hq_ref, wq_ref)) * (DK ** -0.5)
    k = l2n(conv_silu(k_ref, hk_ref, wk_ref))
    v = conv_silu(v_ref, hv_ref, wv_ref)
    beta = beta_ref[...]
    gc = gc_ref[...]

    ri = lax.broadcasted_iota(jnp.int32, (R, R), 0)
    ci = lax.broadcasted_iota(jnp.int32, (R, R), 1)
    same = (ri // C) == (ci // C)
    incl = same & (ri >= ci)
    strict = same & (ri > ci)
    lane0 = (lax.broadcasted_iota(jnp.int32, (R, DK), 1) == 0).astype(F32)
    g_row = _dot_nt(lane0, gc, precision=HI)
    g_col = jnp.concatenate([gc] * (R // DK), axis=1)
    decay = jnp.where(incl, jnp.exp(jnp.where(incl, g_col - g_row, 0.0)), 0.0)
    last = (ci == (ri // C) * C + (C - 1)).astype(F32)
    gc_last = jnp.dot(last, gc, preferred_element_type=F32, precision=HI)

    kb = k * beta
    vb = v * beta
    k16 = k.astype(BF)
    lmat = jnp.where(strict, _dot_nt(kb.astype(BF), k16) * decay, 0.0)
    qm = -lmat
    lp = lmat
    for _ in range(C.bit_length() - 2):
        lp16 = lp.astype(BF)
        lp = _dot(lp16, lp16)
        qm = qm + lp + _dot(qm.astype(BF), lp.astype(BF))
    rhs = jnp.concatenate([vb, kb * jnp.exp(gc)], axis=1)
    sol = rhs + _dot(qm.astype(BF), rhs.astype(BF))
    ubase, wdec = sol[:, :DK], sol[:, DK:]

    aqk = jnp.where(incl, _dot_nt(q.astype(BF), k16) * decay, 0.0)
    x = _dot(aqk.astype(BF), jnp.concatenate([wdec, ubase], axis=1).astype(BF))
    qeff_ref[...] = (q * jnp.exp(gc) - x[:, :DK]).astype(qeff_ref.dtype)
    obase_ref[...] = x[:, DK:]
    ubase_ref[...] = ubase
    wdec_ref[...] = wdec.astype(wdec_ref.dtype)
    ktail_ref[...] = (k * jnp.exp(gc_last - gc)).astype(ktail_ref.dtype)


def _dn_prep(p, qkv_conv_w, beta_b, gc_b, seq, n_heads, q_off):
    M = p.shape[0]
    DK = DN_HEAD_DIM
    W = n_heads * DK
    R = _tile(seq, 256)
    qb = q_off // DK
    cols = [lambda h, o=o: qb + o * n_heads + h for o in range(3)]
    blk = lambda c: pl.BlockSpec((R, DK), lambda i, h: (i, c(h)))
    wblk = lambda o: pl.BlockSpec((qkv_conv_w.shape[0], DK), lambda i, h: (0, o * n_heads + h))
    head = pl.BlockSpec((R, DK), lambda i, h: (i, h))
    return pl.pallas_call(
        functools.partial(_dn_prep_kernel, tiles_per_seq=seq // R),
        grid=(M // R, n_heads),
        in_specs=[blk(cols[0]), blk(cols[1]), blk(cols[2]),
                  _halo_spec(R, DK, cols[0], 2), _halo_spec(R, DK, cols[1], 2),
                  _halo_spec(R, DK, cols[2], 2),
                  wblk(0), wblk(1), wblk(2), head, head],
        out_specs=[head] * 5,
        out_shape=[jax.ShapeDtypeStruct((M, W), BF)] * 3 + [jax.ShapeDtypeStruct((M, W), F32)] * 2,
        compiler_params=_cparams("parallel", "parallel"),
        name="dn_prep",
    )(p, p, p, p, p, p, qkv_conv_w, qkv_conv_w, qkv_conv_w, beta_b, gc_b)


def _dn_scan_kernel(qeff_ref, wdec_ref, ktail_ref, ubase_ref, obase_ref, gc_ref, z_ref, nw_ref,
                    o_ref, state_ref, *, heads_per_step):
    S_len = qeff_ref.shape[0]
    C, DK = DN_CHUNK, DN_HEAD_DIM
    state_ref[...] = jnp.zeros_like(state_ref)
    nw = nw_ref[...]

    def chunk(c, carry):
        r0 = pl.multiple_of(c * C, C)
        rows = pl.ds(r0, C)
        for h in range(heads_per_step):
            cols = slice(h * DK, (h + 1) * DK)
            st = state_ref[h]
            lhs = jnp.concatenate([qeff_ref[rows, cols], wdec_ref[rows, cols]], axis=0)
            r = _dot(lhs, st.astype(BF))
            o = obase_ref[rows, cols] + r[:C]
            u = ubase_ref[rows, cols] - r[C:]
            tail = pl.ds(pl.multiple_of(r0 + (C - 8), 8), 8)
            g_last = jnp.exp(gc_ref[tail, cols][7:8])
            upd = lax.dot_general(ktail_ref[rows, cols], u.astype(BF), (((0,), (0,)), ((), ())),
                                  preferred_element_type=F32)
            state_ref[h] = st * g_last + upd
            zf = z_ref[rows, cols].astype(F32)
            on = o * lax.rsqrt(jnp.mean(o * o, axis=-1, keepdims=True) + EPS) * nw * _silu(zf)
            o_ref[rows, cols] = on.astype(o_ref.dtype)
        return carry

    lax.fori_loop(0, S_len // C, chunk, 0)


def _dn_scan(qeff, wdec, ktail, ubase, obase, gc_b, p, dn_out_norm, seq, n_heads, z_off,
             heads_per_step=2):
    M, W = qeff.shape
    G = heads_per_step
    GW = G * DN_HEAD_DIM
    zb = z_off // GW
    blk = pl.BlockSpec((seq, GW), lambda b, g: (b, g))
    return pl.pallas_call(
        functools.partial(_dn_scan_kernel, heads_per_step=G),
        grid=(M // seq, n_heads // G),
        in_specs=[blk] * 6 + [pl.BlockSpec((seq, GW), lambda b, g: (b, zb + g)),
                              pl.BlockSpec((1, DN_HEAD_DIM), lambda b, g: (0, 0))],
        out_specs=blk,
        out_shape=jax.ShapeDtypeStruct((M, W), BF),
        scratch_shapes=[pltpu.VMEM((G, DN_HEAD_DIM, DN_HEAD_DIM), F32)],
        compiler_params=_cparams("parallel", "parallel"),
        name="dn_scan",
    )(qeff, wdec, ktail, ubase, obase, gc_b, p, dn_out_norm)


def _merge_kernel(ya_ref, yb_ref, wa_ref, wb_ref, ga_ref, gb_ref, o_ref):
    ya = _dot(ya_ref[...], wa_ref[...])
    yb = _dot(yb_ref[...], wb_ref[...])
    m = jax.nn.sigmoid(ga_ref[...].astype(F32)) * ya + jax.nn.sigmoid(gb_ref[...].astype(F32)) * yb
    o_ref[...] = m.astype(o_ref.dtype)


def _merge(ya, yb, wa, wb, p, ga_off, gb_off, tm_pref=1024, tn_pref=512):
    M, KA = ya.shape
    KB = yb.shape[1]
    N = wa.shape[1]
    tm = _tile(M, tm_pref)
    tn = _tile(N, tn_pref)
    while ga_off % tn or gb_off % tn:
        tn //= 2
    ga, gb = ga_off // tn, gb_off // tn
    return pl.pallas_call(
        _merge_kernel,
        grid=(M // tm, N // tn),
        in_specs=[pl.BlockSpec((tm, KA), lambda i, j: (i, 0)),
                  pl.BlockSpec((tm, KB), lambda i, j: (i, 0)),
                  pl.BlockSpec((KA, tn), lambda i, j: (0, j)),
                  pl.BlockSpec((KB, tn), lambda i, j: (0, j)),
                  pl.BlockSpec((tm, tn), lambda i, j: (i, ga + j)),
                  pl.BlockSpec((tm, tn), lambda i, j: (i, gb + j))],
        out_specs=pl.BlockSpec((tm, tn), lambda i, j: (i, j)),
        out_shape=jax.ShapeDtypeStruct((M, N), BF),
        compiler_params=_cparams("parallel", "arbitrary"),
        name="merge",
    )(ya, yb, wa, wb, p, p)


def _oproj_kernel(a_ref, w_ref, res_ref, nw_ref, h_ref, hn_ref, *, n_k):
    kk = pl.program_id(1)

    @pl.when(kk == 0)
    def _():
        h_ref[...] = res_ref[...]

    h_ref[...] += _dot(a_ref[...], w_ref[...])

    @pl.when(kk == n_k - 1)
    def _():
        hn_ref[...] = _rms(h_ref[...], nw_ref[...]).astype(hn_ref.dtype)


def _oproj(a, w, res, nw, tm_pref=512, tk_pref=512):
    M, K = a.shape
    N = w.shape[1]
    tm, tk = _tile(M, tm_pref), _tile(K, tk_pref)
    n_k = K // tk
    row = lambda i, k: (i, 0)
    return pl.pallas_call(
        functools.partial(_oproj_kernel, n_k=n_k),
        grid=(M // tm, n_k),
        in_specs=[pl.BlockSpec((tm, tk), lambda i, k: (i, k)),
                  pl.BlockSpec((tk, N), lambda i, k: (k, 0)),
                  pl.BlockSpec((tm, N), row, pipeline_mode=pl.Buffered(1)),
                  pl.BlockSpec((1, N), lambda i, k: (0, 0))],
        out_specs=[pl.BlockSpec((tm, N), row), pl.BlockSpec((tm, N), row)],
        out_shape=[jax.ShapeDtypeStruct((M, N), F32), jax.ShapeDtypeStruct((M, N), BF)],
        compiler_params=_cparams("parallel", "arbitrary"),
        name="oproj",
    )(a, w, res, nw)


def _mem_kv_kernel(m_ref, nw_ref, wk_ref, wv_ref, k_ref, v_ref):
    mn = _rms(m_ref[...], nw_ref[...]).astype(BF)
    k_ref[...] = _dot(mn, wk_ref[...]).astype(k_ref.dtype)
    v_ref[...] = _dot(mn, wv_ref[...]).astype(v_ref.dtype)


def _mem_kv(mem, nw, wk, wv, tr_pref=256):
    Mm, D = mem.shape
    N = wk.shape[1]
    tr = _tile(Mm, tr_pref)
    full = lambda shape: pl.BlockSpec(shape, lambda i: (0, 0))
    return pl.pallas_call(
        _mem_kv_kernel,
        grid=(Mm // tr,),
        in_specs=[pl.BlockSpec((tr, D), lambda i: (i, 0)), full((1, D)), full((D, N)), full((D, N))],
        out_specs=[pl.BlockSpec((tr, N), lambda i: (i, 0))] * 2,
        out_shape=[jax.ShapeDtypeStruct((Mm, N), BF)] * 2,
        compiler_params=_cparams("parallel"),
        name="mem_kv",
    )(mem, nw, wk, wv)


def _xattn_kernel(hn_ref, h_ref, wq_ref, k_ref, v_ref, wo_ref, o_ref, *, n_heads):
    DH = XATTN_HEAD_DIM
    q = _dot(hn_ref[...], wq_ref[...])
    outs = []
    for hd in range(n_heads):
        cols = slice(hd * DH, (hd + 1) * DH)
        s = _dot_nt(q[:, cols].astype(BF), k_ref[:, cols]) * (DH ** -0.5)
        s = s - jnp.max(s, axis=-1, keepdims=True)
        e = jnp.exp(s)
        pr = e / jnp.sum(e, axis=-1, keepdims=True)
        outs.append(_dot(pr.astype(BF), v_ref[:, cols]))
    o = jnp.concatenate(outs, axis=1).astype(BF)
    o_ref[...] = h_ref[...] + _dot(o, wo_ref[...])


def _xattn(hn, h, wq, kh, vh, wo, seq, n_mem, tm_pref=256):
    M, D = hn.shape
    N = wq.shape[1]
    tm = _tile(seq, tm_pref)
    per = seq // tm
    full = lambda shape: pl.BlockSpec(shape, lambda i: (0, 0))
    kv = pl.BlockSpec((n_mem, N), lambda i: (i // per, 0))
    return pl.pallas_call(
        functools.partial(_xattn_kernel, n_heads=N // XATTN_HEAD_DIM),
        grid=(M // tm,),
        in_specs=[pl.BlockSpec((tm, D), lambda i: (i, 0)), pl.BlockSpec((tm, D), lambda i: (i, 0)),
                  full((D, N)), kv, kv, full((N, D))],
        out_specs=pl.BlockSpec((tm, D), lambda i: (i, 0)),
        out_shape=jax.ShapeDtypeStruct((M, D), F32),
        compiler_params=_cparams("parallel"),
        name="xattn",
    )(hn, h, wq, kh, vh, wo)


def kernel(x, mem, ffn1_norm, ffn1_w_gate, ffn1_w_up, ffn1_w_down, mix_norm, w_in, conv_w, qkv_conv_w, a_log, dt_bias, dn_out_norm, w_out_conv, w_out_delta, w_o, xattn_norm, mem_norm, xattn_wq, xattn_wk, xattn_wv, xattn_wo, ffn2_norm, ffn2_w_gate, ffn2_w_up, ffn2_w_down, final_norm):
    B, S, D = x.shape
    depth = ffn1_norm.shape[0]
    n_mem = mem.shape[1]
    CW = conv_w.shape[-1]
    H = a_log.shape[-1]
    DW = H * DN_HEAD_DIM
    M = B * S
    assert S % DN_CHUNK == 0 and 2 * H <= 128
    o_qkv = 3 * CW
    o_z = o_qkv + 3 * DW
    o_b = o_z + DW
    o_ga = o_b + 2 * H
    o_gb = o_ga + D
    assert w_in.shape[-1] == o_gb + D
    vec = lambda v: v.reshape(1, -1).astype(F32)
    bf = lambda w: w.astype(BF)

    h = x.reshape(M, D)
    mem2 = mem.reshape(B * n_mem, D)
    y = None
    for l in range(depth):
        last = l == depth - 1
        h, un = _ffn(h, vec(ffn1_norm[l]), bf(ffn1_w_gate[l]), bf(ffn1_w_up[l]), bf(ffn1_w_down[l]),
                     vec(mix_norm[l]), emit_h=True, out_dtype=BF)

        w_main = bf(jnp.concatenate([w_in[l][:, :o_b], w_in[l][:, o_ga:]], axis=1))
        w_ba = bf(jnp.pad(w_in[l][:, o_b:o_ga], ((0, 0), (0, 128 - 2 * H))))
        p = _matmul(un, w_main, BF, name="in_proj")
        ba = _matmul(un, w_ba, F32, name="in_proj_ba")

        ya = _conv_a(p, conv_w[l].astype(F32), S, CW)
        beta_b, gc_b = _gates(ba, a_log[l], dt_bias[l], H, _tile(S, 256))
        qeff, wdec, ktail, ubase, obase = _dn_prep(p, qkv_conv_w[l].astype(F32), beta_b, gc_b, S, H, o_qkv)
        yb = _dn_scan(qeff, wdec, ktail, ubase, obase, gc_b, p, vec(dn_out_norm[l]), S, H, o_z)

        merged = _merge(ya, yb, bf(w_out_conv[l]), bf(w_out_delta[l]), p, o_b, o_b + D)
        h, hn = _oproj(merged, bf(w_o[l]), h, vec(xattn_norm[l]))

        kh, vh = _mem_kv(mem2, vec(mem_norm[l]), bf(xattn_wk[l]), bf(xattn_wv[l]))
        h = _xattn(hn, h, bf(xattn_wq[l]), kh, vh, bf(xattn_wo[l]), S, n_mem)

        if last:
            y = _ffn(h, vec(ffn2_norm[l]), bf(ffn2_w_gate[l]), bf(ffn2_w_up[l]), bf(ffn2_w_down[l]),
                     vec(final_norm), emit_h=False, out_dtype=F32)[0]
        else:
            h = _ffn(h, vec(ffn2_norm[l]), bf(ffn2_w_gate[l]), bf(ffn2_w_up[l]), bf(ffn2_w_down[l]),
                     vec(final_norm), emit_h=True, out_dtype=F32)[0]
    return y.reshape(B, S, D)
```

```python
import functools

import jax
import jax.numpy as jnp
from jax import lax
from jax.experimental import pallas as pl
from jax.experimental.pallas import tpu as pltpu

EPS = 1e-6
BF = jnp.bfloat16
F32 = jnp.float32
HI = lax.Precision.HIGHEST

DN_HEAD_DIM = 128
DN_CHUNK = 64
XATTN_HEAD_DIM = 128
HALO = 16
SUBLANES = 8
VMEM_LIMIT = 60 * 1024 * 1024


def _cparams(*sem):
    return pltpu.CompilerParams(dimension_semantics=sem, vmem_limit_bytes=VMEM_LIMIT)


def _tile(dim, pref):
    t = min(pref, dim)
    while dim % t:
        t //= 2
    return t


def _rms(x, w):
    return x * lax.rsqrt(jnp.mean(x * x, axis=-1, keepdims=True) + EPS) * w


def _silu(x):
    return x * jax.nn.sigmoid(x)


def _dot(a, b):
    return jnp.dot(a, b, preferred_element_type=F32)


def _dot_nt(a, b, precision=None):
    return lax.dot_general(a, b, (((1,), (1,)), ((), ())), preferred_element_type=F32,
                           precision=precision)


def _ffn_kernel(x_ref, nw_ref, wg_ref, wu_ref, wd_ref, ow_ref, *refs, n_f, emit_h):
    if emit_h:
        h_ref, on_ref, xn_ref = refs
        acc_ref = h_ref
    else:
        on_ref, xn_ref = refs
        acc_ref = on_ref
    f = pl.program_id(1)

    @pl.when(f == 0)
    def _():
        xn_ref[...] = _rms(x_ref[...], nw_ref[...]).astype(BF)
        acc_ref[...] = jnp.zeros_like(acc_ref)

    xn = xn_ref[...]
    g = _dot(xn, wg_ref[...])
    u = _dot(xn, wu_ref[...])
    a = (_silu(g) * u).astype(BF)
    acc_ref[...] += _dot(a, wd_ref[...])

    @pl.when(f == n_f - 1)
    def _():
        h = x_ref[...] + 0.5 * acc_ref[...]
        if emit_h:
            h_ref[...] = h
        on_ref[...] = _rms(h, ow_ref[...]).astype(on_ref.dtype)


def _ffn(x, nw, wg, wu, wd, ow, *, emit_h, out_dtype, tm_pref=512, tf_pref=256):
    M, D = x.shape
    F = wg.shape[1]
    tm, tf = _tile(M, tm_pref), _tile(F, tf_pref)
    n_f = F // tf
    row = lambda i, f: (i, 0)
    vec = pl.BlockSpec((1, D), lambda i, f: (0, 0))
    out_shape = [jax.ShapeDtypeStruct((M, D), out_dtype)]
    out_specs = [pl.BlockSpec((tm, D), row)]
    if emit_h:
        out_shape.insert(0, jax.ShapeDtypeStruct((M, D), F32))
        out_specs.insert(0, pl.BlockSpec((tm, D), row))
    return pl.pallas_call(
        functools.partial(_ffn_kernel, n_f=n_f, emit_h=emit_h),
        grid=(M // tm, n_f),
        in_specs=[
            pl.BlockSpec((tm, D), row, pipeline_mode=pl.Buffered(1)),
            vec,
            pl.BlockSpec((D, tf), lambda i, f: (0, f)),
            pl.BlockSpec((D, tf), lambda i, f: (0, f)),
            pl.BlockSpec((tf, D), lambda i, f: (f, 0)),
            vec,
        ],
        out_specs=out_specs,
        out_shape=out_shape,
        scratch_shapes=[pltpu.VMEM((tm, D), BF)],
        compiler_params=_cparams("parallel", "arbitrary"),
        name="ffn_h" if emit_h else "ffn_out",
    )(x, nw, wg, wu, wd, ow)


def _mm_kernel(a_ref, w_ref, *refs, residual):
    if residual:
        res_ref, o_ref = refs
        o_ref[...] = res_ref[...] + _dot(a_ref[...], w_ref[...])
    else:
        (o_ref,) = refs
        o_ref[...] = _dot(a_ref[...], w_ref[...]).astype(o_ref.dtype)


def _matmul(a, w, out_dtype, *, res=None, tm_pref=1024, tn_pref=512, name="matmul"):
    M, K = a.shape
    N = w.shape[1]
    tm, tn = _tile(M, tm_pref), _tile(N, tn_pref)
    out_blk = pl.BlockSpec((tm, tn), lambda i, j: (i, j))
    in_specs = [pl.BlockSpec((tm, K), lambda i, j: (i, 0)),
                pl.BlockSpec((K, tn), lambda i, j: (0, j))]
    args = [a, w]
    if res is not None:
        in_specs.append(out_blk)
        args.append(res)
    return pl.pallas_call(
        functools.partial(_mm_kernel, residual=res is not None),
        grid=(M // tm, N // tn),
        in_specs=in_specs,
        out_specs=out_blk,
        out_shape=jax.ShapeDtypeStruct((M, N), out_dtype),
        compiler_params=_cparams("parallel", "arbitrary"),
        name=name,
    )(*args)


def _mm_ws_kernel(a_ref, w_ref, o_ref, wbf_ref):
    @pl.when(pl.program_id(1) == 0)
    def _():
        wbf_ref[...] = w_ref[...].astype(BF)

    o_ref[...] = _dot(a_ref[...], wbf_ref[...]).astype(o_ref.dtype)


def _matmul_ws(a, w, n_cols, out_dtype, *, tm_pref=1024, tn_pref=512, name="matmul_ws"):
    M, K = a.shape
    tm, tn = _tile(M, tm_pref), _tile(n_cols, tn_pref)
    return pl.pallas_call(
        _mm_ws_kernel,
        grid=(n_cols // tn, M // tm),
        in_specs=[pl.BlockSpec((tm, K), lambda j, i: (i, 0)),
                  pl.BlockSpec((K, tn), lambda j, i: (0, j))],
        out_specs=pl.BlockSpec((tm, tn), lambda j, i: (i, j)),
        out_shape=jax.ShapeDtypeStruct((M, n_cols), out_dtype),
        scratch_shapes=[pltpu.VMEM((K, tn), BF)],
        compiler_params=_cparams("parallel", "arbitrary"),
        name=name,
    )(a, w)


def _gates_kernel(ba_ref, eb_ref, ea_ref, sel_ref, alog_ref, dtb_ref, beta_ref, gc_ref, gct_ref):
    R = ba_ref.shape[0]
    ba = ba_ref[...]
    b = jnp.dot(ba, eb_ref[...], preferred_element_type=F32, precision=HI)
    a = jnp.dot(ba, ea_ref[...], preferred_element_type=F32, precision=HI)
    beta_ref[...] = jax.nn.sigmoid(b)
    x = a + dtb_ref[...]
    softplus = jnp.maximum(x, 0.0) + jnp.log1p(jnp.exp(-jnp.abs(x)))
    g = -jnp.exp(alog_ref[...]) * softplus
    ri = lax.broadcasted_iota(jnp.int32, (R, R), 0)
    ci = lax.broadcasted_iota(jnp.int32, (R, R), 1)
    tri = ((ri // DN_CHUNK == ci // DN_CHUNK) & (ri >= ci)).astype(F32)
    gc = jnp.dot(tri, g, preferred_element_type=F32, precision=HI)
    gc_ref[...] = gc
    gct_ref[...] = _dot_nt(sel_ref[...], gc, precision=HI)


def _gates(ba, a_log, dt_bias, n_heads, heads_per_group, tile_rows):
    M = ba.shape[0]
    G = heads_per_group
    W = n_heads * DN_HEAD_DIM
    lane = jnp.arange(W)
    lane_head = lane // DN_HEAD_DIM
    rows = jnp.arange(128)[:, None]
    eb = (rows == lane_head[None, :]).astype(F32)
    ea = (rows == lane_head[None, :] + n_heads).astype(F32)
    n_rows = (n_heads // G) * SUBLANES
    r = jnp.arange(n_rows)
    head_of_row = jnp.where(r % SUBLANES < G, (r // SUBLANES) * G + r % SUBLANES, -1)
    sel = (lane[None, :] == head_of_row[:, None] * DN_HEAD_DIM).astype(F32)
    rep = lambda v: jnp.repeat(v.astype(F32), DN_HEAD_DIM)[None, :]
    R = tile_rows
    full = lambda shape: pl.BlockSpec(shape, lambda i: (0, 0))
    return pl.pallas_call(
        _gates_kernel,
        grid=(M // R,),
        in_specs=[pl.BlockSpec((R, 128), lambda i: (i, 0)), full((128, W)), full((128, W)),
                  full((n_rows, W)), full((1, W)), full((1, W))],
        out_specs=[pl.BlockSpec((R, W), lambda i: (i, 0))] * 2
        + [pl.BlockSpec((n_rows, R), lambda i: (0, i))],
        out_shape=[jax.ShapeDtypeStruct((M, W), F32)] * 2
        + [jax.ShapeDtypeStruct((n_rows, M), F32)],
        compiler_params=_cparams("parallel"),
        name="dn_gates",
    )(ba, eb, ea, sel, rep(a_log), rep(dt_bias))


def _shift_rows(t, halo, k):
    r = pltpu.roll(t, k, 0)
    hr = pltpu.roll(halo, k, 0)[:8]
    rows = lax.broadcasted_iota(jnp.int32, (8, t.shape[1]), 0)
    first = jnp.where(rows < k, hr, r[:8])
    return jnp.concatenate([first, r[8:]], axis=0)


def _causal_conv(t, halo, w):
    K = w.shape[0]
    y = t * w[K - 1:K]
    for d in range(1, K):
        y = y + _shift_rows(t, halo, d) * w[K - 1 - d:K - d]
    return y


def _halo_spec(rows_per_tile, width, col_block, n_lead):
    per = rows_per_tile // HALO
    if n_lead == 1:
        return pl.BlockSpec((HALO, width), lambda i: (jnp.maximum(i * per - 1, 0), col_block))
    return pl.BlockSpec((HALO, width), lambda i, g: (jnp.maximum(i * per - 1, 0), col_block(g)))


def _conv_a_kernel(cx_ref, cc_ref, cb_ref, hx_ref, hc_ref, w_ref, o_ref, *, tiles_per_seq):
    first = (pl.program_id(0) % tiles_per_seq == 0)
    keep = jnp.where(first, 0.0, 1.0)
    t = cx_ref[...].astype(F32) * cc_ref[...].astype(F32)
    halo = hx_ref[...].astype(F32) * hc_ref[...].astype(F32) * keep
    y = _causal_conv(t, halo, w_ref[...])
    o_ref[...] = (cb_ref[...].astype(F32) * y).astype(o_ref.dtype)


def _conv_a(p, conv_w, seq, width, tr_pref=256):
    M = p.shape[0]
    tr = _tile(seq, tr_pref)
    blk = lambda c: pl.BlockSpec((tr, width), lambda i: (i, c))
    return pl.pallas_call(
        functools.partial(_conv_a_kernel, tiles_per_seq=seq // tr),
        grid=(M // tr,),
        in_specs=[blk(0), blk(1), blk(2), _halo_spec(tr, width, 0, 1), _halo_spec(tr, width, 1, 1),
                  pl.BlockSpec(conv_w.shape, lambda i: (0, 0))],
        out_specs=pl.BlockSpec((tr, width), lambda i: (i, 0)),
        out_shape=jax.ShapeDtypeStruct((M, width), BF),
        compiler_params=_cparams("parallel"),
        name="conv_a",
    )(p, p, p, p, p, conv_w)


def _dn_prep_kernel(q_ref, k_ref, v_ref, hq_ref, hk_ref, hv_ref, wq_ref, wk_ref, wv_ref,
                    beta_ref, gc_ref, gct_ref, qeff_ref, wdec_ref, ktail_ref, ubase_ref, obase_ref,
                    *, tiles_per_seq, heads, wave):
    R = q_ref.shape[0]
    C, DK = DN_CHUNK, DN_HEAD_DIM
    first = (pl.program_id(0) % tiles_per_seq == 0)
    keep = jnp.where(first, 0.0, 1.0)
    ri = lax.broadcasted_iota(jnp.int32, (R, R), 0)
    ci = lax.broadcasted_iota(jnp.int32, (R, R), 1)
    same = (ri // C) == (ci // C)
    incl = same & (ri >= ci)
    strict = same & (ri > ci)

    def l2n(x):
        return x * lax.rsqrt(jnp.sum(x * x, axis=-1, keepdims=True) + EPS)

    def head_inputs(hh):
        cols = slice(hh * DK, (hh + 1) * DK)

        def conv_silu(t_ref, h_ref, w_ref):
            y = _causal_conv(t_ref[:, cols].astype(F32), h_ref[:, cols].astype(F32) * keep,
                             w_ref[:, cols])
            return _silu(y)

        q = l2n(conv_silu(q_ref, hq_ref, wq_ref)) * (DK ** -0.5)
        k = l2n(conv_silu(k_ref, hk_ref, wk_ref))
        v = conv_silu(v_ref, hv_ref, wv_ref)
        beta = beta_ref[:, cols]
        gc = gc_ref[:, cols]
        g_row = jnp.broadcast_to(gct_ref[hh:hh + 1, :], (R, R))
        g_col = jnp.concatenate([gc] * (R // DK), axis=1)
        decay = jnp.where(incl, jnp.exp(jnp.where(incl, g_col - g_row, 0.0)), 0.0)
        kb = k * beta
        k16 = k.astype(BF)
        lmat = jnp.where(strict, _dot_nt(kb.astype(BF), k16) * decay, 0.0)
        aqk = jnp.where(incl, _dot_nt(q.astype(BF), k16) * decay, 0.0).astype(BF)
        rhs = jnp.concatenate([v * beta, kb * jnp.exp(gc)], axis=1)
        return dict(cols=cols, q=q, k=k, gc=gc, aqk=aqk, rhs=rhs, m=-lmat)

    n_fac = C.bit_length() - 1
    for w0 in range(0, heads, wave):
        hs = [head_inputs(hh) for hh in range(w0, min(w0 + wave, heads))]
        for d in hs:
            m16 = d["m"].astype(BF)
            d["qm"] = d["m"]
            d["m"] = _dot(m16, m16)
        for j in range(1, n_fac):
            for d in hs:
                m16 = d["m"].astype(BF)
                if j < n_fac - 1:
                    r = _dot(jnp.concatenate([m16, d["qm"].astype(BF)], axis=0), m16)
                    d["qm"] = d["qm"] + d["m"] + r[R:]
                    d["m"] = r[:R]
                else:
                    d["qm"] = d["qm"] + d["m"] + _dot(d["qm"].astype(BF), m16)
        for d in hs:
            d["sol"] = d["rhs"] + _dot(d["qm"].astype(BF), d["rhs"].astype(BF))
        for d in hs:
            cols, sol, gc, q, k = d["cols"], d["sol"], d["gc"], d["q"], d["k"]
            ubase, wdec = sol[:, :DK], sol[:, DK:]
            x = _dot(d["aqk"], jnp.concatenate([wdec, ubase], axis=1).astype(BF))
            gc_last = jnp.concatenate(
                [jnp.broadcast_to(gc[c * C + C - 1:c * C + C, :], (C, DK)) for c in range(R // C)],
                axis=0)
            qeff_ref[:, cols] = (q * jnp.exp(gc) - x[:, :DK]).astype(qeff_ref.dtype)
            obase_ref[:, cols] = x[:, DK:]
            ubase_ref[:, cols] = ubase
            wdec_ref[:, cols] = wdec.astype(wdec_ref.dtype)
            ktail_ref[:, cols] = (k * jnp.exp(gc_last - gc)).astype(ktail_ref.dtype)


def _dn_prep(p, qkv_conv_w, beta_b, gc_b, gct, seq, n_heads, heads_per_group, q_off):
    M = p.shape[0]
    G = heads_per_group
    GW = G * DN_HEAD_DIM
    W = n_heads * DN_HEAD_DIM
    n_groups = n_heads // G
    R = _tile(seq, 256)
    assert q_off % GW == 0 and R % DN_HEAD_DIM == 0
    cols = [lambda g, o=o: q_off // GW + o * n_groups + g for o in range(3)]
    blk = lambda c: pl.BlockSpec((R, GW), lambda i, g: (i, c(g)))
    wblk = lambda o: pl.BlockSpec((qkv_conv_w.shape[0], GW), lambda i, g: (0, o * n_groups + g))
    head = pl.BlockSpec((R, GW), lambda i, g: (i, g))
    return pl.pallas_call(
        functools.partial(_dn_prep_kernel, tiles_per_seq=seq // R, heads=G, wave=min(G, 4)),
        grid=(M // R, n_groups),
        in_specs=[blk(cols[0]), blk(cols[1]), blk(cols[2]),
                  _halo_spec(R, GW, cols[0], 2), _halo_spec(R, GW, cols[1], 2),
                  _halo_spec(R, GW, cols[2], 2),
                  wblk(0), wblk(1), wblk(2), head, head,
                  pl.BlockSpec((SUBLANES, R), lambda i, g: (g, i))],
        out_specs=[head] * 5,
        out_shape=[jax.ShapeDtypeStruct((M, W), BF)] * 3 + [jax.ShapeDtypeStruct((M, W), F32)] * 2,
        compiler_params=_cparams("parallel", "parallel"),
        name="dn_prep",
    )(p, p, p, p, p, p, qkv_conv_w, qkv_conv_w, qkv_conv_w, beta_b, gc_b, gct)


def _dn_scan_kernel(qeff_ref, wdec_ref, ktail_ref, ubase_ref, obase_ref, gc_ref, z_ref, nw_ref,
                    o_ref, state_ref, *, heads):
    rows_per_step = qeff_ref.shape[0]
    C, DK = DN_CHUNK, DN_HEAD_DIM

    @pl.when(pl.program_id(2) == 0)
    def _():
        state_ref[...] = jnp.zeros_like(state_ref)

    nw = nw_ref[...]

    def chunk(c, carry):
        r0 = pl.multiple_of(c * C, C)
        rows = pl.ds(r0, C)
        tail = pl.ds(pl.multiple_of(r0 + (C - 8), 8), 8)
        for h in range(heads):
            cols = slice(h * DK, (h + 1) * DK)
            st = state_ref[h]
            lhs = jnp.concatenate([qeff_ref[rows, cols], wdec_ref[rows, cols]], axis=0)
            r = _dot(lhs, st.astype(BF))
            o = obase_ref[rows, cols] + r[:C]
            u = ubase_ref[rows, cols] - r[C:]
            g_last = jnp.exp(gc_ref[tail, cols][7:8])
            upd = lax.dot_general(ktail_ref[rows, cols], u.astype(BF), (((0,), (0,)), ((), ())),
                                  preferred_element_type=F32)
            state_ref[h] = st * g_last + upd
            zf = z_ref[rows, cols].astype(F32)
            on = o * lax.rsqrt(jnp.mean(o * o, axis=-1, keepdims=True) + EPS) * nw * _silu(zf)
            o_ref[rows, cols] = on.astype(o_ref.dtype)
        return carry

    lax.fori_loop(0, rows_per_step // C, chunk, 0)


def _dn_scan(qeff, wdec, ktail, ubase, obase, gc_b, p, dn_out_norm, seq, n_heads, heads_per_group,
             z_off, ts_pref=512):
    M, W = qeff.shape
    G = heads_per_group
    GW = G * DN_HEAD_DIM
    ts = _tile(seq, ts_pref)
    per = seq // ts
    assert z_off % GW == 0
    zb = z_off // GW
    blk = pl.BlockSpec((ts, GW), lambda b, g, t: (b * per + t, g))
    return pl.pallas_call(
        functools.partial(_dn_scan_kernel, heads=G),
        grid=(M // seq, n_heads // G, per),
        in_specs=[blk] * 6 + [pl.BlockSpec((ts, GW), lambda b, g, t: (b * per + t, zb + g)),
                              pl.BlockSpec((1, DN_HEAD_DIM), lambda b, g, t: (0, 0))],
        out_specs=blk,
        out_shape=jax.ShapeDtypeStruct((M, W), BF),
        scratch_shapes=[pltpu.VMEM((G, DN_HEAD_DIM, DN_HEAD_DIM), F32)],
        compiler_params=_cparams("parallel", "parallel", "arbitrary"),
        name="dn_scan",
    )(qeff, wdec, ktail, ubase, obase, gc_b, p, dn_out_norm)


def _merge_kernel(ya_ref, yb_ref, wa_ref, wb_ref, ga_ref, gb_ref, o_ref):
    ya = _dot(ya_ref[...], wa_ref[...])
    yb = _dot(yb_ref[...], wb_ref[...])
    m = jax.nn.sigmoid(ga_ref[...].astype(F32)) * ya + jax.nn.sigmoid(gb_ref[...].astype(F32)) * yb
    o_ref[...] = m.astype(o_ref.dtype)


def _merge(ya, yb, wa, wb, gates, tm_pref=1024, tn_pref=512):
    M, KA = ya.shape
    KB = yb.shape[1]
    N = wa.shape[1]
    tm, tn = _tile(M, tm_pref), _tile(N, tn_pref)
    nb = N // tn
    return pl.pallas_call(
        _merge_kernel,
        grid=(M // tm, nb),
        in_specs=[pl.BlockSpec((tm, KA), lambda i, j: (i, 0)),
                  pl.BlockSpec((tm, KB), lambda i, j: (i, 0)),
                  pl.BlockSpec((KA, tn), lambda i, j: (0, j)),
                  pl.BlockSpec((KB, tn), lambda i, j: (0, j)),
                  pl.BlockSpec((tm, tn), lambda i, j: (i, j)),
                  pl.BlockSpec((tm, tn), lambda i, j: (i, nb + j))],
        out_specs=pl.BlockSpec((tm, tn), lambda i, j: (i, j)),
        out_shape=jax.ShapeDtypeStruct((M, N), BF),
        compiler_params=_cparams("parallel", "arbitrary"),
        name="merge",
    )(ya, yb, wa, wb, gates, gates)


def _mem_kv_kernel(m_ref, nw_ref, wk_ref, wv_ref, k_ref, v_ref):
    mn = _rms(m_ref[...], nw_ref[...]).astype(BF)
    k_ref[...] = _dot(mn, wk_ref[...]).astype(k_ref.dtype)
    v_ref[...] = _dot(mn, wv_ref[...]).astype(v_ref.dtype)


def _mem_kv(mem, nw, wk, wv, tr_pref=256):
    Mm, D = mem.shape
    N = wk.shape[1]
    tr = _tile(Mm, tr_pref)
    full = lambda shape: pl.BlockSpec(shape, lambda i: (0, 0))
    return pl.pallas_call(
        _mem_kv_kernel,
        grid=(Mm // tr,),
        in_specs=[pl.BlockSpec((tr, D), lambda i: (i, 0)), full((1, D)), full((D, N)), full((D, N))],
        out_specs=[pl.BlockSpec((tr, N), lambda i: (i, 0))] * 2,
        out_shape=[jax.ShapeDtypeStruct((Mm, N), BF)] * 2,
        compiler_params=_cparams("parallel"),
        name="mem_kv",
    )(mem, nw, wk, wv)


def _xattn_kernel(h_ref, nw_ref, wq_ref, k_ref, v_ref, wo_ref, o_ref, *, n_heads):
    DH = XATTN_HEAD_DIM
    h = h_ref[...]
    q = _dot(_rms(h, nw_ref[...]).astype(BF), wq_ref[...])
    outs = []
    for hd in range(n_heads):
        cols = slice(hd * DH, (hd + 1) * DH)
        s = _dot_nt(q[:, cols].astype(BF), k_ref[:, cols]) * (DH ** -0.5)
        s = s - jnp.max(s, axis=-1, keepdims=True)
        e = jnp.exp(s)
        pr = e / jnp.sum(e, axis=-1, keepdims=True)
        outs.append(_dot(pr.astype(BF), v_ref[:, cols]))
    o = jnp.concatenate(outs, axis=1).astype(BF)
    o_ref[...] = h + _dot(o, wo_ref[...])


def _xattn(h, nw, wq, kh, vh, wo, seq, n_mem, tm_pref=256):
    M, D = h.shape
    N = wq.shape[1]
    tm = _tile(seq, tm_pref)
    per = seq // tm
    full = lambda shape: pl.BlockSpec(shape, lambda i: (0, 0))
    kv = pl.BlockSpec((n_mem, N), lambda i: (i // per, 0))
    return pl.pallas_call(
        functools.partial(_xattn_kernel, n_heads=N // XATTN_HEAD_DIM),
        grid=(M // tm,),
        in_specs=[pl.BlockSpec((tm, D), lambda i: (i, 0)), full((1, D)),
                  full((D, N)), kv, kv, full((N, D))],
        out_specs=pl.BlockSpec((tm, D), lambda i: (i, 0)),
        out_shape=jax.ShapeDtypeStruct((M, D), F32),
        compiler_params=_cparams("parallel"),
        name="xattn",
    )(h, nw, wq, kh, vh, wo)


def kernel(x, mem, ffn1_norm, ffn1_w_gate, ffn1_w_up, ffn1_w_down, mix_norm, w_in, conv_w, qkv_conv_w, a_log, dt_bias, dn_out_norm, w_out_conv, w_out_delta, w_o, xattn_norm, mem_norm, xattn_wq, xattn_wk, xattn_wv, xattn_wo, ffn2_norm, ffn2_w_gate, ffn2_w_up, ffn2_w_down, final_norm):
    B, S, D = x.shape
    depth = ffn1_norm.shape[0]
    n_mem = mem.shape[1]
    CW = conv_w.shape[-1]
    H = a_log.shape[-1]
    DW = H * DN_HEAD_DIM
    G = min(H, SUBLANES)
    M = B * S
    assert S % DN_CHUNK == 0 and 2 * H <= 128 and H % G == 0
    o_qkv = 3 * CW
    o_z = o_qkv + 3 * DW
    o_b = o_z + DW
    o_ga = o_b + 2 * H
    assert w_in.shape[-1] == o_ga + 2 * D
    vec = lambda v: v.reshape(1, -1).astype(F32)
    bf = lambda w: w.astype(BF)

    h = x.reshape(M, D)
    mem2 = mem.reshape(B * n_mem, D)
    for l in range(depth):
        h, un = _ffn(h, vec(ffn1_norm[l]), bf(ffn1_w_gate[l]), bf(ffn1_w_up[l]), bf(ffn1_w_down[l]),
                     vec(mix_norm[l]), emit_h=True, out_dtype=BF)

        p = _matmul_ws(un, w_in[l], o_b, BF, name="in_proj")
        ba = _matmul(un, bf(jnp.pad(w_in[l][:, o_b:o_ga], ((0, 0), (0, 128 - 2 * H)))), F32,
                     name="in_proj_ba")
        gates = _matmul(un, bf(w_in[l][:, o_ga:]), BF, name="in_proj_gates")

        ya = _conv_a(p, conv_w[l].astype(F32), S, CW)
        beta_b, gc_b, gct = _gates(ba, a_log[l], dt_bias[l], H, G, _tile(S, 256))
        qeff, wdec, ktail, ubase, obase = _dn_prep(p, qkv_conv_w[l].astype(F32), beta_b, gc_b, gct,
                                                   S, H, G, o_qkv)
        yb = _dn_scan(qeff, wdec, ktail, ubase, obase, gc_b, p, vec(dn_out_norm[l]), S, H, G, o_z)

        merged = _merge(ya, yb, bf(w_out_conv[l]), bf(w_out_delta[l]), gates)
        h = _matmul(merged, bf(w_o[l]), F32, res=h, name="oproj")

        kh, vh = _mem_kv(mem2, vec(mem_norm[l]), bf(xattn_wk[l]), bf(xattn_wv[l]))
        h = _xattn(h, vec(xattn_norm[l]), bf(xattn_wq[l]), kh, vh, bf(xattn_wo[l]), S, n_mem)

        if l == depth - 1:
            h = _ffn(h, vec(ffn2_norm[l]), bf(ffn2_w_gate[l]), bf(ffn2_w_up[l]), bf(ffn2_w_down[l]),
                     vec(final_norm), emit_h=False, out_dtype=F32)[0]
        else:
            h = _ffn(h, vec(ffn2_norm[l]), bf(ffn2_w_gate[l]), bf(ffn2_w_up[l]), bf(ffn2_w_down[l]),
                     vec(final_norm), emit_h=True, out_dtype=F32)[0]
    return h.reshape(B, S, D)
```

```python
import functools

import jax
import jax.numpy as jnp
from jax import lax
from jax.experimental import pallas as pl
from jax.experimental.pallas import tpu as pltpu

EPS = 1e-6
BF = jnp.bfloat16
F32 = jnp.float32
HI = lax.Precision.HIGHEST

DN_HEAD_DIM = 128
DN_CHUNK = 64
XATTN_HEAD_DIM = 128
HALO = 16
SUBLANES = 8
VMEM_BYTES_V7X = 64 * 1024 * 1024
VMEM_LIMIT = VMEM_BYTES_V7X - 4 * 1024 * 1024


def _cparams(*sem, vmem_limit=VMEM_LIMIT):
    return pltpu.CompilerParams(dimension_semantics=sem, vmem_limit_bytes=vmem_limit)


def _tile(dim, pref):
    t = min(pref, dim)
    while dim % t:
        t //= 2
    return t


def _rms(x, w):
    return x * lax.rsqrt(jnp.mean(x * x, axis=-1, keepdims=True) + EPS) * w


def _silu(x):
    return x * jax.nn.sigmoid(x)


def _dot(a, b):
    return jnp.dot(a, b, preferred_element_type=F32)


def _dot_nt(a, b, precision=None):
    return lax.dot_general(a, b, (((1,), (1,)), ((), ())), preferred_element_type=F32,
                           precision=precision)


def _ffn_kernel(xn_ref, x_hbm, wg_ref, wu_ref, wd_ref, ow_ref, o_hbm, acc_ref, sem_in, sem_out,
                *, n_i, n_f, tm, col_chunk, norm_out):
    i = pl.program_id(0)
    f = pl.program_id(1)

    def rows(t):
        return pl.ds(pl.multiple_of(t * tm, tm), tm)

    def load(t):
        return pltpu.make_async_copy(x_hbm.at[rows(t)], acc_ref, sem_in)

    def store(t):
        return pltpu.make_async_copy(acc_ref, o_hbm.at[rows(t)], sem_out)

    @pl.when(f == 0)
    def _():
        @pl.when(i > 0)
        def _():
            store(i - 1).wait()

        load(i).start()

    xn = xn_ref[...]
    g = _dot(xn, wg_ref[...].astype(BF))
    u = _dot(xn, wu_ref[...].astype(BF))
    a = (0.5 * _silu(g) * u).astype(BF)

    @pl.when(f == 0)
    def _():
        load(i).wait()

    for c0 in range(0, acc_ref.shape[1], col_chunk):
        cs = slice(c0, c0 + col_chunk)
        acc_ref[:, cs] += _dot(a, wd_ref[:, cs].astype(BF))

    @pl.when(f == n_f - 1)
    def _():
        if norm_out:
            acc_ref[...] = _rms(acc_ref[...], ow_ref[...])
        store(i).start()

        @pl.when(i == n_i - 1)
        def _():
            store(i).wait()


def _ffn(xn, x, wg, wu, wd, ow, layer, *, norm_out, tm_pref=1024, tf_pref=256, name="ffn"):
    M, D = x.shape
    F = wg.shape[-1]
    tm, tf = _tile(M, tm_pref), _tile(F, tf_pref)
    n_i, n_f = M // tm, F // tf
    return pl.pallas_call(
        functools.partial(_ffn_kernel, n_i=n_i, n_f=n_f, tm=tm, col_chunk=_tile(D, 1024),
                          norm_out=norm_out),
        grid=(n_i, n_f),
        in_specs=[
            pl.BlockSpec((tm, D), lambda i, f: (i, 0), pipeline_mode=pl.Buffered(1)),
            pl.BlockSpec(memory_space=pl.ANY),
            pl.BlockSpec((None, D, tf), lambda i, f: (layer, 0, f)),
            pl.BlockSpec((None, D, tf), lambda i, f: (layer, 0, f)),
            pl.BlockSpec((None, tf, D), lambda i, f: (layer, f, 0)),
            pl.BlockSpec((1, D), lambda i, f: (0, 0)),
        ],
        out_specs=pl.BlockSpec(memory_space=pl.ANY),
        out_shape=jax.ShapeDtypeStruct((M, D), F32),
        scratch_shapes=[pltpu.VMEM((tm, D), F32), pltpu.SemaphoreType.DMA(()),
                        pltpu.SemaphoreType.DMA(())],
        compiler_params=_cparams("arbitrary", "arbitrary", vmem_limit=VMEM_BYTES_V7X - 2 * 1024 * 1024),
        name=name,
    )(xn, x, wg, wu, wd, ow)


def _norm_kernel(x_ref, w_ref, *refs, with_proj):
    xn = _rms(x_ref[...], w_ref[...]).astype(BF)
    if with_proj:
        wp_ref, o_ref, p_ref = refs
        p_ref[...] = _dot(xn, wp_ref[...])
    else:
        (o_ref,) = refs
    o_ref[...] = xn


def _norm(x, w, w_proj=None, tr_pref=512):
    M, D = x.shape
    tr = _tile(M, tr_pref)
    in_specs = [pl.BlockSpec((tr, D), lambda i: (i, 0)), pl.BlockSpec((1, D), lambda i: (0, 0))]
    out_specs = [pl.BlockSpec((tr, D), lambda i: (i, 0))]
    out_shape = [jax.ShapeDtypeStruct((M, D), BF)]
    args = [x, w]
    if w_proj is not None:
        N = w_proj.shape[1]
        in_specs.append(pl.BlockSpec((D, N), lambda i: (0, 0)))
        out_specs.append(pl.BlockSpec((tr, N), lambda i: (i, 0)))
        out_shape.append(jax.ShapeDtypeStruct((M, N), F32))
        args.append(w_proj)
    return pl.pallas_call(
        functools.partial(_norm_kernel, with_proj=w_proj is not None),
        grid=(M // tr,),
        in_specs=in_specs,
        out_specs=out_specs,
        out_shape=out_shape,
        compiler_params=_cparams("parallel"),
        name="norm_proj" if w_proj is not None else "norm",
    )(*args)


def _mm_kernel(a_ref, w_ref, *refs, residual):
    if residual:
        res_ref, o_ref = refs
        o_ref[...] = res_ref[...] + _dot(a_ref[...], w_ref[...])
    else:
        (o_ref,) = refs
        o_ref[...] = _dot(a_ref[...], w_ref[...]).astype(o_ref.dtype)


def _matmul(a, w, out_dtype, *, res=None, tm_pref=1024, tn_pref=512, name="matmul"):
    M, K = a.shape
    N = w.shape[1]
    tm, tn = _tile(M, tm_pref), _tile(N, tn_pref)
    out_blk = pl.BlockSpec((tm, tn), lambda i, j: (i, j))
    in_specs = [pl.BlockSpec((tm, K), lambda i, j: (i, 0)),
                pl.BlockSpec((K, tn), lambda i, j: (0, j))]
    args = [a, w]
    if res is not None:
        in_specs.append(out_blk)
        args.append(res)
    return pl.pallas_call(
        functools.partial(_mm_kernel, residual=res is not None),
        grid=(M // tm, N // tn),
        in_specs=in_specs,
        out_specs=out_blk,
        out_shape=jax.ShapeDtypeStruct((M, N), out_dtype),
        compiler_params=_cparams("parallel", "arbitrary"),
        name=name,
    )(*args)


def _mm_ws_kernel(a_ref, w_ref, o_ref, wbf_ref):
    @pl.when(pl.program_id(1) == 0)
    def _():
        wbf_ref[...] = w_ref[...].astype(BF)

    o_ref[...] = _dot(a_ref[...], wbf_ref[...]).astype(o_ref.dtype)


def _matmul_ws(a, w, layer, n_cols, out_dtype, *, tm_pref=1024, tn_pref=512, name="matmul_ws"):
    M, K = a.shape
    tm, tn = _tile(M, tm_pref), _tile(n_cols, tn_pref)
    return pl.pallas_call(
        _mm_ws_kernel,
        grid=(n_cols // tn, M // tm),
        in_specs=[pl.BlockSpec((tm, K), lambda j, i: (i, 0)),
                  pl.BlockSpec((None, K, tn), lambda j, i: (layer, 0, j))],
        out_specs=pl.BlockSpec((tm, tn), lambda j, i: (i, j)),
        out_shape=jax.ShapeDtypeStruct((M, n_cols), out_dtype),
        scratch_shapes=[pltpu.VMEM((K, tn), BF)],
        compiler_params=_cparams("parallel", "arbitrary"),
        name=name,
    )(a, w)


def _split3(x):
    hi = x.astype(BF)
    r1 = x - hi.astype(F32)
    mid = r1.astype(BF)
    lo = (r1 - mid.astype(F32)).astype(BF)
    return hi, mid, lo


def _gates_kernel(ba_ref, eb_ref, ea_ref, sel_ref, alog_ref, dtb_ref, beta_ref, gc_ref, gct_ref):
    R = ba_ref.shape[0]
    ba = ba_ref[...]
    beta = jax.nn.sigmoid(ba)
    x = ba + dtb_ref[...]
    softplus = jnp.maximum(x, 0.0) + jnp.log1p(jnp.exp(-jnp.abs(x)))
    g = -jnp.exp(alog_ref[...]) * softplus
    ri = lax.broadcasted_iota(jnp.int32, (R, R), 0)
    ci = lax.broadcasted_iota(jnp.int32, (R, R), 1)
    tri = ((ri // DN_CHUNK == ci // DN_CHUNK) & (ri >= ci)).astype(BF)
    gc = sum(_dot(tri, piece) for piece in _split3(g))
    gc_parts = _split3(gc)
    beta_ref[...] = sum(_dot(piece, eb_ref[...]) for piece in _split3(beta))
    gc_ref[...] = sum(_dot(piece, ea_ref[...]) for piece in gc_parts)
    gct_ref[...] = sum(_dot_nt(sel_ref[...], piece) for piece in gc_parts)


def _gates(ba, a_log, dt_bias, n_heads, heads_per_group, tile_rows):
    M = ba.shape[0]
    G, H = heads_per_group, n_heads
    W = H * DN_HEAD_DIM
    lane_head = jnp.arange(W) // DN_HEAD_DIM
    rows = jnp.arange(128)[:, None]
    eb = (rows == lane_head[None, :]).astype(BF)
    ea = (rows == lane_head[None, :] + H).astype(BF)
    n_rows = (H // G) * SUBLANES
    r = jnp.arange(n_rows)
    head_of_row = jnp.where(r % SUBLANES < G, (r // SUBLANES) * G + r % SUBLANES, -128)
    sel = (jnp.arange(128)[None, :] == head_of_row[:, None] + H).astype(BF)
    at_a = lambda v: jnp.pad(v.astype(F32), (H, 128 - 2 * H))[None, :]
    R = tile_rows
    full = lambda shape: pl.BlockSpec(shape, lambda i: (0, 0))
    return pl.pallas_call(
        _gates_kernel,
        grid=(M // R,),
        in_specs=[pl.BlockSpec((R, 128), lambda i: (i, 0)), full((128, W)), full((128, W)),
                  full((n_rows, 128)), full((1, 128)), full((1, 128))],
        out_specs=[pl.BlockSpec((R, W), lambda i: (i, 0))] * 2
        + [pl.BlockSpec((n_rows, R), lambda i: (0, i))],
        out_shape=[jax.ShapeDtypeStruct((M, W), F32)] * 2
        + [jax.ShapeDtypeStruct((n_rows, M), F32)],
        compiler_params=_cparams("parallel"),
        name="dn_gates",
    )(ba, eb, ea, sel, at_a(a_log), at_a(dt_bias))


def _shift_rows(t, halo, k):
    r = pltpu.roll(t, k, 0)
    hr = pltpu.roll(halo, k, 0)[:8]
    rows = lax.broadcasted_iota(jnp.int32, (8, t.shape[1]), 0)
    first = jnp.where(rows < k, hr, r[:8])
    return jnp.concatenate([first, r[8:]], axis=0)


def _causal_conv(t, halo, w):
    K = w.shape[0]
    y = t * w[K - 1:K]
    for d in range(1, K):
        y = y + _shift_rows(t, halo, d) * w[K - 1 - d:K - d]
    return y


def _halo_spec(rows_per_tile, width, col_block, n_lead):
    per = rows_per_tile // HALO
    if n_lead == 1:
        return pl.BlockSpec((HALO, width), lambda i: (jnp.maximum(i * per - 1, 0), col_block))
    return pl.BlockSpec((HALO, width), lambda i, g: (jnp.maximum(i * per - 1, 0), col_block(g)))


def _conv_a_kernel(cx_ref, cc_ref, cb_ref, hx_ref, hc_ref, w_ref, o_ref, *, tiles_per_seq):
    first = (pl.program_id(0) % tiles_per_seq == 0)
    keep = jnp.where(first, 0.0, 1.0)
    t = cx_ref[...].astype(F32) * cc_ref[...].astype(F32)
    halo = hx_ref[...].astype(F32) * hc_ref[...].astype(F32) * keep
    y = _causal_conv(t, halo, w_ref[...])
    o_ref[...] = (cb_ref[...].astype(F32) * y).astype(o_ref.dtype)


def _conv_a(p, conv_w, seq, width, tr_pref=256):
    M = p.shape[0]
    tr = _tile(seq, tr_pref)
    blk = lambda c: pl.BlockSpec((tr, width), lambda i: (i, c))
    return pl.pallas_call(
        functools.partial(_conv_a_kernel, tiles_per_seq=seq // tr),
        grid=(M // tr,),
        in_specs=[blk(0), blk(1), blk(2), _halo_spec(tr, width, 0, 1), _halo_spec(tr, width, 1, 1),
                  pl.BlockSpec(conv_w.shape, lambda i: (0, 0))],
        out_specs=pl.BlockSpec((tr, width), lambda i: (i, 0)),
        out_shape=jax.ShapeDtypeStruct((M, width), BF),
        compiler_params=_cparams("parallel"),
        name="conv_a",
    )(p, p, p, p, p, conv_w)


def _dn_prep_kernel(q_ref, k_ref, v_ref, hq_ref, hk_ref, hv_ref, wq_ref, wk_ref, wv_ref,
                    beta_ref, gc_ref, gct_ref, qeff_ref, wdec_ref, ktail_ref, ubase_ref, obase_ref,
                    *, tiles_per_seq, heads, wave):
    R = q_ref.shape[0]
    C, DK = DN_CHUNK, DN_HEAD_DIM
    first = (pl.program_id(0) % tiles_per_seq == 0)
    keep = jnp.where(first, 0.0, 1.0)
    ri = lax.broadcasted_iota(jnp.int32, (R, R), 0)
    ci = lax.broadcasted_iota(jnp.int32, (R, R), 1)
    same = (ri // C) == (ci // C)
    incl = same & (ri >= ci)
    strict = same & (ri > ci)

    def l2n(x):
        return x * lax.rsqrt(jnp.sum(x * x, axis=-1, keepdims=True) + EPS)

    def head_inputs(hh):
        cols = slice(hh * DK, (hh + 1) * DK)

        def conv_silu(t_ref, h_ref, w_ref):
            y = _causal_conv(t_ref[:, cols].astype(F32), h_ref[:, cols].astype(F32) * keep,
                             w_ref[:, cols])
            return _silu(y)

        q = l2n(conv_silu(q_ref, hq_ref, wq_ref)) * (DK ** -0.5)
        k = l2n(conv_silu(k_ref, hk_ref, wk_ref))
        v = conv_silu(v_ref, hv_ref, wv_ref)
        beta = beta_ref[:, cols]
        gc = gc_ref[:, cols]
        g_row = jnp.broadcast_to(gct_ref[hh:hh + 1, :], (R, R))
        g_col = jnp.concatenate([gc] * (R // DK), axis=1)
        decay = jnp.where(incl, jnp.exp(jnp.where(incl, g_col - g_row, 0.0)), 0.0)
        kb = k * beta
        k16 = k.astype(BF)
        lmat = jnp.where(strict, _dot_nt(kb.astype(BF), k16) * decay, 0.0)
        aqk = jnp.where(incl, _dot_nt(q.astype(BF), k16) * decay, 0.0).astype(BF)
        rhs = jnp.concatenate([v * beta, kb * jnp.exp(gc)], axis=1)
        return dict(cols=cols, q=q, k=k, gc=gc, aqk=aqk, rhs=rhs, m=-lmat)

    n_fac = C.bit_length() - 1
    for w0 in range(0, heads, wave):
        hs = [head_inputs(hh) for hh in range(w0, min(w0 + wave, heads))]
        for d in hs:
            m16 = d["m"].astype(BF)
            d["qm"] = d["m"]
            d["m"] = _dot(m16, m16)
        for j in range(1, n_fac):
            for d in hs:
                m16 = d["m"].astype(BF)
                if j < n_fac - 1:
                    r = _dot(jnp.concatenate([m16, d["qm"].astype(BF)], axis=0), m16)
                    d["qm"] = d["qm"] + d["m"] + r[R:]
                    d["m"] = r[:R]
                else:
                    d["qm"] = d["qm"] + d["m"] + _dot(d["qm"].astype(BF), m16)
        for d in hs:
            d["sol"] = d["rhs"] + _dot(d["qm"].astype(BF), d["rhs"].astype(BF))
        for d in hs:
            cols, sol, gc, q, k = d["cols"], d["sol"], d["gc"], d["q"], d["k"]
            ubase, wdec = sol[:, :DK], sol[:, DK:]
            x = _dot(d["aqk"], jnp.concatenate([wdec, ubase], axis=1).astype(BF))
            gc_last = jnp.concatenate(
                [jnp.broadcast_to(gc[c * C + C - 1:c * C + C, :], (C, DK)) for c in range(R // C)],
                axis=0)
            qeff_ref[:, cols] = (q * jnp.exp(gc) - x[:, :DK]).astype(qeff_ref.dtype)
            obase_ref[:, cols] = x[:, DK:]
            ubase_ref[:, cols] = ubase
            wdec_ref[:, cols] = wdec.astype(wdec_ref.dtype)
            ktail_ref[:, cols] = (k * jnp.exp(gc_last - gc)).astype(ktail_ref.dtype)


def _dn_prep(p, qkv_conv_w, beta_b, gc_b, gct, seq, n_heads, heads_per_group, q_off):
    M = p.shape[0]
    G = heads_per_group
    GW = G * DN_HEAD_DIM
    W = n_heads * DN_HEAD_DIM
    n_groups = n_heads // G
    R = _tile(seq, 256)
    assert q_off % GW == 0 and R % DN_HEAD_DIM == 0
    cols = [lambda g, o=o: q_off // GW + o * n_groups + g for o in range(3)]
    blk = lambda c: pl.BlockSpec((R, GW), lambda i, g: (i, c(g)))
    wblk = lambda o: pl.BlockSpec((qkv_conv_w.shape[0], GW), lambda i, g: (0, o * n_groups + g))
    head = pl.BlockSpec((R, GW), lambda i, g: (i, g))
    return pl.pallas_call(
        functools.partial(_dn_prep_kernel, tiles_per_seq=seq // R, heads=G, wave=min(G, 4)),
        grid=(M // R, n_groups),
        in_specs=[blk(cols[0]), blk(cols[1]), blk(cols[2]),
                  _halo_spec(R, GW, cols[0], 2), _halo_spec(R, GW, cols[1], 2),
                  _halo_spec(R, GW, cols[2], 2),
                  wblk(0), wblk(1), wblk(2), head, head,
                  pl.BlockSpec((SUBLANES, R), lambda i, g: (g, i))],
        out_specs=[head] * 5,
        out_shape=[jax.ShapeDtypeStruct((M, W), BF)] * 3 + [jax.ShapeDtypeStruct((M, W), F32)] * 2,
        compiler_params=_cparams("parallel", "parallel"),
        name="dn_prep",
    )(p, p, p, p, p, p, qkv_conv_w, qkv_conv_w, qkv_conv_w, beta_b, gc_b, gct)


def _dn_scan_kernel(qeff_ref, wdec_ref, ktail_ref, ubase_ref, obase_ref, gc_ref, z_ref, nw_ref,
                    o_ref, state_ref, *, heads):
    rows_per_step = qeff_ref.shape[0]
    C, DK = DN_CHUNK, DN_HEAD_DIM

    @pl.when(pl.program_id(2) == 0)
    def _():
        state_ref[...] = jnp.zeros_like(state_ref)

    nw = nw_ref[...]

    def chunk(c, carry):
        r0 = pl.multiple_of(c * C, C)
        rows = pl.ds(r0, C)
        tail = pl.ds(pl.multiple_of(r0 + (C - 8), 8), 8)
        for h in range(heads):
            cols = slice(h * DK, (h + 1) * DK)
            st = state_ref[h]
            lhs = jnp.concatenate([qeff_ref[rows, cols], wdec_ref[rows, cols]], axis=0)
            r = _dot(lhs, st.astype(BF))
            o = obase_ref[rows, cols] + r[:C]
            u = ubase_ref[rows, cols] - r[C:]
            g_last = jnp.exp(gc_ref[tail, cols][7:8])
            upd = lax.dot_general(ktail_ref[rows, cols], u.astype(BF), (((0,), (0,)), ((), ())),
                                  preferred_element_type=F32)
            state_ref[h] = st * g_last + upd
            zf = z_ref[rows, cols].astype(F32)
            on = o * lax.rsqrt(jnp.mean(o * o, axis=-1, keepdims=True) + EPS) * nw * _silu(zf)
            o_ref[rows, cols] = on.astype(o_ref.dtype)
        return carry

    lax.fori_loop(0, rows_per_step // C, chunk, 0)


def _dn_scan(qeff, wdec, ktail, ubase, obase, gc_b, p, dn_out_norm, seq, n_heads, heads_per_group,
             z_off, ts_pref=512):
    M, W = qeff.shape
    G = heads_per_group
    GW = G * DN_HEAD_DIM
    ts = _tile(seq, ts_pref)
    per = seq // ts
    assert z_off % GW == 0
    zb = z_off // GW
    blk = pl.BlockSpec((ts, GW), lambda b, g, t: (b * per + t, g))
    return pl.pallas_call(
        functools.partial(_dn_scan_kernel, heads=G),
        grid=(M // seq, n_heads // G, per),
        in_specs=[blk] * 6 + [pl.BlockSpec((ts, GW), lambda b, g, t: (b * per + t, zb + g)),
                              pl.BlockSpec((1, DN_HEAD_DIM), lambda b, g, t: (0, 0))],
        out_specs=blk,
        out_shape=jax.ShapeDtypeStruct((M, W), BF),
        scratch_shapes=[pltpu.VMEM((G, DN_HEAD_DIM, DN_HEAD_DIM), F32)],
        compiler_params=_cparams("parallel", "parallel", "arbitrary"),
        name="dn_scan",
    )(qeff, wdec, ktail, ubase, obase, gc_b, p, dn_out_norm)


def _merge_kernel(ya_ref, yb_ref, wa_ref, wb_ref, ga_ref, gb_ref, o_ref):
    ya = _dot(ya_ref[...], wa_ref[...])
    yb = _dot(yb_ref[...], wb_ref[...])
    m = jax.nn.sigmoid(ga_ref[...].astype(F32)) * ya + jax.nn.sigmoid(gb_ref[...].astype(F32)) * yb
    o_ref[...] = m.astype(o_ref.dtype)


def _merge(ya, yb, wa, wb, gates, tm_pref=1024, tn_pref=512):
    M, KA = ya.shape
    KB = yb.shape[1]
    N = wa.shape[1]
    tm, tn = _tile(M, tm_pref), _tile(N, tn_pref)
    nb = N // tn
    return pl.pallas_call(
        _merge_kernel,
        grid=(M // tm, nb),
        in_specs=[pl.BlockSpec((tm, KA), lambda i, j: (i, 0)),
                  pl.BlockSpec((tm, KB), lambda i, j: (i, 0)),
                  pl.BlockSpec((KA, tn), lambda i, j: (0, j)),
                  pl.BlockSpec((KB, tn), lambda i, j: (0, j)),
                  pl.BlockSpec((tm, tn), lambda i, j: (i, j)),
                  pl.BlockSpec((tm, tn), lambda i, j: (i, nb + j))],
        out_specs=pl.BlockSpec((tm, tn), lambda i, j: (i, j)),
        out_shape=jax.ShapeDtypeStruct((M, N), BF),
        compiler_params=_cparams("parallel", "arbitrary"),
        name="merge",
    )(ya, yb, wa, wb, gates, gates)


def _mem_kv_kernel(m_ref, nw_ref, wk_ref, wv_ref, k_ref, v_ref):
    mn = _rms(m_ref[...], nw_ref[...]).astype(BF)
    k_ref[...] = _dot(mn, wk_ref[...]).astype(k_ref.dtype)
    v_ref[...] = _dot(mn, wv_ref[...]).astype(v_ref.dtype)


def _mem_kv(mem, nw, wk, wv, tr_pref=256):
    Mm, D = mem.shape
    N = wk.shape[1]
    tr = _tile(Mm, tr_pref)
    full = lambda shape: pl.BlockSpec(shape, lambda i: (0, 0))
    return pl.pallas_call(
        _mem_kv_kernel,
        grid=(Mm // tr,),
        in_specs=[pl.BlockSpec((tr, D), lambda i: (i, 0)), full((1, D)), full((D, N)), full((D, N))],
        out_specs=[pl.BlockSpec((tr, N), lambda i: (i, 0))] * 2,
        out_shape=[jax.ShapeDtypeStruct((Mm, N), BF)] * 2,
        compiler_params=_cparams("parallel"),
        name="mem_kv",
    )(mem, nw, wk, wv)


def _xattn_kernel(h_ref, nw_ref, wq_ref, k_ref, v_ref, wo_ref, nw2_ref, o_ref, on_ref, *, n_heads):
    DH = XATTN_HEAD_DIM
    h = h_ref[...]
    q = _dot(_rms(h, nw_ref[...]).astype(BF), wq_ref[...])
    outs = []
    for hd in range(n_heads):
        cols = slice(hd * DH, (hd + 1) * DH)
        s = _dot_nt(q[:, cols].astype(BF), k_ref[:, cols]) * (DH ** -0.5)
        s = s - jnp.max(s, axis=-1, keepdims=True)
        e = jnp.exp(s)
        pr = e / jnp.sum(e, axis=-1, keepdims=True)
        outs.append(_dot(pr.astype(BF), v_ref[:, cols]))
    o = jnp.concatenate(outs, axis=1).astype(BF)
    h_out = h + _dot(o, wo_ref[...])
    o_ref[...] = h_out
    on_ref[...] = _rms(h_out, nw2_ref[...]).astype(on_ref.dtype)


def _xattn(h, nw, wq, kh, vh, wo, nw2, seq, n_mem, tm_pref=256):
    M, D = h.shape
    N = wq.shape[1]
    tm = _tile(seq, tm_pref)
    per = seq // tm
    full = lambda shape: pl.BlockSpec(shape, lambda i: (0, 0))
    kv = pl.BlockSpec((n_mem, N), lambda i: (i // per, 0))
    row = pl.BlockSpec((tm, D), lambda i: (i, 0))
    return pl.pallas_call(
        functools.partial(_xattn_kernel, n_heads=N // XATTN_HEAD_DIM),
        grid=(M // tm,),
        in_specs=[row, full((1, D)), full((D, N)), kv, kv, full((N, D)), full((1, D))],
        out_specs=[row, row],
        out_shape=[jax.ShapeDtypeStruct((M, D), F32), jax.ShapeDtypeStruct((M, D), BF)],
        compiler_params=_cparams("parallel"),
        name="xattn",
    )(h, nw, wq, kh, vh, wo, nw2)


def kernel(x, mem, ffn1_norm, ffn1_w_gate, ffn1_w_up, ffn1_w_down, mix_norm, w_in, conv_w, qkv_conv_w, a_log, dt_bias, dn_out_norm, w_out_conv, w_out_delta, w_o, xattn_norm, mem_norm, xattn_wq, xattn_wk, xattn_wv, xattn_wo, ffn2_norm, ffn2_w_gate, ffn2_w_up, ffn2_w_down, final_norm):
    B, S, D = x.shape
    depth = ffn1_norm.shape[0]
    n_mem = mem.shape[1]
    CW = conv_w.shape[-1]
    H = a_log.shape[-1]
    DW = H * DN_HEAD_DIM
    G = min(H, SUBLANES)
    M = B * S
    assert S % DN_CHUNK == 0 and 2 * H <= 128 and H % G == 0
    o_qkv = 3 * CW
    o_z = o_qkv + 3 * DW
    o_b = o_z + DW
    o_ga = o_b + 2 * H
    assert w_in.shape[-1] == o_ga + 2 * D
    vec = lambda v: v.reshape(1, -1).astype(F32)
    bf = lambda w: w.astype(BF)

    h = x.reshape(M, D)
    mem2 = mem.reshape(B * n_mem, D)
    for l in range(depth):
        h = _ffn(_norm(h, vec(ffn1_norm[l]))[0], h, ffn1_w_gate, ffn1_w_up, bf(ffn1_w_down),
                 vec(mix_norm[l]), l, norm_out=False, name="ffn1")

        w_ba = bf(jnp.pad(w_in[l][:, o_b:o_ga], ((0, 0), (0, 128 - 2 * H))))
        un, ba = _norm(h, vec(mix_norm[l]), w_ba)
        p = _matmul_ws(un, w_in, l, o_b, BF, name="in_proj")
        gates = _matmul(un, bf(w_in[l][:, o_ga:]), BF, name="in_proj_gates")

        ya = _conv_a(p, conv_w[l].astype(F32), S, CW)
        beta_b, gc_b, gct = _gates(ba, a_log[l], dt_bias[l], H, G, _tile(S, 256))
        qeff, wdec, ktail, ubase, obase = _dn_prep(p, qkv_conv_w[l].astype(F32), beta_b, gc_b, gct,
                                                   S, H, G, o_qkv)
        yb = _dn_scan(qeff, wdec, ktail, ubase, obase, gc_b, p, vec(dn_out_norm[l]), S, H, G, o_z)

        merged = _merge(ya, yb, bf(w_out_conv[l]), bf(w_out_delta[l]), gates)
        h = _matmul(merged, bf(w_o[l]), F32, res=h, name="oproj")

        kh, vh = _mem_kv(mem2, vec(mem_norm[l]), bf(xattn_wk[l]), bf(xattn_wv[l]))
        h, xn = _xattn(h, vec(xattn_norm[l]), bf(xattn_wq[l]), kh, vh, bf(xattn_wo[l]),
                       vec(ffn2_norm[l]), S, n_mem)

        h = _ffn(xn, h, ffn2_w_gate, ffn2_w_up, bf(ffn2_w_down), vec(final_norm), l,
                 norm_out=(l == depth - 1), name="ffn2")
    return h.reshape(B, S, D)
```

```python
import functools

import jax
import jax.numpy as jnp
from jax import lax
from jax.experimental import pallas as pl
from jax.experimental.pallas import tpu as pltpu

EPS = 1e-6
BF = jnp.bfloat16
F32 = jnp.float32
HI = lax.Precision.HIGHEST

DN_HEAD_DIM = 128
DN_CHUNK = 64
XATTN_HEAD_DIM = 128
HALO = 16
SUBLANES = 8
VMEM_BYTES_V7X = 64 * 1024 * 1024
VMEM_LIMIT = VMEM_BYTES_V7X - 4 * 1024 * 1024


def _cparams(*sem, vmem_limit=VMEM_LIMIT):
    return pltpu.CompilerParams(dimension_semantics=sem, vmem_limit_bytes=vmem_limit)


def _tile(dim, pref):
    t = min(pref, dim)
    while dim % t:
        t //= 2
    return t


def _rms(x, w):
    return x * lax.rsqrt(jnp.mean(x * x, axis=-1, keepdims=True) + EPS) * w


def _silu(x):
    return x * jax.nn.sigmoid(x)


def _dot(a, b):
    return jnp.dot(a, b, preferred_element_type=F32)


def _dot_nt(a, b, precision=None):
    return lax.dot_general(a, b, (((1,), (1,)), ((), ())), preferred_element_type=F32,
                           precision=precision)


def _ffn_kernel(xn_ref, x_hbm, wg_ref, wu_ref, wd_ref, ow_ref, o_hbm, acc_ref, sem_in, sem_out,
                *, n_i, n_f, tm, col_chunk, norm_out):
    i = pl.program_id(0)
    f = pl.program_id(1)

    def rows(t):
        return pl.ds(pl.multiple_of(t * tm, tm), tm)

    def load(t):
        return pltpu.make_async_copy(x_hbm.at[rows(t)], acc_ref, sem_in)

    def store(t):
        return pltpu.make_async_copy(acc_ref, o_hbm.at[rows(t)], sem_out)

    @pl.when(f == 0)
    def _():
        @pl.when(i > 0)
        def _():
            store(i - 1).wait()

        load(i).start()

    xn = xn_ref[...]
    g = _dot(xn, wg_ref[...].astype(BF))
    u = _dot(xn, wu_ref[...].astype(BF))
    a = (0.5 * _silu(g) * u).astype(BF)

    @pl.when(f == 0)
    def _():
        load(i).wait()

    for c0 in range(0, acc_ref.shape[1], col_chunk):
        cs = slice(c0, c0 + col_chunk)
        acc_ref[:, cs] += _dot(a, wd_ref[:, cs].astype(BF))

    @pl.when(f == n_f - 1)
    def _():
        if norm_out:
            acc_ref[...] = _rms(acc_ref[...], ow_ref[...])
        store(i).start()

        @pl.when(i == n_i - 1)
        def _():
            store(i).wait()


def _ffn(xn, x, wg, wu, wd, ow, layer, *, norm_out, tm_pref=1024, tf_pref=256, name="ffn"):
    M, D = x.shape
    F = wg.shape[-1]
    tm, tf = _tile(M, tm_pref), _tile(F, tf_pref)
    n_i, n_f = M // tm, F // tf
    return pl.pallas_call(
        functools.partial(_ffn_kernel, n_i=n_i, n_f=n_f, tm=tm, col_chunk=_tile(D, 1024),
                          norm_out=norm_out),
        grid=(n_i, n_f),
        in_specs=[
            pl.BlockSpec((tm, D), lambda i, f: (i, 0), pipeline_mode=pl.Buffered(1)),
            pl.BlockSpec(memory_space=pl.ANY),
            pl.BlockSpec((None, D, tf), lambda i, f: (layer, 0, f)),
            pl.BlockSpec((None, D, tf), lambda i, f: (layer, 0, f)),
            pl.BlockSpec((None, tf, D), lambda i, f: (layer, f, 0)),
            pl.BlockSpec((1, D), lambda i, f: (0, 0)),
        ],
        out_specs=pl.BlockSpec(memory_space=pl.ANY),
        out_shape=jax.ShapeDtypeStruct((M, D), F32),
        scratch_shapes=[pltpu.VMEM((tm, D), F32), pltpu.SemaphoreType.DMA(()),
                        pltpu.SemaphoreType.DMA(())],
        compiler_params=_cparams("arbitrary", "arbitrary", vmem_limit=VMEM_BYTES_V7X - 2 * 1024 * 1024),
        name=name,
    )(xn, x, wg, wu, wd, ow)


def _norm_kernel(x_ref, w_ref, *refs, with_proj):
    xn = _rms(x_ref[...], w_ref[...]).astype(BF)
    if with_proj:
        wp_ref, o_ref, p_ref = refs
        p_ref[...] = _dot_nt(xn, wp_ref[...])
    else:
        (o_ref,) = refs
    o_ref[...] = xn


def _norm(x, w, w_proj=None, tr_pref=512):
    M, D = x.shape
    tr = _tile(M, tr_pref)
    in_specs = [pl.BlockSpec((tr, D), lambda i: (i, 0)), pl.BlockSpec((1, D), lambda i: (0, 0))]
    out_specs = [pl.BlockSpec((tr, D), lambda i: (i, 0))]
    out_shape = [jax.ShapeDtypeStruct((M, D), BF)]
    args = [x, w]
    if w_proj is not None:
        N = w_proj.shape[0]
        in_specs.append(pl.BlockSpec((N, D), lambda i: (0, 0)))
        out_specs.append(pl.BlockSpec((tr, N), lambda i: (i, 0)))
        out_shape.append(jax.ShapeDtypeStruct((M, N), F32))
        args.append(w_proj)
    return pl.pallas_call(
        functools.partial(_norm_kernel, with_proj=w_proj is not None),
        grid=(M // tr,),
        in_specs=in_specs,
        out_specs=out_specs,
        out_shape=out_shape,
        compiler_params=_cparams("parallel"),
        name="norm_proj" if w_proj is not None else "norm",
    )(*args)


def _mm_kernel(a_ref, w_ref, *refs, residual):
    if residual:
        res_ref, o_ref = refs
        o_ref[...] = res_ref[...] + _dot(a_ref[...], w_ref[...])
    else:
        (o_ref,) = refs
        o_ref[...] = _dot(a_ref[...], w_ref[...]).astype(o_ref.dtype)


def _matmul(a, w, out_dtype, *, res=None, tm_pref=1024, tn_pref=512, name="matmul"):
    M, K = a.shape
    N = w.shape[1]
    tm, tn = _tile(M, tm_pref), _tile(N, tn_pref)
    out_blk = pl.BlockSpec((tm, tn), lambda i, j: (i, j))
    in_specs = [pl.BlockSpec((tm, K), lambda i, j: (i, 0)),
                pl.BlockSpec((K, tn), lambda i, j: (0, j))]
    args = [a, w]
    if res is not None:
        in_specs.append(out_blk)
        args.append(res)
    return pl.pallas_call(
        functools.partial(_mm_kernel, residual=res is not None),
        grid=(M // tm, N // tn),
        in_specs=in_specs,
        out_specs=out_blk,
        out_shape=jax.ShapeDtypeStruct((M, N), out_dtype),
        compiler_params=_cparams("parallel", "arbitrary"),
        name=name,
    )(*args)


def _mm_wst_kernel(a_ref, wt_hbm, o_ref, wf32_ref, wbf_ref, sem, *, layer, tn, n_j, n_head, skip):
    j = pl.program_id(0)

    def fetch(t):
        start = pl.multiple_of(t * tn + jnp.where(t >= n_head, skip, 0), SUBLANES)
        return pltpu.make_async_copy(wt_hbm.at[layer, pl.ds(start, tn)], wf32_ref.at[t % 2],
                                     sem.at[t % 2])

    @pl.when(pl.program_id(1) == 0)
    def _():
        @pl.when(j == 0)
        def _():
            fetch(j).start()

        fetch(j).wait()
        wbf_ref[...] = wf32_ref[j % 2].astype(BF)

        @pl.when(j + 1 < n_j)
        def _():
            fetch(j + 1).start()

    o_ref[...] = _dot_nt(a_ref[...], wbf_ref[...]).astype(o_ref.dtype)


def _matmul_wst(a, wt, layer, n_head, skip, n_out, out_dtype, *, tm_pref=1024, tn_pref=512,
                name="matmul_wst"):
    M, K = a.shape
    tm, tn = _tile(M, tm_pref), _tile(n_out, tn_pref)
    assert n_head % tn == 0 and skip % SUBLANES == 0
    n_j = n_out // tn
    return pl.pallas_call(
        functools.partial(_mm_wst_kernel, layer=layer, tn=tn, n_j=n_j, n_head=n_head // tn, skip=skip),
        grid=(n_j, M // tm),
        in_specs=[pl.BlockSpec((tm, K), lambda j, i: (i, 0)),
                  pl.BlockSpec(memory_space=pl.ANY)],
        out_specs=pl.BlockSpec((tm, tn), lambda j, i: (i, j)),
        out_shape=jax.ShapeDtypeStruct((M, n_out), out_dtype),
        scratch_shapes=[pltpu.VMEM((2, tn, K), F32), pltpu.VMEM((tn, K), BF),
                        pltpu.SemaphoreType.DMA((2,))],
        compiler_params=_cparams("arbitrary", "arbitrary"),
        name=name,
    )(a, wt)


def _split3(x):
    hi = x.astype(BF)
    r1 = x - hi.astype(F32)
    mid = r1.astype(BF)
    lo = (r1 - mid.astype(F32)).astype(BF)
    return hi, mid, lo


def _gates_kernel(ba_ref, eb_ref, ea_ref, sel_ref, alog_ref, dtb_ref, beta_ref, gc_ref, gct_ref):
    R = ba_ref.shape[0]
    ba = ba_ref[...]
    beta = jax.nn.sigmoid(ba)
    x = ba + dtb_ref[...]
    softplus = jnp.maximum(x, 0.0) + jnp.log1p(jnp.exp(-jnp.abs(x)))
    g = -jnp.exp(alog_ref[...]) * softplus
    ri = lax.broadcasted_iota(jnp.int32, (R, R), 0)
    ci = lax.broadcasted_iota(jnp.int32, (R, R), 1)
    tri = ((ri // DN_CHUNK == ci // DN_CHUNK) & (ri >= ci)).astype(BF)
    gc = sum(_dot(tri, piece) for piece in _split3(g))
    gc_parts = _split3(gc)
    beta_ref[...] = sum(_dot(piece, eb_ref[...]) for piece in _split3(beta))
    gc_ref[...] = sum(_dot(piece, ea_ref[...]) for piece in gc_parts)
    gct_ref[...] = sum(_dot_nt(sel_ref[...], piece) for piece in gc_parts)


def _gates(ba, a_log, dt_bias, n_heads, heads_per_group, tile_rows):
    M = ba.shape[0]
    G, H = heads_per_group, n_heads
    W = H * DN_HEAD_DIM
    lane_head = jnp.arange(W) // DN_HEAD_DIM
    rows = jnp.arange(128)[:, None]
    eb = (rows == lane_head[None, :]).astype(BF)
    ea = (rows == lane_head[None, :] + H).astype(BF)
    n_rows = (H // G) * SUBLANES
    r = jnp.arange(n_rows)
    head_of_row = jnp.where(r % SUBLANES < G, (r // SUBLANES) * G + r % SUBLANES, -128)
    sel = (jnp.arange(128)[None, :] == head_of_row[:, None] + H).astype(BF)
    at_a = lambda v: jnp.pad(v.astype(F32), (H, 128 - 2 * H))[None, :]
    R = tile_rows
    full = lambda shape: pl.BlockSpec(shape, lambda i: (0, 0))
    return pl.pallas_call(
        _gates_kernel,
        grid=(M // R,),
        in_specs=[pl.BlockSpec((R, 128), lambda i: (i, 0)), full((128, W)), full((128, W)),
                  full((n_rows, 128)), full((1, 128)), full((1, 128))],
        out_specs=[pl.BlockSpec((R, W), lambda i: (i, 0))] * 2
        + [pl.BlockSpec((n_rows, R), lambda i: (0, i))],
        out_shape=[jax.ShapeDtypeStruct((M, W), F32)] * 2
        + [jax.ShapeDtypeStruct((n_rows, M), F32)],
        compiler_params=_cparams("parallel"),
        name="dn_gates",
    )(ba, eb, ea, sel, at_a(a_log), at_a(dt_bias))


def _shift_rows(t, halo, k):
    r = pltpu.roll(t, k, 0)
    hr = pltpu.roll(halo, k, 0)[:8]
    rows = lax.broadcasted_iota(jnp.int32, (8, t.shape[1]), 0)
    first = jnp.where(rows < k, hr, r[:8])
    return jnp.concatenate([first, r[8:]], axis=0)


def _causal_conv(t, halo, w):
    K = w.shape[0]
    y = t * w[K - 1:K]
    for d in range(1, K):
        y = y + _shift_rows(t, halo, d) * w[K - 1 - d:K - d]
    return y


def _halo_spec(rows_per_tile, width, col_block, n_lead):
    per = rows_per_tile // HALO
    if n_lead == 1:
        return pl.BlockSpec((HALO, width), lambda i: (jnp.maximum(i * per - 1, 0), col_block))
    return pl.BlockSpec((HALO, width), lambda i, g: (jnp.maximum(i * per - 1, 0), col_block(g)))


def _conv_a_kernel(cx_ref, cc_ref, cb_ref, hx_ref, hc_ref, w_ref, o_ref, *, tiles_per_seq):
    first = (pl.program_id(0) % tiles_per_seq == 0)
    keep = jnp.where(first, 0.0, 1.0)
    t = cx_ref[...].astype(F32) * cc_ref[...].astype(F32)
    halo = hx_ref[...].astype(F32) * hc_ref[...].astype(F32) * keep
    y = _causal_conv(t, halo, w_ref[...])
    o_ref[...] = (cb_ref[...].astype(F32) * y).astype(o_ref.dtype)


def _conv_a(p, conv_w, seq, width, tr_pref=256):
    M = p.shape[0]
    tr = _tile(seq, tr_pref)
    blk = lambda c: pl.BlockSpec((tr, width), lambda i: (i, c))
    return pl.pallas_call(
        functools.partial(_conv_a_kernel, tiles_per_seq=seq // tr),
        grid=(M // tr,),
        in_specs=[blk(0), blk(1), blk(2), _halo_spec(tr, width, 0, 1), _halo_spec(tr, width, 1, 1),
                  pl.BlockSpec(conv_w.shape, lambda i: (0, 0))],
        out_specs=pl.BlockSpec((tr, width), lambda i: (i, 0)),
        out_shape=jax.ShapeDtypeStruct((M, width), BF),
        compiler_params=_cparams("parallel"),
        name="conv_a",
    )(p, p, p, p, p, conv_w)


def _dn_prep_kernel(q_ref, k_ref, v_ref, hq_ref, hk_ref, hv_ref, wq_ref, wk_ref, wv_ref,
                    beta_ref, gc_ref, gct_ref, qeff_ref, wdec_ref, ktail_ref, ubase_ref, obase_ref,
                    *, tiles_per_seq, heads, wave):
    R = q_ref.shape[0]
    C, DK = DN_CHUNK, DN_HEAD_DIM
    first = (pl.program_id(0) % tiles_per_seq == 0)
    keep = jnp.where(first, 0.0, 1.0)
    ri = lax.broadcasted_iota(jnp.int32, (R, R), 0)
    ci = lax.broadcasted_iota(jnp.int32, (R, R), 1)
    same = (ri // C) == (ci // C)
    incl = same & (ri >= ci)
    strict = same & (ri > ci)

    def l2n(x):
        return x * lax.rsqrt(jnp.sum(x * x, axis=-1, keepdims=True) + EPS)

    def head_inputs(hh):
        cols = slice(hh * DK, (hh + 1) * DK)

        def conv_silu(t_ref, h_ref, w_ref):
            y = _causal_conv(t_ref[:, cols].astype(F32), h_ref[:, cols].astype(F32) * keep,
                             w_ref[:, cols])
            return _silu(y)

        q = l2n(conv_silu(q_ref, hq_ref, wq_ref)) * (DK ** -0.5)
        k = l2n(conv_silu(k_ref, hk_ref, wk_ref))
        v = conv_silu(v_ref, hv_ref, wv_ref)
        beta = beta_ref[:, cols]
        gc = gc_ref[:, cols]
        g_row = jnp.broadcast_to(gct_ref[hh:hh + 1, :], (R, R))
        g_col = jnp.concatenate([gc] * (R // DK), axis=1)
        decay = jnp.where(incl, jnp.exp(jnp.where(incl, g_col - g_row, 0.0)), 0.0)
        kb = k * beta
        k16 = k.astype(BF)
        lmat = jnp.where(strict, _dot_nt(kb.astype(BF), k16) * decay, 0.0)
        aqk = jnp.where(incl, _dot_nt(q.astype(BF), k16) * decay, 0.0).astype(BF)
        rhs = jnp.concatenate([v * beta, kb * jnp.exp(gc)], axis=1)
        return dict(cols=cols, q=q, k=k, gc=gc, aqk=aqk, rhs=rhs, m=-lmat)

    n_fac = C.bit_length() - 1
    for w0 in range(0, heads, wave):
        hs = [head_inputs(hh) for hh in range(w0, min(w0 + wave, heads))]
        for d in hs:
            m16 = d["m"].astype(BF)
            d["qm"] = d["m"]
            d["m"] = _dot(m16, m16)
        for j in range(1, n_fac):
            for d in hs:
                m16 = d["m"].astype(BF)
                if j < n_fac - 1:
                    r = _dot(jnp.concatenate([m16, d["qm"].astype(BF)], axis=0), m16)
                    d["qm"] = d["qm"] + d["m"] + r[R:]
                    d["m"] = r[:R]
                else:
                    d["qm"] = d["qm"] + d["m"] + _dot(d["qm"].astype(BF), m16)
        for d in hs:
            d["sol"] = d["rhs"] + _dot(d["qm"].astype(BF), d["rhs"].astype(BF))
        for d in hs:
            cols, sol, gc, q, k = d["cols"], d["sol"], d["gc"], d["q"], d["k"]
            ubase, wdec = sol[:, :DK], sol[:, DK:]
            x = _dot(d["aqk"], jnp.concatenate([wdec, ubase], axis=1).astype(BF))
            gc_last = jnp.concatenate(
                [jnp.broadcast_to(gc[c * C + C - 1:c * C + C, :], (C, DK)) for c in range(R // C)],
                axis=0)
            qeff_ref[:, cols] = (q * jnp.exp(gc) - x[:, :DK]).astype(qeff_ref.dtype)
            obase_ref[:, cols] = x[:, DK:]
            ubase_ref[:, cols] = ubase
            wdec_ref[:, cols] = wdec.astype(wdec_ref.dtype)
            ktail_ref[:, cols] = (k * jnp.exp(gc_last - gc)).astype(ktail_ref.dtype)


def _dn_prep(p, qkv_conv_w, beta_b, gc_b, gct, seq, n_heads, heads_per_group, q_off):
    M = p.shape[0]
    G = heads_per_group
    GW = G * DN_HEAD_DIM
    W = n_heads * DN_HEAD_DIM
    n_groups = n_heads // G
    R = _tile(seq, 256)
    assert q_off % GW == 0 and R % DN_HEAD_DIM == 0
    cols = [lambda g, o=o: q_off // GW + o * n_groups + g for o in range(3)]
    blk = lambda c: pl.BlockSpec((R, GW), lambda i, g: (i, c(g)))
    wblk = lambda o: pl.BlockSpec((qkv_conv_w.shape[0], GW), lambda i, g: (0, o * n_groups + g))
    head = pl.BlockSpec((R, GW), lambda i, g: (i, g))
    return pl.pallas_call(
        functools.partial(_dn_prep_kernel, tiles_per_seq=seq // R, heads=G, wave=min(G, 4)),
        grid=(M // R, n_groups),
        in_specs=[blk(cols[0]), blk(cols[1]), blk(cols[2]),
                  _halo_spec(R, GW, cols[0], 2), _halo_spec(R, GW, cols[1], 2),
                  _halo_spec(R, GW, cols[2], 2),
                  wblk(0), wblk(1), wblk(2), head, head,
                  pl.BlockSpec((SUBLANES, R), lambda i, g: (g, i))],
        out_specs=[head] * 5,
        out_shape=[jax.ShapeDtypeStruct((M, W), BF)] * 3 + [jax.ShapeDtypeStruct((M, W), F32)] * 2,
        compiler_params=_cparams("parallel", "parallel"),
        name="dn_prep",
    )(p, p, p, p, p, p, qkv_conv_w, qkv_conv_w, qkv_conv_w, beta_b, gc_b, gct)


def _dn_scan_kernel(qeff_ref, wdec_ref, ktail_ref, ubase_ref, obase_ref, gc_ref, z_ref, nw_ref,
                    o_ref, state_ref, *, heads):
    rows_per_step = qeff_ref.shape[0]
    C, DK = DN_CHUNK, DN_HEAD_DIM

    @pl.when(pl.program_id(2) == 0)
    def _():
        state_ref[...] = jnp.zeros_like(state_ref)

    nw = nw_ref[...]

    def chunk(c, carry):
        r0 = pl.multiple_of(c * C, C)
        rows = pl.ds(r0, C)
        tail = pl.ds(pl.multiple_of(r0 + (C - 8), 8), 8)
        for h in range(heads):
            cols = slice(h * DK, (h + 1) * DK)
            st = state_ref[h]
            lhs = jnp.concatenate([qeff_ref[rows, cols], wdec_ref[rows, cols]], axis=0)
            r = _dot(lhs, st.astype(BF))
            o = obase_ref[rows, cols] + r[:C]
            u = ubase_ref[rows, cols] - r[C:]
            g_last = jnp.exp(gc_ref[tail, cols][7:8])
            upd = lax.dot_general(ktail_ref[rows, cols], u.astype(BF), (((0,), (0,)), ((), ())),
                                  preferred_element_type=F32)
            state_ref[h] = st * g_last + upd
            zf = z_ref[rows, cols].astype(F32)
            on = o * lax.rsqrt(jnp.mean(o * o, axis=-1, keepdims=True) + EPS) * nw * _silu(zf)
            o_ref[rows, cols] = on.astype(o_ref.dtype)
        return carry

    lax.fori_loop(0, rows_per_step // C, chunk, 0)


def _dn_scan(qeff, wdec, ktail, ubase, obase, gc_b, p, dn_out_norm, seq, n_heads, heads_per_group,
             z_off, ts_pref=512):
    M, W = qeff.shape
    G = heads_per_group
    GW = G * DN_HEAD_DIM
    ts = _tile(seq, ts_pref)
    per = seq // ts
    assert z_off % GW == 0
    zb = z_off // GW
    blk = pl.BlockSpec((ts, GW), lambda b, g, t: (b * per + t, g))
    return pl.pallas_call(
        functools.partial(_dn_scan_kernel, heads=G),
        grid=(M // seq, n_heads // G, per),
        in_specs=[blk] * 6 + [pl.BlockSpec((ts, GW), lambda b, g, t: (b * per + t, zb + g)),
                              pl.BlockSpec((1, DN_HEAD_DIM), lambda b, g, t: (0, 0))],
        out_specs=blk,
        out_shape=jax.ShapeDtypeStruct((M, W), BF),
        scratch_shapes=[pltpu.VMEM((G, DN_HEAD_DIM, DN_HEAD_DIM), F32)],
        compiler_params=_cparams("parallel", "parallel", "arbitrary"),
        name="dn_scan",
    )(qeff, wdec, ktail, ubase, obase, gc_b, p, dn_out_norm)


def _merge_kernel(ya_ref, yb_ref, wa_ref, wb_ref, ga_ref, gb_ref, o_ref):
    ya = _dot(ya_ref[...], wa_ref[...])
    yb = _dot(yb_ref[...], wb_ref[...])
    m = jax.nn.sigmoid(ga_ref[...].astype(F32)) * ya + jax.nn.sigmoid(gb_ref[...].astype(F32)) * yb
    o_ref[...] = m.astype(o_ref.dtype)


def _merge(ya, yb, wa, wb, p, gate_off, tm_pref=1024, tn_pref=512):
    M, KA = ya.shape
    KB = yb.shape[1]
    N = wa.shape[1]
    tm, tn = _tile(M, tm_pref), _tile(N, tn_pref)
    while gate_off % tn:
        tn //= 2
    nb = N // tn
    g0 = gate_off // tn
    return pl.pallas_call(
        _merge_kernel,
        grid=(M // tm, nb),
        in_specs=[pl.BlockSpec((tm, KA), lambda i, j: (i, 0)),
                  pl.BlockSpec((tm, KB), lambda i, j: (i, 0)),
                  pl.BlockSpec((KA, tn), lambda i, j: (0, j)),
                  pl.BlockSpec((KB, tn), lambda i, j: (0, j)),
                  pl.BlockSpec((tm, tn), lambda i, j: (i, g0 + j)),
                  pl.BlockSpec((tm, tn), lambda i, j: (i, g0 + nb + j))],
        out_specs=pl.BlockSpec((tm, tn), lambda i, j: (i, j)),
        out_shape=jax.ShapeDtypeStruct((M, N), BF),
        compiler_params=_cparams("parallel", "arbitrary"),
        name="merge",
    )(ya, yb, wa, wb, p, p)


def _mem_kv_kernel(m_ref, nw_ref, wk_ref, wv_ref, k_ref, v_ref):
    mn = _rms(m_ref[...], nw_ref[...]).astype(BF)
    k_ref[...] = _dot(mn, wk_ref[...]).astype(k_ref.dtype)
    v_ref[...] = _dot(mn, wv_ref[...]).astype(v_ref.dtype)


def _mem_kv(mem, nw, wk, wv, tr_pref=256):
    Mm, D = mem.shape
    N = wk.shape[1]
    tr = _tile(Mm, tr_pref)
    full = lambda shape: pl.BlockSpec(shape, lambda i: (0, 0))
    return pl.pallas_call(
        _mem_kv_kernel,
        grid=(Mm // tr,),
        in_specs=[pl.BlockSpec((tr, D), lambda i: (i, 0)), full((1, D)), full((D, N)), full((D, N))],
        out_specs=[pl.BlockSpec((tr, N), lambda i: (i, 0))] * 2,
        out_shape=[jax.ShapeDtypeStruct((Mm, N), BF)] * 2,
        compiler_params=_cparams("parallel"),
        name="mem_kv",
    )(mem, nw, wk, wv)


def _xattn_kernel(h_ref, nw_ref, wq_ref, k_ref, v_ref, wo_ref, nw2_ref, o_ref, on_ref, *, n_heads):
    DH = XATTN_HEAD_DIM
    h = h_ref[...]
    q = _dot(_rms(h, nw_ref[...]).astype(BF), wq_ref[...])
    outs = []
    for hd in range(n_heads):
        cols = slice(hd * DH, (hd + 1) * DH)
        s = _dot_nt(q[:, cols].astype(BF), k_ref[:, cols]) * (DH ** -0.5)
        s = s - jnp.max(s, axis=-1, keepdims=True)
        e = jnp.exp(s)
        pr = e / jnp.sum(e, axis=-1, keepdims=True)
        outs.append(_dot(pr.astype(BF), v_ref[:, cols]))
    o = jnp.concatenate(outs, axis=1).astype(BF)
    h_out = h + _dot(o, wo_ref[...])
    o_ref[...] = h_out
    on_ref[...] = _rms(h_out, nw2_ref[...]).astype(on_ref.dtype)


def _xattn(h, nw, wq, kh, vh, wo, nw2, seq, n_mem, tm_pref=256):
    M, D = h.shape
    N = wq.shape[1]
    tm = _tile(seq, tm_pref)
    per = seq // tm
    full = lambda shape: pl.BlockSpec(shape, lambda i: (0, 0))
    kv = pl.BlockSpec((n_mem, N), lambda i: (i // per, 0))
    row = pl.BlockSpec((tm, D), lambda i: (i, 0))
    return pl.pallas_call(
        functools.partial(_xattn_kernel, n_heads=N // XATTN_HEAD_DIM),
        grid=(M // tm,),
        in_specs=[row, full((1, D)), full((D, N)), kv, kv, full((N, D)), full((1, D))],
        out_specs=[row, row],
        out_shape=[jax.ShapeDtypeStruct((M, D), F32), jax.ShapeDtypeStruct((M, D), BF)],
        compiler_params=_cparams("parallel"),
        name="xattn",
    )(h, nw, wq, kh, vh, wo, nw2)


def kernel(x, mem, ffn1_norm, ffn1_w_gate, ffn1_w_up, ffn1_w_down, mix_norm, w_in, conv_w, qkv_conv_w, a_log, dt_bias, dn_out_norm, w_out_conv, w_out_delta, w_o, xattn_norm, mem_norm, xattn_wq, xattn_wk, xattn_wv, xattn_wo, ffn2_norm, ffn2_w_gate, ffn2_w_up, ffn2_w_down, final_norm):
    B, S, D = x.shape
    depth = ffn1_norm.shape[0]
    n_mem = mem.shape[1]
    CW = conv_w.shape[-1]
    H = a_log.shape[-1]
    DW = H * DN_HEAD_DIM
    G = min(H, SUBLANES)
    M = B * S
    assert S % DN_CHUNK == 0 and 2 * H <= 128 and H % G == 0
    o_qkv = 3 * CW
    o_z = o_qkv + 3 * DW
    o_b = o_z + DW
    o_ga = o_b + 2 * H
    assert w_in.shape[-1] == o_ga + 2 * D
    vec = lambda v: v.reshape(1, -1).astype(F32)
    bf = lambda w: w.astype(BF)

    h = x.reshape(M, D)
    mem2 = mem.reshape(B * n_mem, D)
    for l in range(depth):
        h = _ffn(_norm(h, vec(ffn1_norm[l]))[0], h, ffn1_w_gate, ffn1_w_up, bf(ffn1_w_down),
                 vec(mix_norm[l]), l, norm_out=False, name="ffn1")

        w_in_t = jnp.swapaxes(w_in, 1, 2)
        w_ba = bf(jnp.pad(w_in_t[l, o_b:o_ga], ((0, 128 - 2 * H), (0, 0))))
        un, ba = _norm(h, vec(mix_norm[l]), w_ba)
        p = _matmul_wst(un, w_in_t, l, o_b, o_ga - o_b, o_b + 2 * D, BF, name="in_proj")

        ya = _conv_a(p, conv_w[l].astype(F32), S, CW)
        beta_b, gc_b, gct = _gates(ba, a_log[l], dt_bias[l], H, G, _tile(S, 256))
        qeff, wdec, ktail, ubase, obase = _dn_prep(p, qkv_conv_w[l].astype(F32), beta_b, gc_b, gct,
                                                   S, H, G, o_qkv)
        yb = _dn_scan(qeff, wdec, ktail, ubase, obase, gc_b, p, vec(dn_out_norm[l]), S, H, G, o_z)

        merged = _merge(ya, yb, bf(w_out_conv[l]), bf(w_out_delta[l]), p, o_b)
        h = _matmul(merged, bf(w_o[l]), F32, res=h, name="oproj")

        kh, vh = _mem_kv(mem2, vec(mem_norm[l]), bf(xattn_wk[l]), bf(xattn_wv[l]))
        h, xn = _xattn(h, vec(xattn_norm[l]), bf(xattn_wq[l]), kh, vh, bf(xattn_wo[l]),
                       vec(ffn2_norm[l]), S, n_mem)

        h = _ffn(xn, h, ffn2_w_gate, ffn2_w_up, bf(ffn2_w_down), vec(final_norm), l,
                 norm_out=(l == depth - 1), name="ffn2")
    return h.reshape(B, S, D)
```

```python
import functools

import jax
import jax.numpy as jnp
from jax import lax
from jax.experimental import pallas as pl
from jax.experimental.pallas import tpu as pltpu

EPS = 1e-6
BF = jnp.bfloat16
F32 = jnp.float32
HI = lax.Precision.HIGHEST

DN_HEAD_DIM = 128
DN_CHUNK = 64
XATTN_HEAD_DIM = 128
HALO = 16
SUBLANES = 8
VMEM_BYTES_V7X = 64 * 1024 * 1024
VMEM_LIMIT = VMEM_BYTES_V7X - 4 * 1024 * 1024


def _cparams(*sem):
    return pltpu.CompilerParams(dimension_semantics=sem, vmem_limit_bytes=VMEM_LIMIT)


def _tile(dim, pref):
    t = min(pref, dim)
    while dim % t:
        t //= 2
    return t


def _rms(x, w):
    return x * lax.rsqrt(jnp.mean(x * x, axis=-1, keepdims=True) + EPS) * w


def _silu(x):
    return x * jax.nn.sigmoid(x)


def _dot(a, b):
    return jnp.dot(a, b, preferred_element_type=F32)


def _dot_nt(a, b, precision=None):
    return lax.dot_general(a, b, (((1,), (1,)), ((), ())), preferred_element_type=F32,
                           precision=precision)


def _ffn_kernel(xn_ref, x_hbm, wg_ref, wu_ref, wd_ref, ow_ref, o_hbm, acc_ref, sem_in, sem_out,
                *, n_i, n_f, tm, col_chunk, row_chunk, norm_out):
    i = pl.program_id(0)
    f = pl.program_id(1)

    def rows(t):
        return pl.ds(pl.multiple_of(t * tm, tm), tm)

    def load(t):
        return pltpu.make_async_copy(x_hbm.at[rows(t)], acc_ref, sem_in)

    def store(t):
        return pltpu.make_async_copy(acc_ref, o_hbm.at[rows(t)], sem_out)

    @pl.when(f == 0)
    def _():
        @pl.when(i > 0)
        def _():
            store(i - 1).wait()

        load(i).start()

    xn = xn_ref[...]
    g = _dot(xn, wg_ref[...].astype(BF))
    u = _dot(xn, wu_ref[...].astype(BF))
    a = (0.5 * _silu(g) * u).astype(BF)

    @pl.when(f == 0)
    def _():
        load(i).wait()

    for c0 in range(0, acc_ref.shape[1], col_chunk):
        cs = slice(c0, c0 + col_chunk)
        acc_ref[:, cs] += _dot(a, wd_ref[:, cs].astype(BF))

    @pl.when(f == n_f - 1)
    def _():
        if norm_out:
            def norm_rows(r, carry):
                rs = pl.ds(pl.multiple_of(r * row_chunk, row_chunk), row_chunk)
                acc_ref[rs, :] = _rms(acc_ref[rs, :], ow_ref[...])
                return carry

            lax.fori_loop(0, tm // row_chunk, norm_rows, 0)
        store(i).start()

        @pl.when(i == n_i - 1)
        def _():
            store(i).wait()


def _ffn(xn, x, wg, wu, wd, ow, layer, *, norm_out, tm_pref=1024, tf_pref=256, name="ffn"):
    M, D = x.shape
    F = wg.shape[-1]
    tm, tf = _tile(M, tm_pref), _tile(F, tf_pref)
    n_i, n_f = M // tm, F // tf
    return pl.pallas_call(
        functools.partial(_ffn_kernel, n_i=n_i, n_f=n_f, tm=tm, col_chunk=_tile(D, 1024),
                          row_chunk=_tile(tm, 256),
                          norm_out=norm_out),
        grid=(n_i, n_f),
        in_specs=[
            pl.BlockSpec((tm, D), lambda i, f: (i, 0), pipeline_mode=pl.Buffered(1)),
            pl.BlockSpec(memory_space=pl.ANY),
            pl.BlockSpec((None, D, tf), lambda i, f: (layer, 0, f)),
            pl.BlockSpec((None, D, tf), lambda i, f: (layer, 0, f)),
            pl.BlockSpec((None, tf, D), lambda i, f: (layer, f, 0)),
            pl.BlockSpec((1, D), lambda i, f: (0, 0)),
        ],
        out_specs=pl.BlockSpec(memory_space=pl.ANY),
        out_shape=jax.ShapeDtypeStruct((M, D), F32),
        scratch_shapes=[pltpu.VMEM((tm, D), F32), pltpu.SemaphoreType.DMA(()),
                        pltpu.SemaphoreType.DMA(())],
        compiler_params=_cparams("arbitrary", "arbitrary"),
        name=name,
    )(xn, x, wg, wu, wd, ow)


def _norm_kernel(x_ref, w_ref, *refs, with_proj):
    xn = _rms(x_ref[...], w_ref[...]).astype(BF)
    if with_proj:
        wp_ref, o_ref, p_ref = refs
        p_ref[...] = _dot_nt(xn, wp_ref[...])
    else:
        (o_ref,) = refs
    o_ref[...] = xn


def _norm(x, w, w_proj=None, tr_pref=512):
    M, D = x.shape
    tr = _tile(M, tr_pref)
    in_specs = [pl.BlockSpec((tr, D), lambda i: (i, 0)), pl.BlockSpec((1, D), lambda i: (0, 0))]
    out_specs = [pl.BlockSpec((tr, D), lambda i: (i, 0))]
    out_shape = [jax.ShapeDtypeStruct((M, D), BF)]
    args = [x, w]
    if w_proj is not None:
        N = w_proj.shape[0]
        in_specs.append(pl.BlockSpec((N, D), lambda i: (0, 0)))
        out_specs.append(pl.BlockSpec((tr, N), lambda i: (i, 0)))
        out_shape.append(jax.ShapeDtypeStruct((M, N), F32))
        args.append(w_proj)
    return pl.pallas_call(
        functools.partial(_norm_kernel, with_proj=w_proj is not None),
        grid=(M // tr,),
        in_specs=in_specs,
        out_specs=out_specs,
        out_shape=out_shape,
        compiler_params=_cparams("parallel"),
        name="norm_proj" if w_proj is not None else "norm",
    )(*args)


def _mm_kernel(a_ref, w_ref, *refs, residual):
    if residual:
        res_ref, o_ref = refs
        o_ref[...] = res_ref[...] + _dot(a_ref[...], w_ref[...])
    else:
        (o_ref,) = refs
        o_ref[...] = _dot(a_ref[...], w_ref[...]).astype(o_ref.dtype)


def _matmul(a, w, out_dtype, *, res=None, tm_pref=1024, tn_pref=512, name="matmul"):
    M, K = a.shape
    N = w.shape[1]
    tm, tn = _tile(M, tm_pref), _tile(N, tn_pref)
    out_blk = pl.BlockSpec((tm, tn), lambda i, j: (i, j))
    in_specs = [pl.BlockSpec((tm, K), lambda i, j: (i, 0)),
                pl.BlockSpec((K, tn), lambda i, j: (0, j))]
    args = [a, w]
    if res is not None:
        in_specs.append(out_blk)
        args.append(res)
    return pl.pallas_call(
        functools.partial(_mm_kernel, residual=res is not None),
        grid=(M // tm, N // tn),
        in_specs=in_specs,
        out_specs=out_blk,
        out_shape=jax.ShapeDtypeStruct((M, N), out_dtype),
        compiler_params=_cparams("parallel", "arbitrary"),
        name=name,
    )(*args)


def _mm_wst_kernel(a_ref, wt_hbm, o_ref, wf32_ref, wbf_ref, sem, *, layer, tn, n_j, n_head, skip):
    j = pl.program_id(0)

    def fetch(t):
        start = pl.multiple_of(t * tn + jnp.where(t >= n_head, skip, 0), SUBLANES)
        return pltpu.make_async_copy(wt_hbm.at[layer, pl.ds(start, tn)], wf32_ref.at[t % 2],
                                     sem.at[t % 2])

    @pl.when(pl.program_id(1) == 0)
    def _():
        @pl.when(j == 0)
        def _():
            fetch(j).start()

        fetch(j).wait()
        wbf_ref[...] = wf32_ref[j % 2].astype(BF)

        @pl.when(j + 1 < n_j)
        def _():
            fetch(j + 1).start()

    o_ref[...] = _dot_nt(a_ref[...], wbf_ref[...]).astype(o_ref.dtype)


def _matmul_wst(a, wt, layer, n_head, skip, n_out, out_dtype, *, tm_pref=1024, tn_pref=512,
                name="matmul_wst"):
    M, K = a.shape
    tm, tn = _tile(M, tm_pref), _tile(n_out, tn_pref)
    assert n_head % tn == 0 and skip % SUBLANES == 0
    n_j = n_out // tn
    return pl.pallas_call(
        functools.partial(_mm_wst_kernel, layer=layer, tn=tn, n_j=n_j, n_head=n_head // tn, skip=skip),
        grid=(n_j, M // tm),
        in_specs=[pl.BlockSpec((tm, K), lambda j, i: (i, 0)),
                  pl.BlockSpec(memory_space=pl.ANY)],
        out_specs=pl.BlockSpec((tm, tn), lambda j, i: (i, j)),
        out_shape=jax.ShapeDtypeStruct((M, n_out), out_dtype),
        scratch_shapes=[pltpu.VMEM((2, tn, K), F32), pltpu.VMEM((tn, K), BF),
                        pltpu.SemaphoreType.DMA((2,))],
        compiler_params=_cparams("arbitrary", "arbitrary"),
        name=name,
    )(a, wt)


def _split3(x):
    hi = x.astype(BF)
    r1 = x - hi.astype(F32)
    mid = r1.astype(BF)
    lo = (r1 - mid.astype(F32)).astype(BF)
    return hi, mid, lo


def _gates_kernel(ba_ref, eb_ref, ea_ref, sel_ref, alog_ref, dtb_ref, beta_ref, gc_ref, gct_ref):
    R = ba_ref.shape[0]
    ba = ba_ref[...]
    beta = jax.nn.sigmoid(ba)
    x = ba + dtb_ref[...]
    softplus = jnp.maximum(x, 0.0) + jnp.log1p(jnp.exp(-jnp.abs(x)))
    g = -jnp.exp(alog_ref[...]) * softplus
    ri = lax.broadcasted_iota(jnp.int32, (R, R), 0)
    ci = lax.broadcasted_iota(jnp.int32, (R, R), 1)
    tri = ((ri // DN_CHUNK == ci // DN_CHUNK) & (ri >= ci)).astype(BF)
    gc = sum(_dot(tri, piece) for piece in _split3(g))
    gc_parts = _split3(gc)
    beta_ref[...] = sum(_dot(piece, eb_ref[...]) for piece in _split3(beta))
    gc_ref[...] = sum(_dot(piece, ea_ref[...]) for piece in gc_parts)
    gct_ref[...] = sum(_dot_nt(sel_ref[...], piece) for piece in gc_parts)


def _gates(ba, a_log, dt_bias, n_heads, heads_per_group, tile_rows):
    M = ba.shape[0]
    G, H = heads_per_group, n_heads
    W = H * DN_HEAD_DIM
    lane_head = jnp.arange(W) // DN_HEAD_DIM
    rows = jnp.arange(128)[:, None]
    eb = (rows == lane_head[None, :]).astype(BF)
    ea = (rows == lane_head[None, :] + H).astype(BF)
    n_rows = (H // G) * SUBLANES
    r = jnp.arange(n_rows)
    head_of_row = jnp.where(r % SUBLANES < G, (r // SUBLANES) * G + r % SUBLANES, -128)
    sel = (jnp.arange(128)[None, :] == head_of_row[:, None] + H).astype(BF)
    at_a = lambda v: jnp.pad(v.astype(F32), (H, 128 - 2 * H))[None, :]
    R = tile_rows
    full = lambda shape: pl.BlockSpec(shape, lambda i: (0, 0))
    return pl.pallas_call(
        _gates_kernel,
        grid=(M // R,),
        in_specs=[pl.BlockSpec((R, 128), lambda i: (i, 0)), full((128, W)), full((128, W)),
                  full((n_rows, 128)), full((1, 128)), full((1, 128))],
        out_specs=[pl.BlockSpec((R, W), lambda i: (i, 0))] * 2
        + [pl.BlockSpec((n_rows, R), lambda i: (0, i))],
        out_shape=[jax.ShapeDtypeStruct((M, W), F32)] * 2
        + [jax.ShapeDtypeStruct((n_rows, M), F32)],
        compiler_params=_cparams("parallel"),
        name="dn_gates",
    )(ba, eb, ea, sel, at_a(a_log), at_a(dt_bias))


def _shift_rows(t, halo, k):
    r = pltpu.roll(t, k, 0)
    hr = pltpu.roll(halo, k, 0)[:8]
    rows = lax.broadcasted_iota(jnp.int32, (8, t.shape[1]), 0)
    first = jnp.where(rows < k, hr, r[:8])
    return jnp.concatenate([first, r[8:]], axis=0)


def _causal_conv(t, halo, w):
    K = w.shape[0]
    y = t * w[K - 1:K]
    for d in range(1, K):
        y = y + _shift_rows(t, halo, d) * w[K - 1 - d:K - d]
    return y


def _halo_spec(rows_per_tile, width, col_block, n_lead):
    per = rows_per_tile // HALO
    if n_lead == 1:
        return pl.BlockSpec((HALO, width), lambda i: (jnp.maximum(i * per - 1, 0), col_block))
    return pl.BlockSpec((HALO, width), lambda i, g: (jnp.maximum(i * per - 1, 0), col_block(g)))


def _conv_a_kernel(cx_ref, cc_ref, cb_ref, hx_ref, hc_ref, w_ref, o_ref, *, tiles_per_seq):
    first = (pl.program_id(0) % tiles_per_seq == 0)
    keep = jnp.where(first, 0.0, 1.0)
    t = cx_ref[...].astype(F32) * cc_ref[...].astype(F32)
    halo = hx_ref[...].astype(F32) * hc_ref[...].astype(F32) * keep
    y = _causal_conv(t, halo, w_ref[...])
    o_ref[...] = (cb_ref[...].astype(F32) * y).astype(o_ref.dtype)


def _conv_a(p, conv_w, seq, width, tr_pref=256):
    M = p.shape[0]
    tr = _tile(seq, tr_pref)
    blk = lambda c: pl.BlockSpec((tr, width), lambda i: (i, c))
    return pl.pallas_call(
        functools.partial(_conv_a_kernel, tiles_per_seq=seq // tr),
        grid=(M // tr,),
        in_specs=[blk(0), blk(1), blk(2), _halo_spec(tr, width, 0, 1), _halo_spec(tr, width, 1, 1),
                  pl.BlockSpec(conv_w.shape, lambda i: (0, 0))],
        out_specs=pl.BlockSpec((tr, width), lambda i: (i, 0)),
        out_shape=jax.ShapeDtypeStruct((M, width), BF),
        compiler_params=_cparams("parallel"),
        name="conv_a",
    )(p, p, p, p, p, conv_w)


def _dn_prep_kernel(q_ref, k_ref, v_ref, hq_ref, hk_ref, hv_ref, wq_ref, wk_ref, wv_ref,
                    beta_ref, gc_ref, gct_ref, qeff_ref, wdec_ref, ktail_ref, ubase_ref, obase_ref,
                    *, tiles_per_seq, heads, wave):
    R = q_ref.shape[0]
    C, DK = DN_CHUNK, DN_HEAD_DIM
    first = (pl.program_id(0) % tiles_per_seq == 0)
    keep = jnp.where(first, 0.0, 1.0)
    ri = lax.broadcasted_iota(jnp.int32, (R, R), 0)
    ci = lax.broadcasted_iota(jnp.int32, (R, R), 1)
    same = (ri // C) == (ci // C)
    incl = same & (ri >= ci)
    strict = same & (ri > ci)

    def l2n(x):
        return x * lax.rsqrt(jnp.sum(x * x, axis=-1, keepdims=True) + EPS)

    def head_inputs(hh):
        cols = slice(hh * DK, (hh + 1) * DK)

        def conv_silu(t_ref, h_ref, w_ref):
            y = _causal_conv(t_ref[:, cols].astype(F32), h_ref[:, cols].astype(F32) * keep,
                             w_ref[:, cols])
            return _silu(y)

        q = l2n(conv_silu(q_ref, hq_ref, wq_ref)) * (DK ** -0.5)
        k = l2n(conv_silu(k_ref, hk_ref, wk_ref))
        v = conv_silu(v_ref, hv_ref, wv_ref)
        beta = beta_ref[:, cols]
        gc = gc_ref[:, cols]
        g_row = jnp.broadcast_to(gct_ref[hh:hh + 1, :], (R, R))
        g_col = jnp.concatenate([gc] * (R // DK), axis=1)
        decay = jnp.where(incl, jnp.exp(jnp.where(incl, g_col - g_row, 0.0)), 0.0)
        kb = k * beta
        k16 = k.astype(BF)
        lmat = jnp.where(strict, _dot_nt(kb.astype(BF), k16) * decay, 0.0)
        aqk = jnp.where(incl, _dot_nt(q.astype(BF), k16) * decay, 0.0).astype(BF)
        rhs = jnp.concatenate([v * beta, kb * jnp.exp(gc)], axis=1)
        return dict(cols=cols, q=q, k=k, gc=gc, aqk=aqk, rhs=rhs, m=-lmat)

    n_fac = C.bit_length() - 1
    for w0 in range(0, heads, wave):
        hs = [head_inputs(hh) for hh in range(w0, min(w0 + wave, heads))]
        for d in hs:
            m16 = d["m"].astype(BF)
            d["qm"] = d["m"]
            d["m"] = _dot(m16, m16)
        for j in range(1, n_fac):
            for d in hs:
                m16 = d["m"].astype(BF)
                if j < n_fac - 1:
                    r = _dot(jnp.concatenate([m16, d["qm"].astype(BF)], axis=0), m16)
                    d["qm"] = d["qm"] + d["m"] + r[R:]
                    d["m"] = r[:R]
                else:
                    d["qm"] = d["qm"] + d["m"] + _dot(d["qm"].astype(BF), m16)
        for d in hs:
            d["sol"] = d["rhs"] + _dot(d["qm"].astype(BF), d["rhs"].astype(BF))
        for d in hs:
            cols, sol, gc, q, k = d["cols"], d["sol"], d["gc"], d["q"], d["k"]
            ubase, wdec = sol[:, :DK], sol[:, DK:]
            x = _dot(d["aqk"], jnp.concatenate([wdec, ubase], axis=1).astype(BF))
            gc_last = jnp.concatenate(
                [jnp.broadcast_to(gc[c * C + C - 1:c * C + C, :], (C, DK)) for c in range(R // C)],
                axis=0)
            qeff_ref[:, cols] = (q * jnp.exp(gc) - x[:, :DK]).astype(qeff_ref.dtype)
            obase_ref[:, cols] = x[:, DK:]
            ubase_ref[:, cols] = ubase
            wdec_ref[:, cols] = wdec.astype(wdec_ref.dtype)
            ktail_ref[:, cols] = (k * jnp.exp(gc_last - gc)).astype(ktail_ref.dtype)


def _dn_prep(p, qkv_conv_w, beta_b, gc_b, gct, seq, n_heads, heads_per_group, q_off):
    M = p.shape[0]
    G = heads_per_group
    GW = G * DN_HEAD_DIM
    W = n_heads * DN_HEAD_DIM
    n_groups = n_heads // G
    R = _tile(seq, 256)
    assert q_off % GW == 0 and R % DN_HEAD_DIM == 0
    cols = [lambda g, o=o: q_off // GW + o * n_groups + g for o in range(3)]
    blk = lambda c: pl.BlockSpec((R, GW), lambda i, g: (i, c(g)))
    wblk = lambda o: pl.BlockSpec((qkv_conv_w.shape[0], GW), lambda i, g: (0, o * n_groups + g))
    head = pl.BlockSpec((R, GW), lambda i, g: (i, g))
    return pl.pallas_call(
        functools.partial(_dn_prep_kernel, tiles_per_seq=seq // R, heads=G, wave=min(G, 4)),
        grid=(M // R, n_groups),
        in_specs=[blk(cols[0]), blk(cols[1]), blk(cols[2]),
                  _halo_spec(R, GW, cols[0], 2), _halo_spec(R, GW, cols[1], 2),
                  _halo_spec(R, GW, cols[2], 2),
                  wblk(0), wblk(1), wblk(2), head, head,
                  pl.BlockSpec((SUBLANES, R), lambda i, g: (g, i))],
        out_specs=[head] * 5,
        out_shape=[jax.ShapeDtypeStruct((M, W), BF)] * 3 + [jax.ShapeDtypeStruct((M, W), F32)] * 2,
        compiler_params=_cparams("parallel", "parallel"),
        name="dn_prep",
    )(p, p, p, p, p, p, qkv_conv_w, qkv_conv_w, qkv_conv_w, beta_b, gc_b, gct)


def _dn_scan_kernel(qeff_ref, wdec_ref, ktail_ref, ubase_ref, obase_ref, gc_ref, z_ref, nw_ref,
                    o_ref, state_ref, *, heads):
    rows_per_step = qeff_ref.shape[0]
    C, DK = DN_CHUNK, DN_HEAD_DIM

    @pl.when(pl.program_id(2) == 0)
    def _():
        state_ref[...] = jnp.zeros_like(state_ref)

    nw = nw_ref[...]

    def chunk(c, carry):
        r0 = pl.multiple_of(c * C, C)
        rows = pl.ds(r0, C)
        tail = pl.ds(pl.multiple_of(r0 + (C - 8), 8), 8)
        for h in range(heads):
            cols = slice(h * DK, (h + 1) * DK)
            st = state_ref[h]
            lhs = jnp.concatenate([qeff_ref[rows, cols], wdec_ref[rows, cols]], axis=0)
            r = _dot(lhs, st.astype(BF))
            o = obase_ref[rows, cols] + r[:C]
            u = ubase_ref[rows, cols] - r[C:]
            g_last = jnp.exp(gc_ref[tail, cols][7:8])
            upd = lax.dot_general(ktail_ref[rows, cols], u.astype(BF), (((0,), (0,)), ((), ())),
                                  preferred_element_type=F32)
            state_ref[h] = st * g_last + upd
            zf = z_ref[rows, cols].astype(F32)
            on = o * lax.rsqrt(jnp.mean(o * o, axis=-1, keepdims=True) + EPS) * nw * _silu(zf)
            o_ref[rows, cols] = on.astype(o_ref.dtype)
        return carry

    lax.fori_loop(0, rows_per_step // C, chunk, 0)


def _dn_scan(qeff, wdec, ktail, ubase, obase, gc_b, p, dn_out_norm, seq, n_heads, heads_per_group,
             z_off, ts_pref=512):
    M, W = qeff.shape
    G = heads_per_group
    GW = G * DN_HEAD_DIM
    ts = _tile(seq, ts_pref)
    per = seq // ts
    assert z_off % GW == 0
    zb = z_off // GW
    blk = pl.BlockSpec((ts, GW), lambda b, g, t: (b * per + t, g))
    return pl.pallas_call(
        functools.partial(_dn_scan_kernel, heads=G),
        grid=(M // seq, n_heads // G, per),
        in_specs=[blk] * 6 + [pl.BlockSpec((ts, GW), lambda b, g, t: (b * per + t, zb + g)),
                              pl.BlockSpec((1, DN_HEAD_DIM), lambda b, g, t: (0, 0))],
        out_specs=blk,
        out_shape=jax.ShapeDtypeStruct((M, W), BF),
        scratch_shapes=[pltpu.VMEM((G, DN_HEAD_DIM, DN_HEAD_DIM), F32)],
        compiler_params=_cparams("parallel", "parallel", "arbitrary"),
        name="dn_scan",
    )(qeff, wdec, ktail, ubase, obase, gc_b, p, dn_out_norm)


def _merge_kernel(ya_ref, yb_ref, wa_ref, wb_ref, ga_ref, gb_ref, o_ref):
    ya = _dot(ya_ref[...], wa_ref[...])
    yb = _dot(yb_ref[...], wb_ref[...])
    m = jax.nn.sigmoid(ga_ref[...].astype(F32)) * ya + jax.nn.sigmoid(gb_ref[...].astype(F32)) * yb
    o_ref[...] = m.astype(o_ref.dtype)


def _merge(ya, yb, wa, wb, p, gate_off, tm_pref=1024, tn_pref=512):
    M, KA = ya.shape
    KB = yb.shape[1]
    N = wa.shape[1]
    tm, tn = _tile(M, tm_pref), _tile(N, tn_pref)
    while gate_off % tn:
        tn //= 2
    nb = N // tn
    g0 = gate_off // tn
    return pl.pallas_call(
        _merge_kernel,
        grid=(M // tm, nb),
        in_specs=[pl.BlockSpec((tm, KA), lambda i, j: (i, 0)),
                  pl.BlockSpec((tm, KB), lambda i, j: (i, 0)),
                  pl.BlockSpec((KA, tn), lambda i, j: (0, j)),
                  pl.BlockSpec((KB, tn), lambda i, j: (0, j)),
                  pl.BlockSpec((tm, tn), lambda i, j: (i, g0 + j)),
                  pl.BlockSpec((tm, tn), lambda i, j: (i, g0 + nb + j))],
        out_specs=pl.BlockSpec((tm, tn), lambda i, j: (i, j)),
        out_shape=jax.ShapeDtypeStruct((M, N), BF),
        compiler_params=_cparams("parallel", "arbitrary"),
        name="merge",
    )(ya, yb, wa, wb, p, p)


def _mem_kv_kernel(m_ref, nw_ref, wk_ref, wv_ref, k_ref, v_ref):
    mn = _rms(m_ref[...], nw_ref[...]).astype(BF)
    k_ref[...] = _dot(mn, wk_ref[...]).astype(k_ref.dtype)
    v_ref[...] = _dot(mn, wv_ref[...]).astype(v_ref.dtype)


def _mem_kv(mem, nw, wk, wv, tr_pref=256):
    Mm, D = mem.shape
    N = wk.shape[1]
    tr = _tile(Mm, tr_pref)
    full = lambda shape: pl.BlockSpec(shape, lambda i: (0, 0))
    return pl.pallas_call(
        _mem_kv_kernel,
        grid=(Mm // tr,),
        in_specs=[pl.BlockSpec((tr, D), lambda i: (i, 0)), full((1, D)), full((D, N)), full((D, N))],
        out_specs=[pl.BlockSpec((tr, N), lambda i: (i, 0))] * 2,
        out_shape=[jax.ShapeDtypeStruct((Mm, N), BF)] * 2,
        compiler_params=_cparams("parallel"),
        name="mem_kv",
    )(mem, nw, wk, wv)


def _xattn_kernel(h_ref, nw_ref, wq_ref, k_ref, v_ref, wo_ref, nw2_ref, o_ref, on_ref, *, n_heads):
    DH = XATTN_HEAD_DIM
    h = h_ref[...]
    q = _dot(_rms(h, nw_ref[...]).astype(BF), wq_ref[...])
    outs = []
    for hd in range(n_heads):
        cols = slice(hd * DH, (hd + 1) * DH)
        s = _dot_nt(q[:, cols].astype(BF), k_ref[:, cols]) * (DH ** -0.5)
        s = s - jnp.max(s, axis=-1, keepdims=True)
        e = jnp.exp(s)
        pr = e / jnp.sum(e, axis=-1, keepdims=True)
        outs.append(_dot(pr.astype(BF), v_ref[:, cols]))
    o = jnp.concatenate(outs, axis=1).astype(BF)
    h_out = h + _dot(o, wo_ref[...])
    o_ref[...] = h_out
    on_ref[...] = _rms(h_out, nw2_ref[...]).astype(on_ref.dtype)


def _xattn(h, nw, wq, kh, vh, wo, nw2, seq, n_mem, tm_pref=256):
    M, D = h.shape
    N = wq.shape[1]
    tm = _tile(seq, tm_pref)
    per = seq // tm
    full = lambda shape: pl.BlockSpec(shape, lambda i: (0, 0))
    kv = pl.BlockSpec((n_mem, N), lambda i: (i // per, 0))
    row = pl.BlockSpec((tm, D), lambda i: (i, 0))
    return pl.pallas_call(
        functools.partial(_xattn_kernel, n_heads=N // XATTN_HEAD_DIM),
        grid=(M // tm,),
        in_specs=[row, full((1, D)), full((D, N)), kv, kv, full((N, D)), full((1, D))],
        out_specs=[row, row],
        out_shape=[jax.ShapeDtypeStruct((M, D), F32), jax.ShapeDtypeStruct((M, D), BF)],
        compiler_params=_cparams("parallel"),
        name="xattn",
    )(h, nw, wq, kh, vh, wo, nw2)


def kernel(x, mem, ffn1_norm, ffn1_w_gate, ffn1_w_up, ffn1_w_down, mix_norm, w_in, conv_w, qkv_conv_w, a_log, dt_bias, dn_out_norm, w_out_conv, w_out_delta, w_o, xattn_norm, mem_norm, xattn_wq, xattn_wk, xattn_wv, xattn_wo, ffn2_norm, ffn2_w_gate, ffn2_w_up, ffn2_w_down, final_norm):
    B, S, D = x.shape
    depth = ffn1_norm.shape[0]
    n_mem = mem.shape[1]
    CW = conv_w.shape[-1]
    H = a_log.shape[-1]
    DW = H * DN_HEAD_DIM
    G = min(H, SUBLANES)
    M = B * S
    assert S % DN_CHUNK == 0 and 2 * H <= 128 and H % G == 0
    o_qkv = 3 * CW
    o_z = o_qkv + 3 * DW
    o_b = o_z + DW
    o_ga = o_b + 2 * H
    assert w_in.shape[-1] == o_ga + 2 * D
    vec = lambda v: v.reshape(1, -1).astype(F32)
    bf = lambda w: w.astype(BF)

    h = x.reshape(M, D)
    mem2 = mem.reshape(B * n_mem, D)
    for l in range(depth):
        h = _ffn(_norm(h, vec(ffn1_norm[l]))[0], h, ffn1_w_gate, ffn1_w_up, ffn1_w_down,
                 vec(mix_norm[l]), l, norm_out=False, name="ffn1")

        w_in_t = jnp.swapaxes(w_in, 1, 2)
        w_ba = bf(jnp.pad(w_in_t[l, o_b:o_ga], ((0, 128 - 2 * H), (0, 0))))
        un, ba = _norm(h, vec(mix_norm[l]), w_ba)
        p = _matmul_wst(un, w_in_t, l, o_b, o_ga - o_b, o_b + 2 * D, BF, name="in_proj")

        ya = _conv_a(p, conv_w[l].astype(F32), S, CW)
        beta_b, gc_b, gct = _gates(ba, a_log[l], dt_bias[l], H, G, _tile(S, 256))
        qeff, wdec, ktail, ubase, obase = _dn_prep(p, qkv_conv_w[l].astype(F32), beta_b, gc_b, gct,
                                                   S, H, G, o_qkv)
        yb = _dn_scan(qeff, wdec, ktail, ubase, obase, gc_b, p, vec(dn_out_norm[l]), S, H, G, o_z)

        merged = _merge(ya, yb, bf(w_out_conv[l]), bf(w_out_delta[l]), p, o_b)
        h = _matmul(merged, bf(w_o[l]), F32, res=h, name="oproj")

        kh, vh = _mem_kv(mem2, vec(mem_norm[l]), bf(xattn_wk[l]), bf(xattn_wv[l]))
        h, xn = _xattn(h, vec(xattn_norm[l]), bf(xattn_wq[l]), kh, vh, bf(xattn_wo[l]),
                       vec(ffn2_norm[l]), S, n_mem)

        h = _ffn(xn, h, ffn2_w_gate, ffn2_w_up, ffn2_w_down, vec(final_norm), l,
                 norm_out=(l == depth - 1), name="ffn2")
    return h.reshape(B, S, D)
```

```python
import functools

import jax
import jax.numpy as jnp
from jax import lax
from jax.experimental import pallas as pl
from jax.experimental.pallas import tpu as pltpu

EPS = 1e-6
BF = jnp.bfloat16
F32 = jnp.float32
HI = lax.Precision.HIGHEST

DN_HEAD_DIM = 128
DN_CHUNK = 64
XATTN_HEAD_DIM = 128
HALO = 16
SUBLANES = 8
VMEM_BYTES_V7X = 64 * 1024 * 1024
VMEM_LIMIT = VMEM_BYTES_V7X - 4 * 1024 * 1024


def _cparams(*sem):
    return pltpu.CompilerParams(dimension_semantics=sem, vmem_limit_bytes=VMEM_LIMIT)


def _tile(dim, pref):
    t = min(pref, dim)
    while dim % t:
        t //= 2
    return t


def _rms(x, w):
    return x * lax.rsqrt(jnp.mean(x * x, axis=-1, keepdims=True) + EPS) * w


def _silu(x):
    return x * jax.nn.sigmoid(x)


def _dot(a, b):
    return jnp.dot(a, b, preferred_element_type=F32)


def _dot_nt(a, b, precision=None):
    return lax.dot_general(a, b, (((1,), (1,)), ((), ())), preferred_element_type=F32,
                           precision=precision)


def _ffn_kernel(xn_ref, x_hbm, wg_ref, wu_ref, wd_ref, ow_ref, o_hbm, acc_ref, sem_in, sem_out,
                *, n_i, n_f, tm, col_chunk, row_chunk, norm_out):
    i = pl.program_id(0)
    f = pl.program_id(1)

    def rows(t):
        return pl.ds(pl.multiple_of(t * tm, tm), tm)

    def load(t):
        return pltpu.make_async_copy(x_hbm.at[rows(t)], acc_ref, sem_in)

    def store(t):
        return pltpu.make_async_copy(acc_ref, o_hbm.at[rows(t)], sem_out)

    @pl.when(f == 0)
    def _():
        @pl.when(i > 0)
        def _():
            store(i - 1).wait()

        load(i).start()

    xn = xn_ref[...]
    g = _dot(xn, wg_ref[...].astype(BF))
    u = _dot(xn, wu_ref[...].astype(BF))
    a = (0.5 * _silu(g) * u).astype(BF)

    @pl.when(f == 0)
    def _():
        load(i).wait()

    for c0 in range(0, acc_ref.shape[1], col_chunk):
        cs = slice(c0, c0 + col_chunk)
        acc_ref[:, cs] += _dot(a, wd_ref[:, cs].astype(BF))

    @pl.when(f == n_f - 1)
    def _():
        if norm_out:
            def norm_rows(r, carry):
                rs = pl.ds(pl.multiple_of(r * row_chunk, row_chunk), row_chunk)
                acc_ref[rs, :] = _rms(acc_ref[rs, :], ow_ref[...])
                return carry

            lax.fori_loop(0, tm // row_chunk, norm_rows, 0)
        store(i).start()

        @pl.when(i == n_i - 1)
        def _():
            store(i).wait()


def _ffn(xn, x, wg, wu, wd, ow, layer, *, norm_out, tm_pref=1024, tf_pref=256, name="ffn"):
    M, D = x.shape
    F = wg.shape[-1]
    tm, tf = _tile(M, tm_pref), _tile(F, tf_pref)
    n_i, n_f = M // tm, F // tf
    return pl.pallas_call(
        functools.partial(_ffn_kernel, n_i=n_i, n_f=n_f, tm=tm, col_chunk=_tile(D, 1024),
                          row_chunk=_tile(tm, 256),
                          norm_out=norm_out),
        grid=(n_i, n_f),
        in_specs=[
            pl.BlockSpec((tm, D), lambda i, f: (i, 0), pipeline_mode=pl.Buffered(1)),
            pl.BlockSpec(memory_space=pl.ANY),
            pl.BlockSpec((None, D, tf), lambda i, f: (layer, 0, f)),
            pl.BlockSpec((None, D, tf), lambda i, f: (layer, 0, f)),
            pl.BlockSpec((None, tf, D), lambda i, f: (layer, f, 0)),
            pl.BlockSpec((1, D), lambda i, f: (0, 0)),
        ],
        out_specs=pl.BlockSpec(memory_space=pl.ANY),
        out_shape=jax.ShapeDtypeStruct((M, D), F32),
        scratch_shapes=[pltpu.VMEM((tm, D), F32), pltpu.SemaphoreType.DMA(()),
                        pltpu.SemaphoreType.DMA(())],
        compiler_params=_cparams("arbitrary", "arbitrary"),
        name=name,
    )(xn, x, wg, wu, wd, ow)


def _norm_kernel(x_ref, w_ref, *refs, with_proj):
    xn = _rms(x_ref[...], w_ref[...]).astype(BF)
    if with_proj:
        wp_ref, o_ref, p_ref = refs
        p_ref[...] = _dot_nt(xn, wp_ref[...])
    else:
        (o_ref,) = refs
    o_ref[...] = xn


def _norm(x, w, w_proj=None, tr_pref=512):
    M, D = x.shape
    tr = _tile(M, tr_pref)
    in_specs = [pl.BlockSpec((tr, D), lambda i: (i, 0)), pl.BlockSpec((1, D), lambda i: (0, 0))]
    out_specs = [pl.BlockSpec((tr, D), lambda i: (i, 0))]
    out_shape = [jax.ShapeDtypeStruct((M, D), BF)]
    args = [x, w]
    if w_proj is not None:
        N = w_proj.shape[0]
        in_specs.append(pl.BlockSpec((N, D), lambda i: (0, 0)))
        out_specs.append(pl.BlockSpec((tr, N), lambda i: (i, 0)))
        out_shape.append(jax.ShapeDtypeStruct((M, N), F32))
        args.append(w_proj)
    return pl.pallas_call(
        functools.partial(_norm_kernel, with_proj=w_proj is not None),
        grid=(M // tr,),
        in_specs=in_specs,
        out_specs=out_specs,
        out_shape=out_shape,
        compiler_params=_cparams("parallel"),
        name="norm_proj" if w_proj is not None else "norm",
    )(*args)


def _mm_kernel(a_ref, w_ref, *refs, residual):
    if residual:
        res_ref, o_ref = refs
        o_ref[...] = res_ref[...] + _dot(a_ref[...], w_ref[...])
    else:
        (o_ref,) = refs
        o_ref[...] = _dot(a_ref[...], w_ref[...]).astype(o_ref.dtype)


def _matmul(a, w, out_dtype, *, res=None, tm_pref=1024, tn_pref=512, name="matmul"):
    M, K = a.shape
    N = w.shape[1]
    tm, tn = _tile(M, tm_pref), _tile(N, tn_pref)
    out_blk = pl.BlockSpec((tm, tn), lambda i, j: (i, j))
    in_specs = [pl.BlockSpec((tm, K), lambda i, j: (i, 0)),
                pl.BlockSpec((K, tn), lambda i, j: (0, j))]
    args = [a, w]
    if res is not None:
        in_specs.append(out_blk)
        args.append(res)
    return pl.pallas_call(
        functools.partial(_mm_kernel, residual=res is not None),
        grid=(M // tm, N // tn),
        in_specs=in_specs,
        out_specs=out_blk,
        out_shape=jax.ShapeDtypeStruct((M, N), out_dtype),
        compiler_params=_cparams("parallel", "arbitrary"),
        name=name,
    )(*args)


def _mm_ws_kernel(a_ref, w_ref, res_ref, o_ref, wbf_ref):
    @pl.when(pl.program_id(1) == 0)
    def _():
        wbf_ref[...] = w_ref[...].astype(BF)

    o_ref[...] = res_ref[...] + _dot(a_ref[...], wbf_ref[...])


def _matmul_ws(a, w, layer, res, *, tm_pref=1024, tn_pref=512, name="matmul_ws"):
    M, K = a.shape
    N = w.shape[-1]
    tm, tn = _tile(M, tm_pref), _tile(N, tn_pref)
    out_blk = pl.BlockSpec((tm, tn), lambda j, i: (i, j))
    return pl.pallas_call(
        _mm_ws_kernel,
        grid=(N // tn, M // tm),
        in_specs=[pl.BlockSpec((tm, K), lambda j, i: (i, 0)),
                  pl.BlockSpec((None, K, tn), lambda j, i: (layer, 0, j)),
                  out_blk],
        out_specs=out_blk,
        out_shape=jax.ShapeDtypeStruct((M, N), res.dtype),
        scratch_shapes=[pltpu.VMEM((K, tn), BF)],
        compiler_params=_cparams("parallel", "arbitrary"),
        name=name,
    )(a, w, res)


def _mm_wst_kernel(a_ref, wt_hbm, o_ref, wf32_ref, wbf_ref, sem, *, layer, tn, n_j, n_head, skip):
    j = pl.program_id(0)

    def fetch(t):
        start = pl.multiple_of(t * tn + jnp.where(t >= n_head, skip, 0), SUBLANES)
        return pltpu.make_async_copy(wt_hbm.at[layer, pl.ds(start, tn)], wf32_ref.at[t % 2],
                                     sem.at[t % 2])

    @pl.when(pl.program_id(1) == 0)
    def _():
        @pl.when(j == 0)
        def _():
            fetch(j).start()

        fetch(j).wait()
        wbf_ref[...] = wf32_ref[j % 2].astype(BF)

        @pl.when(j + 1 < n_j)
        def _():
            fetch(j + 1).start()

    o_ref[...] = _dot_nt(a_ref[...], wbf_ref[...]).astype(o_ref.dtype)


def _matmul_wst(a, wt, layer, n_head, skip, n_out, out_dtype, *, tm_pref=1024, tn_pref=512,
                name="matmul_wst"):
    M, K = a.shape
    tm, tn = _tile(M, tm_pref), _tile(n_out, tn_pref)
    assert n_head % tn == 0 and skip % SUBLANES == 0
    n_j = n_out // tn
    return pl.pallas_call(
        functools.partial(_mm_wst_kernel, layer=layer, tn=tn, n_j=n_j, n_head=n_head // tn, skip=skip),
        grid=(n_j, M // tm),
        in_specs=[pl.BlockSpec((tm, K), lambda j, i: (i, 0)),
                  pl.BlockSpec(memory_space=pl.ANY)],
        out_specs=pl.BlockSpec((tm, tn), lambda j, i: (i, j)),
        out_shape=jax.ShapeDtypeStruct((M, n_out), out_dtype),
        scratch_shapes=[pltpu.VMEM((2, tn, K), F32), pltpu.VMEM((tn, K), BF),
                        pltpu.SemaphoreType.DMA((2,))],
        compiler_params=_cparams("arbitrary", "arbitrary"),
        name=name,
    )(a, wt)


def _split3(x):
    hi = x.astype(BF)
    r1 = x - hi.astype(F32)
    mid = r1.astype(BF)
    lo = (r1 - mid.astype(F32)).astype(BF)
    return hi, mid, lo


def _gates_kernel(ba_ref, eb_ref, ea_ref, sel_ref, alog_ref, dtb_ref, beta_ref, gc_ref, gct_ref):
    R = ba_ref.shape[0]
    ba = ba_ref[...]
    beta = jax.nn.sigmoid(ba)
    x = ba + dtb_ref[...]
    softplus = jnp.maximum(x, 0.0) + jnp.log1p(jnp.exp(-jnp.abs(x)))
    g = -jnp.exp(alog_ref[...]) * softplus
    ri = lax.broadcasted_iota(jnp.int32, (R, R), 0)
    ci = lax.broadcasted_iota(jnp.int32, (R, R), 1)
    tri = ((ri // DN_CHUNK == ci // DN_CHUNK) & (ri >= ci)).astype(BF)
    gc = sum(_dot(tri, piece) for piece in _split3(g))
    gc_parts = _split3(gc)
    beta_ref[...] = sum(_dot(piece, eb_ref[...]) for piece in _split3(beta))
    gc_ref[...] = sum(_dot(piece, ea_ref[...]) for piece in gc_parts)
    gct_ref[...] = sum(_dot_nt(sel_ref[...], piece) for piece in gc_parts)


def _group_rows(heads_per_group):
    return -(-heads_per_group // SUBLANES) * SUBLANES


def _gates(ba, a_log, dt_bias, n_heads, heads_per_group, tile_rows):
    M = ba.shape[0]
    G, H = heads_per_group, n_heads
    W = H * DN_HEAD_DIM
    lane_head = jnp.arange(W) // DN_HEAD_DIM
    rows = jnp.arange(128)[:, None]
    eb = (rows == lane_head[None, :]).astype(BF)
    ea = (rows == lane_head[None, :] + H).astype(BF)
    rpg = _group_rows(G)
    n_rows = (H // G) * rpg
    r = jnp.arange(n_rows)
    head_of_row = jnp.where(r % rpg < G, (r // rpg) * G + r % rpg, -128)
    sel = (jnp.arange(128)[None, :] == head_of_row[:, None] + H).astype(BF)
    at_a = lambda v: jnp.pad(v.astype(F32), (H, 128 - 2 * H))[None, :]
    R = tile_rows
    full = lambda shape: pl.BlockSpec(shape, lambda i: (0, 0))
    return pl.pallas_call(
        _gates_kernel,
        grid=(M // R,),
        in_specs=[pl.BlockSpec((R, 128), lambda i: (i, 0)), full((128, W)), full((128, W)),
                  full((n_rows, 128)), full((1, 128)), full((1, 128))],
        out_specs=[pl.BlockSpec((R, W), lambda i: (i, 0))] * 2
        + [pl.BlockSpec((n_rows, R), lambda i: (0, i))],
        out_shape=[jax.ShapeDtypeStruct((M, W), F32)] * 2
        + [jax.ShapeDtypeStruct((n_rows, M), F32)],
        compiler_params=_cparams("parallel"),
        name="dn_gates",
    )(ba, eb, ea, sel, at_a(a_log), at_a(dt_bias))


def _shift_rows(t, halo, k):
    r = pltpu.roll(t, k, 0)
    hr = pltpu.roll(halo, k, 0)[:8]
    rows = lax.broadcasted_iota(jnp.int32, (8, t.shape[1]), 0)
    first = jnp.where(rows < k, hr, r[:8])
    return jnp.concatenate([first, r[8:]], axis=0)


def _causal_conv(t, halo, w):
    K = w.shape[0]
    y = t * w[K - 1:K]
    for d in range(1, K):
        y = y + _shift_rows(t, halo, d) * w[K - 1 - d:K - d]
    return y


def _halo_spec(rows_per_tile, width, col_block, n_lead):
    per = rows_per_tile // HALO
    if n_lead == 1:
        return pl.BlockSpec((HALO, width), lambda i: (jnp.maximum(i * per - 1, 0), col_block))
    return pl.BlockSpec((HALO, width), lambda i, g: (jnp.maximum(i * per - 1, 0), col_block(g)))


def _conv_a_kernel(cx_ref, cc_ref, cb_ref, hx_ref, hc_ref, w_ref, o_ref, *, tiles_per_seq):
    first = (pl.program_id(0) % tiles_per_seq == 0)
    keep = jnp.where(first, 0.0, 1.0)
    t = cx_ref[...].astype(F32) * cc_ref[...].astype(F32)
    halo = hx_ref[...].astype(F32) * hc_ref[...].astype(F32) * keep
    y = _causal_conv(t, halo, w_ref[...])
    o_ref[...] = (cb_ref[...].astype(F32) * y).astype(o_ref.dtype)


def _conv_a(p, conv_w, seq, width, tr_pref=256):
    M = p.shape[0]
    tr = _tile(seq, tr_pref)
    blk = lambda c: pl.BlockSpec((tr, width), lambda i: (i, c))
    return pl.pallas_call(
        functools.partial(_conv_a_kernel, tiles_per_seq=seq // tr),
        grid=(M // tr,),
        in_specs=[blk(0), blk(1), blk(2), _halo_spec(tr, width, 0, 1), _halo_spec(tr, width, 1, 1),
                  pl.BlockSpec(conv_w.shape, lambda i: (0, 0))],
        out_specs=pl.BlockSpec((tr, width), lambda i: (i, 0)),
        out_shape=jax.ShapeDtypeStruct((M, width), BF),
        compiler_params=_cparams("parallel"),
        name="conv_a",
    )(p, p, p, p, p, conv_w)


def _dn_prep_kernel(q_ref, k_ref, v_ref, hq_ref, hk_ref, hv_ref, wq_ref, wk_ref, wv_ref,
                    beta_ref, gc_ref, gct_ref, qeff_ref, wdec_ref, ktail_ref, ubase_ref, obase_ref,
                    *, tiles_per_seq, heads, wave):
    R = q_ref.shape[0]
    C, DK = DN_CHUNK, DN_HEAD_DIM
    first = (pl.program_id(0) % tiles_per_seq == 0)
    keep = jnp.where(first, 0.0, 1.0)
    ri = lax.broadcasted_iota(jnp.int32, (R, R), 0)
    ci = lax.broadcasted_iota(jnp.int32, (R, R), 1)
    same = (ri // C) == (ci // C)
    incl = same & (ri >= ci)
    strict = same & (ri > ci)

    def l2n(x):
        return x * lax.rsqrt(jnp.sum(x * x, axis=-1, keepdims=True) + EPS)

    def head_inputs(hh):
        cols = slice(hh * DK, (hh + 1) * DK)

        def conv_silu(t_ref, h_ref, w_ref):
            y = _causal_conv(t_ref[:, cols].astype(F32), h_ref[:, cols].astype(F32) * keep,
                             w_ref[:, cols])
            return _silu(y)

        q = l2n(conv_silu(q_ref, hq_ref, wq_ref)) * (DK ** -0.5)
        k = l2n(conv_silu(k_ref, hk_ref, wk_ref))
        v = conv_silu(v_ref, hv_ref, wv_ref)
        beta = beta_ref[:, cols]
        gc = gc_ref[:, cols]
        g_row = jnp.broadcast_to(gct_ref[hh:hh + 1, :], (R, R))
        g_col = jnp.concatenate([gc] * (R // DK), axis=1)
        decay = jnp.where(incl, jnp.exp(jnp.where(incl, g_col - g_row, 0.0)), 0.0)
        kb = k * beta
        k16 = k.astype(BF)
        lmat = jnp.where(strict, _dot_nt(kb.astype(BF), k16) * decay, 0.0)
        aqk = jnp.where(incl, _dot_nt(q.astype(BF), k16) * decay, 0.0).astype(BF)
        rhs = jnp.concatenate([v * beta, kb * jnp.exp(gc)], axis=1)
        return dict(cols=cols, q=q, k=k, gc=gc, aqk=aqk, rhs=rhs, m=-lmat)

    n_fac = C.bit_length() - 1
    waves = [list(range(w0, min(w0 + wave, heads))) for w0 in range(0, heads, wave)]
    hs = [head_inputs(hh) for hh in waves[0]]
    for wi in range(len(waves)):
        todo = list(waves[wi + 1]) if wi + 1 < len(waves) else []
        hs_next = []
        for j in range(n_fac):
            for d in hs:
                m16 = d["m"].astype(BF)
                if j == 0:
                    d["qm"] = d["m"]
                    d["m"] = _dot(m16, m16)
                elif j < n_fac - 1:
                    r = _dot(jnp.concatenate([m16, d["qm"].astype(BF)], axis=0), m16)
                    d["qm"] = d["qm"] + d["m"] + r[R:]
                    d["m"] = r[:R]
                else:
                    d["qm"] = d["qm"] + d["m"] + _dot(d["qm"].astype(BF), m16)
            n_emit = -(-len(todo) // (n_fac - j))
            hs_next += [head_inputs(hh) for hh in todo[:n_emit]]
            todo = todo[n_emit:]
        for d in hs:
            d["sol"] = d["rhs"] + _dot(d["qm"].astype(BF), d["rhs"].astype(BF))
        for d in hs:
            cols, sol, gc, q, k = d["cols"], d["sol"], d["gc"], d["q"], d["k"]
            ubase, wdec = sol[:, :DK], sol[:, DK:]
            x = _dot(d["aqk"], jnp.concatenate([wdec, ubase], axis=1).astype(BF))
            gc_last = jnp.concatenate(
                [jnp.broadcast_to(gc[c * C + C - 1:c * C + C, :], (C, DK)) for c in range(R // C)],
                axis=0)
            qeff_ref[:, cols] = (q * jnp.exp(gc) - x[:, :DK]).astype(qeff_ref.dtype)
            obase_ref[:, cols] = x[:, DK:]
            ubase_ref[:, cols] = ubase
            wdec_ref[:, cols] = wdec.astype(wdec_ref.dtype)
            ktail_ref[:, cols] = (k * jnp.exp(gc_last - gc)).astype(ktail_ref.dtype)
        hs = hs_next


def _dn_prep(p, qkv_conv_w, beta_b, gc_b, gct, seq, n_heads, heads_per_group, q_off):
    M = p.shape[0]
    G = heads_per_group
    GW = G * DN_HEAD_DIM
    W = n_heads * DN_HEAD_DIM
    n_groups = n_heads // G
    R = _tile(seq, 256)
    assert q_off % GW == 0 and R % DN_HEAD_DIM == 0
    cols = [lambda g, o=o: q_off // GW + o * n_groups + g for o in range(3)]
    blk = lambda c: pl.BlockSpec((R, GW), lambda i, g: (i, c(g)))
    wblk = lambda o: pl.BlockSpec((qkv_conv_w.shape[0], GW), lambda i, g: (0, o * n_groups + g))
    head = pl.BlockSpec((R, GW), lambda i, g: (i, g))
    return pl.pallas_call(
        functools.partial(_dn_prep_kernel, tiles_per_seq=seq // R, heads=G, wave=min(G, 4)),
        grid=(M // R, n_groups),
        in_specs=[blk(cols[0]), blk(cols[1]), blk(cols[2]),
                  _halo_spec(R, GW, cols[0], 2), _halo_spec(R, GW, cols[1], 2),
                  _halo_spec(R, GW, cols[2], 2),
                  wblk(0), wblk(1), wblk(2), head, head,
                  pl.BlockSpec((_group_rows(G), R), lambda i, g: (g, i))],
        out_specs=[head] * 5,
        out_shape=[jax.ShapeDtypeStruct((M, W), BF)] * 3 + [jax.ShapeDtypeStruct((M, W), F32)] * 2,
        compiler_params=_cparams("parallel", "parallel"),
        name="dn_prep",
    )(p, p, p, p, p, p, qkv_conv_w, qkv_conv_w, qkv_conv_w, beta_b, gc_b, gct)


def _dn_scan_kernel(qeff_ref, wdec_ref, ktail_ref, ubase_ref, obase_ref, gc_ref, z_ref, nw_ref,
                    o_ref, state_ref, *, heads):
    rows_per_step = qeff_ref.shape[0]
    C, DK = DN_CHUNK, DN_HEAD_DIM

    @pl.when(pl.program_id(2) == 0)
    def _():
        state_ref[...] = jnp.zeros_like(state_ref)

    nw = nw_ref[...]

    def chunk(c, carry):
        r0 = pl.multiple_of(c * C, C)
        rows = pl.ds(r0, C)
        tail = pl.ds(pl.multiple_of(r0 + (C - 8), 8), 8)
        for h in range(heads):
            cols = slice(h * DK, (h + 1) * DK)
            st = state_ref[h]
            lhs = jnp.concatenate([qeff_ref[rows, cols], wdec_ref[rows, cols]], axis=0)
            r = _dot(lhs, st.astype(BF))
            o = obase_ref[rows, cols] + r[:C]
            u = ubase_ref[rows, cols] - r[C:]
            g_last = jnp.exp(gc_ref[tail, cols][7:8])
            upd = lax.dot_general(ktail_ref[rows, cols], u.astype(BF), (((0,), (0,)), ((), ())),
                                  preferred_element_type=F32)
            state_ref[h] = st * g_last + upd
            zf = z_ref[rows, cols].astype(F32)
            on = o * lax.rsqrt(jnp.mean(o * o, axis=-1, keepdims=True) + EPS) * nw * _silu(zf)
            o_ref[rows, cols] = on.astype(o_ref.dtype)
        return carry

    lax.fori_loop(0, rows_per_step // C, chunk, 0)


def _dn_scan(qeff, wdec, ktail, ubase, obase, gc_b, p, dn_out_norm, seq, n_heads, heads_per_group,
             z_off, ts_pref=512):
    M, W = qeff.shape
    G = heads_per_group
    GW = G * DN_HEAD_DIM
    ts = _tile(seq, ts_pref)
    per = seq // ts
    assert z_off % GW == 0
    zb = z_off // GW
    blk = pl.BlockSpec((ts, GW), lambda b, g, t: (b * per + t, g))
    return pl.pallas_call(
        functools.partial(_dn_scan_kernel, heads=G),
        grid=(M // seq, n_heads // G, per),
        in_specs=[blk] * 6 + [pl.BlockSpec((ts, GW), lambda b, g, t: (b * per + t, zb + g)),
                              pl.BlockSpec((1, DN_HEAD_DIM), lambda b, g, t: (0, 0))],
        out_specs=blk,
        out_shape=jax.ShapeDtypeStruct((M, W), BF),
        scratch_shapes=[pltpu.VMEM((G, DN_HEAD_DIM, DN_HEAD_DIM), F32)],
        compiler_params=_cparams("parallel", "parallel", "arbitrary"),
        name="dn_scan",
    )(qeff, wdec, ktail, ubase, obase, gc_b, p, dn_out_norm)


def _merge_kernel(ya_ref, yb_ref, wa_ref, wb_ref, ga_ref, gb_ref, o_ref):
    ya = _dot(ya_ref[...], wa_ref[...])
    yb = _dot(yb_ref[...], wb_ref[...])
    m = jax.nn.sigmoid(ga_ref[...].astype(F32)) * ya + jax.nn.sigmoid(gb_ref[...].astype(F32)) * yb
    o_ref[...] = m.astype(o_ref.dtype)


def _merge(ya, yb, wa, wb, p, gate_off, tm_pref=1024, tn_pref=512):
    M, KA = ya.shape
    KB = yb.shape[1]
    N = wa.shape[1]
    tm, tn = _tile(M, tm_pref), _tile(N, tn_pref)
    while gate_off % tn:
        tn //= 2
    nb = N // tn
    g0 = gate_off // tn
    return pl.pallas_call(
        _merge_kernel,
        grid=(M // tm, nb),
        in_specs=[pl.BlockSpec((tm, KA), lambda i, j: (i, 0)),
                  pl.BlockSpec((tm, KB), lambda i, j: (i, 0)),
                  pl.BlockSpec((KA, tn), lambda i, j: (0, j)),
                  pl.BlockSpec((KB, tn), lambda i, j: (0, j)),
                  pl.BlockSpec((tm, tn), lambda i, j: (i, g0 + j)),
                  pl.BlockSpec((tm, tn), lambda i, j: (i, g0 + nb + j))],
        out_specs=pl.BlockSpec((tm, tn), lambda i, j: (i, j)),
        out_shape=jax.ShapeDtypeStruct((M, N), BF),
        compiler_params=_cparams("parallel", "arbitrary"),
        name="merge",
    )(ya, yb, wa, wb, p, p)


def _mem_kv_kernel(m_ref, nw_ref, wk_ref, wv_ref, k_ref, v_ref):
    mn = _rms(m_ref[...], nw_ref[...]).astype(BF)
    k_ref[...] = _dot(mn, wk_ref[...]).astype(k_ref.dtype)
    v_ref[...] = _dot(mn, wv_ref[...]).astype(v_ref.dtype)


def _mem_kv(mem, nw, wk, wv, tr_pref=256):
    Mm, D = mem.shape
    N = wk.shape[1]
    tr = _tile(Mm, tr_pref)
    full = lambda shape: pl.BlockSpec(shape, lambda i: (0, 0))
    return pl.pallas_call(
        _mem_kv_kernel,
        grid=(Mm // tr,),
        in_specs=[pl.BlockSpec((tr, D), lambda i: (i, 0)), full((1, D)), full((D, N)), full((D, N))],
        out_specs=[pl.BlockSpec((tr, N), lambda i: (i, 0))] * 2,
        out_shape=[jax.ShapeDtypeStruct((Mm, N), BF)] * 2,
        compiler_params=_cparams("parallel"),
        name="mem_kv",
    )(mem, nw, wk, wv)


def _xattn_kernel(h_ref, nw_ref, wq_ref, k_ref, v_ref, wo_ref, nw2_ref, o_ref, on_ref, *, n_heads):
    DH = XATTN_HEAD_DIM
    h = h_ref[...]
    q = _dot(_rms(h, nw_ref[...]).astype(BF), wq_ref[...])
    outs = []
    for hd in range(n_heads):
        cols = slice(hd * DH, (hd + 1) * DH)
        s = _dot_nt(q[:, cols].astype(BF), k_ref[:, cols]) * (DH ** -0.5)
        s = s - jnp.max(s, axis=-1, keepdims=True)
        e = jnp.exp(s)
        pr = e / jnp.sum(e, axis=-1, keepdims=True)
        outs.append(_dot(pr.astype(BF), v_ref[:, cols]))
    o = jnp.concatenate(outs, axis=1).astype(BF)
    h_out = h + _dot(o, wo_ref[...])
    o_ref[...] = h_out
    on_ref[...] = _rms(h_out, nw2_ref[...]).astype(on_ref.dtype)


def _xattn(h, nw, wq, kh, vh, wo, nw2, seq, n_mem, tm_pref=256):
    M, D = h.shape
    N = wq.shape[1]
    tm = _tile(seq, tm_pref)
    per = seq // tm
    full = lambda shape: pl.BlockSpec(shape, lambda i: (0, 0))
    kv = pl.BlockSpec((n_mem, N), lambda i: (i // per, 0))
    row = pl.BlockSpec((tm, D), lambda i: (i, 0))
    return pl.pallas_call(
        functools.partial(_xattn_kernel, n_heads=N // XATTN_HEAD_DIM),
        grid=(M // tm,),
        in_specs=[row, full((1, D)), full((D, N)), kv, kv, full((N, D)), full((1, D))],
        out_specs=[row, row],
        out_shape=[jax.ShapeDtypeStruct((M, D), F32), jax.ShapeDtypeStruct((M, D), BF)],
        compiler_params=_cparams("parallel"),
        name="xattn",
    )(h, nw, wq, kh, vh, wo, nw2)


def kernel(x, mem, ffn1_norm, ffn1_w_gate, ffn1_w_up, ffn1_w_down, mix_norm, w_in, conv_w, qkv_conv_w, a_log, dt_bias, dn_out_norm, w_out_conv, w_out_delta, w_o, xattn_norm, mem_norm, xattn_wq, xattn_wk, xattn_wv, xattn_wo, ffn2_norm, ffn2_w_gate, ffn2_w_up, ffn2_w_down, final_norm):
    B, S, D = x.shape
    depth = ffn1_norm.shape[0]
    n_mem = mem.shape[1]
    CW = conv_w.shape[-1]
    H = a_log.shape[-1]
    DW = H * DN_HEAD_DIM
    G = min(H, 16)
    M = B * S
    assert S % DN_CHUNK == 0 and 2 * H <= 128 and H % G == 0
    o_qkv = 3 * CW
    o_z = o_qkv + 3 * DW
    o_b = o_z + DW
    o_ga = o_b + 2 * H
    assert w_in.shape[-1] == o_ga + 2 * D
    vec = lambda v: v.reshape(1, -1).astype(F32)
    bf = lambda w: w.astype(BF)

    h = x.reshape(M, D)
    mem2 = mem.reshape(B * n_mem, D)
    for l in range(depth):
        h = _ffn(_norm(h, vec(ffn1_norm[l]))[0], h, ffn1_w_gate, ffn1_w_up, ffn1_w_down,
                 vec(mix_norm[l]), l, norm_out=False, name="ffn1")

        w_in_t = jnp.swapaxes(w_in, 1, 2)
        w_ba = bf(jnp.pad(w_in_t[l, o_b:o_ga], ((0, 128 - 2 * H), (0, 0))))
        un, ba = _norm(h, vec(mix_norm[l]), w_ba)
        p = _matmul_wst(un, w_in_t, l, o_b, o_ga - o_b, o_b + 2 * D, BF, name="in_proj")

        ya = _conv_a(p, conv_w[l].astype(F32), S, CW)
        beta_b, gc_b, gct = _gates(ba, a_log[l], dt_bias[l], H, G, _tile(S, 256))
        qeff, wdec, ktail, ubase, obase = _dn_prep(p, qkv_conv_w[l].astype(F32), beta_b, gc_b, gct,
                                                   S, H, G, o_qkv)
        yb = _dn_scan(qeff, wdec, ktail, ubase, obase, gc_b, p, vec(dn_out_norm[l]), S, H, G, o_z)

        merged = _merge(ya, yb, bf(w_out_conv[l]), bf(w_out_delta[l]), p, o_b)
        h = _matmul_ws(merged, w_o, l, h, name="oproj")

        kh, vh = _mem_kv(mem2, vec(mem_norm[l]), bf(xattn_wk[l]), bf(xattn_wv[l]))
        h, xn = _xattn(h, vec(xattn_norm[l]), bf(xattn_wq[l]), kh, vh, bf(xattn_wo[l]),
                       vec(ffn2_norm[l]), S, n_mem)

        h = _ffn(xn, h, ffn2_w_gate, ffn2_w_up, ffn2_w_down, vec(final_norm), l,
                 norm_out=(l == depth - 1), name="ffn2")
    return h.reshape(B, S, D)
```

```python
import functools

import jax
import jax.numpy as jnp
from jax import lax
from jax.experimental import pallas as pl
from jax.experimental.pallas import tpu as pltpu

EPS = 1e-6
BF = jnp.bfloat16
F32 = jnp.float32
HI = lax.Precision.HIGHEST

DN_HEAD_DIM = 128
DN_CHUNK = 64
XATTN_HEAD_DIM = 128
HALO = 16
SUBLANES = 8
VMEM_BYTES_V7X = 64 * 1024 * 1024
VMEM_LIMIT = VMEM_BYTES_V7X - 4 * 1024 * 1024


def _cparams(*sem):
    return pltpu.CompilerParams(dimension_semantics=sem, vmem_limit_bytes=VMEM_LIMIT)


def _tile(dim, pref):
    t = min(pref, dim)
    while dim % t:
        t //= 2
    return t


def _rms(x, w):
    return x * lax.rsqrt(jnp.mean(x * x, axis=-1, keepdims=True) + EPS) * w


def _silu(x):
    return x * jax.nn.sigmoid(x)


def _dot(a, b):
    return jnp.dot(a, b, preferred_element_type=F32)


def _dot_nt(a, b, precision=None):
    return lax.dot_general(a, b, (((1,), (1,)), ((), ())), preferred_element_type=F32,
                           precision=precision)


def _ffn_kernel(xn_ref, x_hbm, wg_ref, wu_ref, wd_ref, ow_ref, o_hbm, acc_ref, sem_in, sem_out,
                *, n_i, n_f, tm, col_chunk, row_chunk, norm_out):
    i = pl.program_id(0)
    f = pl.program_id(1)

    def rows(t):
        return pl.ds(pl.multiple_of(t * tm, tm), tm)

    def load(t):
        return pltpu.make_async_copy(x_hbm.at[rows(t)], acc_ref, sem_in)

    def store(t):
        return pltpu.make_async_copy(acc_ref, o_hbm.at[rows(t)], sem_out)

    @pl.when(f == 0)
    def _():
        @pl.when(i > 0)
        def _():
            store(i - 1).wait()

        load(i).start()

    xn = xn_ref[...]
    g = _dot(xn, wg_ref[...].astype(BF))
    u = _dot(xn, wu_ref[...].astype(BF))
    a = (0.5 * _silu(g) * u).astype(BF)

    @pl.when(f == 0)
    def _():
        load(i).wait()

    for c0 in range(0, acc_ref.shape[1], col_chunk):
        cs = slice(c0, c0 + col_chunk)
        acc_ref[:, cs] += _dot(a, wd_ref[:, cs].astype(BF))

    @pl.when(f == n_f - 1)
    def _():
        if norm_out:
            def norm_rows(r, carry):
                rs = pl.ds(pl.multiple_of(r * row_chunk, row_chunk), row_chunk)
                acc_ref[rs, :] = _rms(acc_ref[rs, :], ow_ref[...])
                return carry

            lax.fori_loop(0, tm // row_chunk, norm_rows, 0)
        store(i).start()

        @pl.when(i == n_i - 1)
        def _():
            store(i).wait()


def _ffn(xn, x, wg, wu, wd, ow, layer, *, norm_out, tm_pref=1024, tf_pref=256, name="ffn"):
    M, D = x.shape
    F = wg.shape[-1]
    tm, tf = _tile(M, tm_pref), _tile(F, tf_pref)
    n_i, n_f = M // tm, F // tf
    return pl.pallas_call(
        functools.partial(_ffn_kernel, n_i=n_i, n_f=n_f, tm=tm, col_chunk=_tile(D, 1024),
                          row_chunk=_tile(tm, 256),
                          norm_out=norm_out),
        grid=(n_i, n_f),
        in_specs=[
            pl.BlockSpec((tm, D), lambda i, f: (i, 0), pipeline_mode=pl.Buffered(1)),
            pl.BlockSpec(memory_space=pl.ANY),
            pl.BlockSpec((None, D, tf), lambda i, f: (layer, 0, f)),
            pl.BlockSpec((None, D, tf), lambda i, f: (layer, 0, f)),
            pl.BlockSpec((None, tf, D), lambda i, f: (layer, f, 0)),
            pl.BlockSpec((1, D), lambda i, f: (0, 0)),
        ],
        out_specs=pl.BlockSpec(memory_space=pl.ANY),
        out_shape=jax.ShapeDtypeStruct((M, D), F32),
        scratch_shapes=[pltpu.VMEM((tm, D), F32), pltpu.SemaphoreType.DMA(()),
                        pltpu.SemaphoreType.DMA(())],
        compiler_params=_cparams("arbitrary", "arbitrary"),
        name=name,
    )(xn, x, wg, wu, wd, ow)


def _norm_kernel(x_ref, w_ref, *refs, with_proj):
    xn = _rms(x_ref[...], w_ref[...]).astype(BF)
    if with_proj:
        wp_ref, o_ref, p_ref = refs
        p_ref[...] = _dot_nt(xn, wp_ref[...])
    else:
        (o_ref,) = refs
    o_ref[...] = xn


def _norm(x, w, w_proj=None, tr_pref=512):
    M, D = x.shape
    tr = _tile(M, tr_pref)
    in_specs = [pl.BlockSpec((tr, D), lambda i: (i, 0)), pl.BlockSpec((1, D), lambda i: (0, 0))]
    out_specs = [pl.BlockSpec((tr, D), lambda i: (i, 0))]
    out_shape = [jax.ShapeDtypeStruct((M, D), BF)]
    args = [x, w]
    if w_proj is not None:
        N = w_proj.shape[0]
        in_specs.append(pl.BlockSpec((N, D), lambda i: (0, 0)))
        out_specs.append(pl.BlockSpec((tr, N), lambda i: (i, 0)))
        out_shape.append(jax.ShapeDtypeStruct((M, N), F32))
        args.append(w_proj)
    return pl.pallas_call(
        functools.partial(_norm_kernel, with_proj=w_proj is not None),
        grid=(M // tr,),
        in_specs=in_specs,
        out_specs=out_specs,
        out_shape=out_shape,
        compiler_params=_cparams("parallel"),
        name="norm_proj" if w_proj is not None else "norm",
    )(*args)


def _mm_kernel(a_ref, w_ref, *refs, residual):
    if residual:
        res_ref, o_ref = refs
        o_ref[...] = res_ref[...] + _dot(a_ref[...], w_ref[...])
    else:
        (o_ref,) = refs
        o_ref[...] = _dot(a_ref[...], w_ref[...]).astype(o_ref.dtype)


def _matmul(a, w, out_dtype, *, res=None, tm_pref=1024, tn_pref=512, name="matmul"):
    M, K = a.shape
    N = w.shape[1]
    tm, tn = _tile(M, tm_pref), _tile(N, tn_pref)
    out_blk = pl.BlockSpec((tm, tn), lambda i, j: (i, j))
    in_specs = [pl.BlockSpec((tm, K), lambda i, j: (i, 0)),
                pl.BlockSpec((K, tn), lambda i, j: (0, j))]
    args = [a, w]
    if res is not None:
        in_specs.append(out_blk)
        args.append(res)
    return pl.pallas_call(
        functools.partial(_mm_kernel, residual=res is not None),
        grid=(M // tm, N // tn),
        in_specs=in_specs,
        out_specs=out_blk,
        out_shape=jax.ShapeDtypeStruct((M, N), out_dtype),
        compiler_params=_cparams("parallel", "arbitrary"),
        name=name,
    )(*args)


def _mm_ws_kernel(a_ref, w_ref, res_ref, o_ref, wbf_ref):
    @pl.when(pl.program_id(1) == 0)
    def _():
        wbf_ref[...] = w_ref[...].astype(BF)

    o_ref[...] = res_ref[...] + _dot(a_ref[...], wbf_ref[...])


def _matmul_ws(a, w, layer, res, *, tm_pref=1024, tn_pref=512, name="matmul_ws"):
    M, K = a.shape
    N = w.shape[-1]
    tm, tn = _tile(M, tm_pref), _tile(N, tn_pref)
    out_blk = pl.BlockSpec((tm, tn), lambda j, i: (i, j))
    return pl.pallas_call(
        _mm_ws_kernel,
        grid=(N // tn, M // tm),
        in_specs=[pl.BlockSpec((tm, K), lambda j, i: (i, 0)),
                  pl.BlockSpec((None, K, tn), lambda j, i: (layer, 0, j)),
                  out_blk],
        out_specs=out_blk,
        out_shape=jax.ShapeDtypeStruct((M, N), res.dtype),
        scratch_shapes=[pltpu.VMEM((K, tn), BF)],
        compiler_params=_cparams("parallel", "arbitrary"),
        name=name,
    )(a, w, res)


def _mm_wst_kernel(a_ref, wt_hbm, o_ref, wf32_ref, wbf_ref, sem, *, layer, tn, n_j, n_head, skip):
    j = pl.program_id(0)

    def fetch(t):
        start = pl.multiple_of(t * tn + jnp.where(t >= n_head, skip, 0), SUBLANES)
        return pltpu.make_async_copy(wt_hbm.at[layer, pl.ds(start, tn)], wf32_ref.at[t % 2],
                                     sem.at[t % 2])

    @pl.when(pl.program_id(1) == 0)
    def _():
        @pl.when(j == 0)
        def _():
            fetch(j).start()

        fetch(j).wait()
        wbf_ref[...] = wf32_ref[j % 2].astype(BF)

        @pl.when(j + 1 < n_j)
        def _():
            fetch(j + 1).start()

    o_ref[...] = _dot_nt(a_ref[...], wbf_ref[...]).astype(o_ref.dtype)


def _matmul_wst(a, wt, layer, n_head, skip, n_out, out_dtype, *, tm_pref=1024, tn_pref=512,
                name="matmul_wst"):
    M, K = a.shape
    tm, tn = _tile(M, tm_pref), _tile(n_out, tn_pref)
    assert n_head % tn == 0 and skip % SUBLANES == 0
    n_j = n_out // tn
    return pl.pallas_call(
        functools.partial(_mm_wst_kernel, layer=layer, tn=tn, n_j=n_j, n_head=n_head // tn, skip=skip),
        grid=(n_j, M // tm),
        in_specs=[pl.BlockSpec((tm, K), lambda j, i: (i, 0)),
                  pl.BlockSpec(memory_space=pl.ANY)],
        out_specs=pl.BlockSpec((tm, tn), lambda j, i: (i, j)),
        out_shape=jax.ShapeDtypeStruct((M, n_out), out_dtype),
        scratch_shapes=[pltpu.VMEM((2, tn, K), F32), pltpu.VMEM((tn, K), BF),
                        pltpu.SemaphoreType.DMA((2,))],
        compiler_params=_cparams("arbitrary", "arbitrary"),
        name=name,
    )(a, wt)


def _split3(x):
    hi = x.astype(BF)
    r1 = x - hi.astype(F32)
    mid = r1.astype(BF)
    lo = (r1 - mid.astype(F32)).astype(BF)
    return hi, mid, lo


def _gate_terms(ba, eb, ea, sel, alog, dtb):
    R = ba.shape[0]
    beta = jax.nn.sigmoid(ba)
    x = ba + dtb
    softplus = jnp.maximum(x, 0.0) + jnp.log1p(jnp.exp(-jnp.abs(x)))
    g = -jnp.exp(alog) * softplus
    ri = lax.broadcasted_iota(jnp.int32, (R, R), 0)
    ci = lax.broadcasted_iota(jnp.int32, (R, R), 1)
    tri = ((ri // DN_CHUNK == ci // DN_CHUNK) & (ri >= ci)).astype(BF)
    gc = sum(_dot(tri, piece) for piece in _split3(g))
    gc_parts = _split3(gc)
    beta_b = sum(_dot(piece, eb) for piece in _split3(beta))
    gc_b = sum(_dot(piece, ea) for piece in gc_parts)
    gc_t = sum(_dot_nt(sel, piece) for piece in gc_parts)
    return beta_b, gc_b, gc_t


def _group_rows(heads_per_group):
    return -(-heads_per_group // SUBLANES) * SUBLANES


def _gate_constants(a_log, dt_bias, n_heads, heads_per_group):
    G, H = heads_per_group, n_heads
    W = H * DN_HEAD_DIM
    lane_head = jnp.arange(W) // DN_HEAD_DIM
    rows = jnp.arange(128)[:, None]
    eb = (rows == lane_head[None, :]).astype(BF)
    ea = (rows == lane_head[None, :] + H).astype(BF)
    rpg = _group_rows(G)
    r = jnp.arange((H // G) * rpg)
    head_of_row = jnp.where(r % rpg < G, (r // rpg) * G + r % rpg, -128)
    sel = (jnp.arange(128)[None, :] == head_of_row[:, None] + H).astype(BF)
    at_a = lambda v: jnp.pad(v.astype(F32), (H, 128 - 2 * H))[None, :]
    return eb, ea, sel, at_a(a_log), at_a(dt_bias)


def _shift_rows(t, halo, k):
    r = pltpu.roll(t, k, 0)
    hr = pltpu.roll(halo, k, 0)[:8]
    rows = lax.broadcasted_iota(jnp.int32, (8, t.shape[1]), 0)
    first = jnp.where(rows < k, hr, r[:8])
    return jnp.concatenate([first, r[8:]], axis=0)


def _causal_conv(t, halo, w):
    K = w.shape[0]
    y = t * w[K - 1:K]
    for d in range(1, K):
        y = y + _shift_rows(t, halo, d) * w[K - 1 - d:K - d]
    return y


def _halo_spec(rows_per_tile, width, col_block, n_lead):
    per = rows_per_tile // HALO
    if n_lead == 1:
        return pl.BlockSpec((HALO, width), lambda i: (jnp.maximum(i * per - 1, 0), col_block))
    return pl.BlockSpec((HALO, width), lambda i, g: (jnp.maximum(i * per - 1, 0), col_block(g)))


def _conv_a_kernel(cx_ref, cc_ref, cb_ref, hx_ref, hc_ref, w_ref, o_ref, *, tiles_per_seq):
    first = (pl.program_id(0) % tiles_per_seq == 0)
    keep = jnp.where(first, 0.0, 1.0)
    t = cx_ref[...].astype(F32) * cc_ref[...].astype(F32)
    halo = hx_ref[...].astype(F32) * hc_ref[...].astype(F32) * keep
    y = _causal_conv(t, halo, w_ref[...])
    o_ref[...] = (cb_ref[...].astype(F32) * y).astype(o_ref.dtype)


def _conv_a(p, conv_w, seq, width, tr_pref=256):
    M = p.shape[0]
    tr = _tile(seq, tr_pref)
    blk = lambda c: pl.BlockSpec((tr, width), lambda i: (i, c))
    return pl.pallas_call(
        functools.partial(_conv_a_kernel, tiles_per_seq=seq // tr),
        grid=(M // tr,),
        in_specs=[blk(0), blk(1), blk(2), _halo_spec(tr, width, 0, 1), _halo_spec(tr, width, 1, 1),
                  pl.BlockSpec(conv_w.shape, lambda i: (0, 0))],
        out_specs=pl.BlockSpec((tr, width), lambda i: (i, 0)),
        out_shape=jax.ShapeDtypeStruct((M, width), BF),
        compiler_params=_cparams("parallel"),
        name="conv_a",
    )(p, p, p, p, p, conv_w)


def _dn_prep_kernel(q_ref, k_ref, v_ref, hq_ref, hk_ref, hv_ref, wq_ref, wk_ref, wv_ref,
                    ba_ref, eb_ref, ea_ref, sel_ref, alog_ref, dtb_ref,
                    qeff_ref, wdec_ref, ktail_ref, ubase_ref, obase_ref, glast_ref,
                    *, tiles_per_seq, heads, wave):
    R = q_ref.shape[0]
    C, DK = DN_CHUNK, DN_HEAD_DIM
    first = (pl.program_id(0) % tiles_per_seq == 0)
    keep = jnp.where(first, 0.0, 1.0)
    beta_b, gc_b, gc_t = _gate_terms(ba_ref[...], eb_ref[...], ea_ref[...], sel_ref[...],
                                     alog_ref[...], dtb_ref[...])
    for c in range(R // C):
        glast_ref[c] = jnp.exp(gc_b[c * C + C - 1:c * C + C, :])
    ri = lax.broadcasted_iota(jnp.int32, (R, R), 0)
    ci = lax.broadcasted_iota(jnp.int32, (R, R), 1)
    same = (ri // C) == (ci // C)
    incl = same & (ri >= ci)
    strict = same & (ri > ci)

    def l2n(x):
        return x * lax.rsqrt(jnp.sum(x * x, axis=-1, keepdims=True) + EPS)

    def head_inputs(hh):
        cols = slice(hh * DK, (hh + 1) * DK)

        def conv_silu(t_ref, h_ref, w_ref):
            y = _causal_conv(t_ref[:, cols].astype(F32), h_ref[:, cols].astype(F32) * keep,
                             w_ref[:, cols])
            return _silu(y)

        q = l2n(conv_silu(q_ref, hq_ref, wq_ref)) * (DK ** -0.5)
        k = l2n(conv_silu(k_ref, hk_ref, wk_ref))
        v = conv_silu(v_ref, hv_ref, wv_ref)
        beta = beta_b[:, cols]
        gc = gc_b[:, cols]
        g_row = jnp.broadcast_to(gc_t[hh:hh + 1, :], (R, R))
        g_col = jnp.concatenate([gc] * (R // DK), axis=1)
        decay = jnp.where(incl, jnp.exp(jnp.where(incl, g_col - g_row, 0.0)), 0.0)
        kb = k * beta
        k16 = k.astype(BF)
        lmat = jnp.where(strict, _dot_nt(kb.astype(BF), k16) * decay, 0.0)
        aqk = jnp.where(incl, _dot_nt(q.astype(BF), k16) * decay, 0.0).astype(BF)
        rhs = jnp.concatenate([v * beta, kb * jnp.exp(gc)], axis=1)
        return dict(cols=cols, q=q, k=k, gc=gc, aqk=aqk, rhs=rhs, m=-lmat)

    n_fac = C.bit_length() - 1
    waves = [list(range(w0, min(w0 + wave, heads))) for w0 in range(0, heads, wave)]
    hs = [head_inputs(hh) for hh in waves[0]]
    for wi in range(len(waves)):
        todo = list(waves[wi + 1]) if wi + 1 < len(waves) else []
        hs_next = []
        for j in range(n_fac):
            for d in hs:
                m16 = d["m"].astype(BF)
                if j == 0:
                    d["qm"] = d["m"]
                    d["m"] = _dot(m16, m16)
                elif j < n_fac - 1:
                    r = _dot(jnp.concatenate([m16, d["qm"].astype(BF)], axis=0), m16)
                    d["qm"] = d["qm"] + d["m"] + r[R:]
                    d["m"] = r[:R]
                else:
                    d["qm"] = d["qm"] + d["m"] + _dot(d["qm"].astype(BF), m16)
            n_emit = -(-len(todo) // (n_fac - j))
            hs_next += [head_inputs(hh) for hh in todo[:n_emit]]
            todo = todo[n_emit:]
        for d in hs:
            d["sol"] = d["rhs"] + _dot(d["qm"].astype(BF), d["rhs"].astype(BF))
        for d in hs:
            cols, sol, gc, q, k = d["cols"], d["sol"], d["gc"], d["q"], d["k"]
            ubase, wdec = sol[:, :DK], sol[:, DK:]
            x = _dot(d["aqk"], jnp.concatenate([wdec, ubase], axis=1).astype(BF))
            gc_last = jnp.concatenate(
                [jnp.broadcast_to(gc[c * C + C - 1:c * C + C, :], (C, DK)) for c in range(R // C)],
                axis=0)
            qeff_ref[:, cols] = (q * jnp.exp(gc) - x[:, :DK]).astype(qeff_ref.dtype)
            obase_ref[:, cols] = x[:, DK:]
            ubase_ref[:, cols] = ubase
            wdec_ref[:, cols] = wdec.astype(wdec_ref.dtype)
            ktail_ref[:, cols] = (k * jnp.exp(gc_last - gc)).astype(ktail_ref.dtype)
        hs = hs_next


def _dn_prep(p, qkv_conv_w, ba, a_log, dt_bias, seq, n_heads, heads_per_group, q_off):
    M = p.shape[0]
    G = heads_per_group
    GW = G * DN_HEAD_DIM
    W = n_heads * DN_HEAD_DIM
    n_groups = n_heads // G
    R = _tile(seq, 256)
    rpg = _group_rows(G)
    assert q_off % GW == 0 and R % DN_HEAD_DIM == 0
    eb, ea, sel, alog_c, dtb_c = _gate_constants(a_log, dt_bias, n_heads, G)
    cols = [lambda g, o=o: q_off // GW + o * n_groups + g for o in range(3)]
    blk = lambda c: pl.BlockSpec((R, GW), lambda i, g: (i, c(g)))
    wblk = lambda o: pl.BlockSpec((qkv_conv_w.shape[0], GW), lambda i, g: (0, o * n_groups + g))
    head = pl.BlockSpec((R, GW), lambda i, g: (i, g))
    lanes = pl.BlockSpec((1, 128), lambda i, g: (0, 0))
    n_ch = R // DN_CHUNK
    return pl.pallas_call(
        functools.partial(_dn_prep_kernel, tiles_per_seq=seq // R, heads=G, wave=min(G, 4)),
        grid=(M // R, n_groups),
        in_specs=[blk(cols[0]), blk(cols[1]), blk(cols[2]),
                  _halo_spec(R, GW, cols[0], 2), _halo_spec(R, GW, cols[1], 2),
                  _halo_spec(R, GW, cols[2], 2),
                  wblk(0), wblk(1), wblk(2),
                  pl.BlockSpec((R, 128), lambda i, g: (i, 0)),
                  pl.BlockSpec((128, GW), lambda i, g: (0, g)),
                  pl.BlockSpec((128, GW), lambda i, g: (0, g)),
                  pl.BlockSpec((rpg, 128), lambda i, g: (g, 0)), lanes, lanes],
        out_specs=[head] * 5 + [pl.BlockSpec((n_ch, 1, GW), lambda i, g: (i, 0, g))],
        out_shape=[jax.ShapeDtypeStruct((M, W), BF)] * 3 + [jax.ShapeDtypeStruct((M, W), F32)] * 2
        + [jax.ShapeDtypeStruct((M // DN_CHUNK, 1, W), F32)],
        compiler_params=_cparams("parallel", "parallel"),
        name="dn_prep",
    )(p, p, p, p, p, p, qkv_conv_w, qkv_conv_w, qkv_conv_w, ba, eb, ea, sel, alog_c, dtb_c)


def _dn_scan_kernel(qeff_ref, wdec_ref, ktail_ref, ubase_ref, obase_ref, glast_ref, z_ref, nw_ref,
                    o_ref, state_ref, *, heads):
    rows_per_step = qeff_ref.shape[0]
    C, DK = DN_CHUNK, DN_HEAD_DIM

    @pl.when(pl.program_id(2) == 0)
    def _():
        state_ref[...] = jnp.zeros_like(state_ref)

    nw = nw_ref[...]

    def chunk(c, carry):
        r0 = pl.multiple_of(c * C, C)
        rows = pl.ds(r0, C)
        g_last_all = glast_ref[c]
        for h in range(heads):
            cols = slice(h * DK, (h + 1) * DK)
            st = state_ref[h]
            lhs = jnp.concatenate([qeff_ref[rows, cols], wdec_ref[rows, cols]], axis=0)
            r = _dot(lhs, st.astype(BF))
            o = obase_ref[rows, cols] + r[:C]
            u = ubase_ref[rows, cols] - r[C:]
            g_last = g_last_all[:, cols]
            upd =lax.dot_general(ktail_ref[rows, cols], u.astype(BF), (((0,), (0,)), ((), ())),
                                  preferred_element_type=F32)
            state_ref[h] = st * g_last + upd
            zf = z_ref[rows, cols].astype(F32)
            on = o * lax.rsqrt(jnp.mean(o * o, axis=-1, keepdims=True) + EPS) * nw * _silu(zf)
            o_ref[rows, cols] = on.astype(o_ref.dtype)
        return carry

    lax.fori_loop(0, rows_per_step // C, chunk, 0)


def _dn_scan(qeff, wdec, ktail, ubase, obase, glast, p, dn_out_norm, seq, n_heads, heads_per_group,
             z_off, ts_pref=512):
    M, W = qeff.shape
    G = heads_per_group
    GW = G * DN_HEAD_DIM
    ts = _tile(seq, ts_pref)
    per = seq // ts
    assert z_off % GW == 0
    zb = z_off // GW
    blk = pl.BlockSpec((ts, GW), lambda b, g, t: (b * per + t, g))
    return pl.pallas_call(
        functools.partial(_dn_scan_kernel, heads=G),
        grid=(M // seq, n_heads // G, per),
        in_specs=[blk] * 5
        + [pl.BlockSpec((ts // DN_CHUNK, 1, GW), lambda b, g, t: (b * per + t, 0, g)),
           pl.BlockSpec((ts, GW), lambda b, g, t: (b * per + t, zb + g)),
           pl.BlockSpec((1, DN_HEAD_DIM), lambda b, g, t: (0, 0))],
        out_specs=blk,
        out_shape=jax.ShapeDtypeStruct((M, W), BF),
        scratch_shapes=[pltpu.VMEM((G, DN_HEAD_DIM, DN_HEAD_DIM), F32)],
        compiler_params=_cparams("parallel", "parallel", "arbitrary"),
        name="dn_scan",
    )(qeff, wdec, ktail, ubase, obase, glast, p, dn_out_norm)


def _merge_kernel(ya_ref, yb_ref, wa_ref, wb_ref, ga_ref, gb_ref, o_ref):
    ya = _dot(ya_ref[...], wa_ref[...])
    yb = _dot(yb_ref[...], wb_ref[...])
    m = jax.nn.sigmoid(ga_ref[...].astype(F32)) * ya + jax.nn.sigmoid(gb_ref[...].astype(F32)) * yb
    o_ref[...] = m.astype(o_ref.dtype)


def _merge(ya, yb, wa, wb, p, gate_off, tm_pref=1024, tn_pref=512):
    M, KA = ya.shape
    KB = yb.shape[1]
    N = wa.shape[1]
    tm, tn = _tile(M, tm_pref), _tile(N, tn_pref)
    while gate_off % tn:
        tn //= 2
    nb = N // tn
    g0 = gate_off // tn
    return pl.pallas_call(
        _merge_kernel,
        grid=(M // tm, nb),
        in_specs=[pl.BlockSpec((tm, KA), lambda i, j: (i, 0)),
                  pl.BlockSpec((tm, KB), lambda i, j: (i, 0)),
                  pl.BlockSpec((KA, tn), lambda i, j: (0, j)),
                  pl.BlockSpec((KB, tn), lambda i, j: (0, j)),
                  pl.BlockSpec((tm, tn), lambda i, j: (i, g0 + j)),
                  pl.BlockSpec((tm, tn), lambda i, j: (i, g0 + nb + j))],
        out_specs=pl.BlockSpec((tm, tn), lambda i, j: (i, j)),
        out_shape=jax.ShapeDtypeStruct((M, N), BF),
        compiler_params=_cparams("parallel", "arbitrary"),
        name="merge",
    )(ya, yb, wa, wb, p, p)


def _mem_kv_kernel(m_ref, nw_ref, wk_ref, wv_ref, k_ref, v_ref):
    mn = _rms(m_ref[...], nw_ref[...]).astype(BF)
    k_ref[...] = _dot(mn, wk_ref[...]).astype(k_ref.dtype)
    v_ref[...] = _dot(mn, wv_ref[...]).astype(v_ref.dtype)


def _mem_kv(mem, nw, wk, wv, tr_pref=256):
    Mm, D = mem.shape
    N = wk.shape[1]
    tr = _tile(Mm, tr_pref)
    full = lambda shape: pl.BlockSpec(shape, lambda i: (0, 0))
    return pl.pallas_call(
        _mem_kv_kernel,
        grid=(Mm // tr,),
        in_specs=[pl.BlockSpec((tr, D), lambda i: (i, 0)), full((1, D)), full((D, N)), full((D, N))],
        out_specs=[pl.BlockSpec((tr, N), lambda i: (i, 0))] * 2,
        out_shape=[jax.ShapeDtypeStruct((Mm, N), BF)] * 2,
        compiler_params=_cparams("parallel"),
        name="mem_kv",
    )(mem, nw, wk, wv)


def _xattn_kernel(h_ref, nw_ref, wq_ref, k_ref, v_ref, wo_ref, nw2_ref, o_ref, on_ref, *, n_heads):
    DH = XATTN_HEAD_DIM
    h = h_ref[...]
    q = _dot(_rms(h, nw_ref[...]).astype(BF), wq_ref[...])
    outs = []
    for hd in range(n_heads):
        cols = slice(hd * DH, (hd + 1) * DH)
        s = _dot_nt(q[:, cols].astype(BF), k_ref[:, cols]) * (DH ** -0.5)
        s = s - jnp.max(s, axis=-1, keepdims=True)
        e = jnp.exp(s)
        pr = e / jnp.sum(e, axis=-1, keepdims=True)
        outs.append(_dot(pr.astype(BF), v_ref[:, cols]))
    o = jnp.concatenate(outs, axis=1).astype(BF)
    h_out = h + _dot(o, wo_ref[...])
    o_ref[...] = h_out
    on_ref[...] = _rms(h_out, nw2_ref[...]).astype(on_ref.dtype)


def _xattn(h, nw, wq, kh, vh, wo, nw2, seq, n_mem, tm_pref=256):
    M, D = h.shape
    N = wq.shape[1]
    tm = _tile(seq, tm_pref)
    per = seq // tm
    full = lambda shape: pl.BlockSpec(shape, lambda i: (0, 0))
    kv = pl.BlockSpec((n_mem, N), lambda i: (i // per, 0))
    row = pl.BlockSpec((tm, D), lambda i: (i, 0))
    return pl.pallas_call(
        functools.partial(_xattn_kernel, n_heads=N // XATTN_HEAD_DIM),
        grid=(M // tm,),
        in_specs=[row, full((1, D)), full((D, N)), kv, kv, full((N, D)), full((1, D))],
        out_specs=[row, row],
        out_shape=[jax.ShapeDtypeStruct((M, D), F32), jax.ShapeDtypeStruct((M, D), BF)],
        compiler_params=_cparams("parallel"),
        name="xattn",
    )(h, nw, wq, kh, vh, wo, nw2)


def kernel(x, mem, ffn1_norm, ffn1_w_gate, ffn1_w_up, ffn1_w_down, mix_norm, w_in, conv_w, qkv_conv_w, a_log, dt_bias, dn_out_norm, w_out_conv, w_out_delta, w_o, xattn_norm, mem_norm, xattn_wq, xattn_wk, xattn_wv, xattn_wo, ffn2_norm, ffn2_w_gate, ffn2_w_up, ffn2_w_down, final_norm):
    B, S, D = x.shape
    depth = ffn1_norm.shape[0]
    n_mem = mem.shape[1]
    CW = conv_w.shape[-1]
    H = a_log.shape[-1]
    DW = H * DN_HEAD_DIM
    G = min(H, 16)
    M = B * S
    assert S % DN_CHUNK == 0 and 2 * H <= 128 and H % G == 0
    o_qkv = 3 * CW
    o_z = o_qkv + 3 * DW
    o_b = o_z + DW
    o_ga = o_b + 2 * H
    assert w_in.shape[-1] == o_ga + 2 * D
    vec = lambda v: v.reshape(1, -1).astype(F32)
    bf = lambda w: w.astype(BF)

    h = x.reshape(M, D)
    mem2 = mem.reshape(B * n_mem, D)
    for l in range(depth):
        h = _ffn(_norm(h, vec(ffn1_norm[l]))[0], h, ffn1_w_gate, ffn1_w_up, ffn1_w_down,
                 vec(mix_norm[l]), l, norm_out=False, name="ffn1")

        w_in_t = jnp.swapaxes(w_in, 1, 2)
        w_ba = bf(jnp.pad(w_in_t[l, o_b:o_ga], ((0, 128 - 2 * H), (0, 0))))
        un, ba = _norm(h, vec(mix_norm[l]), w_ba)
        p = _matmul_wst(un, w_in_t, l, o_b, o_ga - o_b, o_b + 2 * D, BF, name="in_proj")

        ya = _conv_a(p, conv_w[l].astype(F32), S, CW)
        qeff, wdec, ktail, ubase, obase, glast = _dn_prep(
            p, qkv_conv_w[l].astype(F32), ba, a_log[l], dt_bias[l], S, H, G, o_qkv)
        yb = _dn_scan(qeff, wdec, ktail, ubase, obase, glast, p, vec(dn_out_norm[l]), S, H, G, o_z)

        merged = _merge(ya, yb, bf(w_out_conv[l]), bf(w_out_delta[l]), p, o_b)
        h = _matmul_ws(merged, w_o, l, h, name="oproj")

        kh, vh = _mem_kv(mem2, vec(mem_norm[l]), bf(xattn_wk[l]), bf(xattn_wv[l]))
        h, xn = _xattn(h, vec(xattn_norm[l]), bf(xattn_wq[l]), kh, vh, bf(xattn_wo[l]),
                       vec(ffn2_norm[l]), S, n_mem)

        h = _ffn(xn, h, ffn2_w_gate, ffn2_w_up, ffn2_w_down, vec(final_norm), l,
                 norm_out=(l == depth - 1), name="ffn2")
    return h.reshape(B, S, D)
```

```python
import functools

import jax
import jax.numpy as jnp
from jax import lax
from jax.experimental import pallas as pl
from jax.experimental.pallas import tpu as pltpu

EPS = 1e-6
BF = jnp.bfloat16
F32 = jnp.float32
HI = lax.Precision.HIGHEST

DN_HEAD_DIM = 128
DN_CHUNK = 64
XATTN_HEAD_DIM = 128
HALO = 16
SUBLANES = 8
VMEM_BYTES_V7X = 64 * 1024 * 1024
VMEM_LIMIT = VMEM_BYTES_V7X - 4 * 1024 * 1024


def _cparams(*sem):
    return pltpu.CompilerParams(dimension_semantics=sem, vmem_limit_bytes=VMEM_LIMIT)


def _tile(dim, pref):
    t = min(pref, dim)
    while dim % t:
        t //= 2
    return t


def _rms(x, w):
    return x * lax.rsqrt(jnp.mean(x * x, axis=-1, keepdims=True) + EPS) * w


def _silu(x):
    return x * jax.nn.sigmoid(x)


def _dot(a, b):
    return jnp.dot(a, b, preferred_element_type=F32)


def _dot_nt(a, b, precision=None):
    return lax.dot_general(a, b, (((1,), (1,)), ((), ())), preferred_element_type=F32,
                           precision=precision)


def _ffn_kernel(xn_ref, x_hbm, wg_ref, wu_ref, wd_ref, ow_ref, o_hbm, acc_ref, sem_in, sem_out,
                *, n_i, n_f, tm, col_chunk, row_chunk, norm_out):
    i = pl.program_id(0)
    f = pl.program_id(1)
    D = acc_ref.shape[1]
    n_cc, n_rc = D // col_chunk, tm // row_chunk
    n_pieces = n_rc if norm_out else n_cc

    def rows(t):
        return pl.ds(pl.multiple_of(t * tm, tm), tm)

    def load(t):
        return pltpu.make_async_copy(x_hbm.at[rows(t)], acc_ref, sem_in)

    def store(t, k):
        if norm_out:
            src = acc_ref.at[pl.ds(k * row_chunk, row_chunk)]
            dst = o_hbm.at[pl.ds(pl.multiple_of(t * tm + k * row_chunk, row_chunk), row_chunk)]
        else:
            src = acc_ref.at[:, pl.ds(k * col_chunk, col_chunk)]
            dst = o_hbm.at[rows(t), pl.ds(k * col_chunk, col_chunk)]
        return pltpu.make_async_copy(src, dst, sem_out.at[k])

    @pl.when(f == 0)
    def _():
        @pl.when(i > 0)
        def _():
            for k in range(n_pieces):
                store(i - 1, k).wait()

        load(i).start()

    xn = xn_ref[...]
    g = _dot(xn, wg_ref[...].astype(BF))
    u = _dot(xn, wu_ref[...].astype(BF))
    a = (0.5 * _silu(g) * u).astype(BF)

    @pl.when(f == 0)
    def _():
        load(i).wait()

    def down(last):
        for c in range(n_cc):
            cs = slice(c * col_chunk, (c + 1) * col_chunk)
            acc_ref[:, cs] += _dot(a, wd_ref[:, cs].astype(BF))
            if last and not norm_out:
                store(i, c).start()
        if last and norm_out:
            for r in range(n_rc):
                rs = slice(r * row_chunk, (r + 1) * row_chunk)
                acc_ref[rs, :] = _rms(acc_ref[rs, :], ow_ref[...])
                store(i, r).start()
        if last:
            @pl.when(i == n_i - 1)
            def _():
                for k in range(n_pieces):
                    store(i, k).wait()

    @pl.when(f < n_f - 1)
    def _():
        down(False)

    @pl.when(f == n_f - 1)
    def _():
        down(True)


def _ffn(xn, x, wg, wu, wd, ow, layer, *, norm_out, tm_pref=1024, tf_pref=256, name="ffn"):
    M, D = x.shape
    F = wg.shape[-1]
    tm, tf = _tile(M, tm_pref), _tile(F, tf_pref)
    n_i, n_f = M // tm, F // tf
    col_chunk, row_chunk = _tile(D, 1024), _tile(tm, 256)
    n_pieces = tm // row_chunk if norm_out else D // col_chunk
    return pl.pallas_call(
        functools.partial(_ffn_kernel, n_i=n_i, n_f=n_f, tm=tm, col_chunk=col_chunk,
                          row_chunk=row_chunk, norm_out=norm_out),
        grid=(n_i, n_f),
        in_specs=[
            pl.BlockSpec((tm, D), lambda i, f: (i, 0), pipeline_mode=pl.Buffered(1)),
            pl.BlockSpec(memory_space=pl.ANY),
            pl.BlockSpec((None, D, tf), lambda i, f: (layer, 0, f)),
            pl.BlockSpec((None, D, tf), lambda i, f: (layer, 0, f)),
            pl.BlockSpec((None, tf, D), lambda i, f: (layer, f, 0)),
            pl.BlockSpec((1, D), lambda i, f: (0, 0)),
        ],
        out_specs=pl.BlockSpec(memory_space=pl.ANY),
        out_shape=jax.ShapeDtypeStruct((M, D), F32),
        scratch_shapes=[pltpu.VMEM((tm, D), F32), pltpu.SemaphoreType.DMA(()),
                        pltpu.SemaphoreType.DMA((n_pieces,))],
        compiler_params=_cparams("arbitrary", "arbitrary"),
        name=name,
    )(xn, x, wg, wu, wd, ow)


def _norm_kernel(x_ref, w_ref, *refs, with_proj):
    xn = _rms(x_ref[...], w_ref[...]).astype(BF)
    if with_proj:
        wp_ref, o_ref, p_ref = refs
        p_ref[...] = _dot_nt(xn, wp_ref[...])
    else:
        (o_ref,) = refs
    o_ref[...] = xn


def _norm(x, w, w_proj=None, tr_pref=512):
    M, D = x.shape
    tr = _tile(M, tr_pref)
    in_specs = [pl.BlockSpec((tr, D), lambda i: (i, 0)), pl.BlockSpec((1, D), lambda i: (0, 0))]
    out_specs = [pl.BlockSpec((tr, D), lambda i: (i, 0))]
    out_shape = [jax.ShapeDtypeStruct((M, D), BF)]
    args = [x, w]
    if w_proj is not None:
        N = w_proj.shape[0]
        in_specs.append(pl.BlockSpec((N, D), lambda i: (0, 0)))
        out_specs.append(pl.BlockSpec((tr, N), lambda i: (i, 0)))
        out_shape.append(jax.ShapeDtypeStruct((M, N), F32))
        args.append(w_proj)
    return pl.pallas_call(
        functools.partial(_norm_kernel, with_proj=w_proj is not None),
        grid=(M // tr,),
        in_specs=in_specs,
        out_specs=out_specs,
        out_shape=out_shape,
        compiler_params=_cparams("parallel"),
        name="norm_proj" if w_proj is not None else "norm",
    )(*args)


def _mm_kernel(a_ref, w_ref, *refs, residual):
    if residual:
        res_ref, o_ref = refs
        o_ref[...] = res_ref[...] + _dot(a_ref[...], w_ref[...])
    else:
        (o_ref,) = refs
        o_ref[...] = _dot(a_ref[...], w_ref[...]).astype(o_ref.dtype)


def _matmul(a, w, out_dtype, *, res=None, tm_pref=1024, tn_pref=512, name="matmul"):
    M, K = a.shape
    N = w.shape[1]
    tm, tn = _tile(M, tm_pref), _tile(N, tn_pref)
    out_blk = pl.BlockSpec((tm, tn), lambda i, j: (i, j))
    in_specs = [pl.BlockSpec((tm, K), lambda i, j: (i, 0)),
                pl.BlockSpec((K, tn), lambda i, j: (0, j))]
    args = [a, w]
    if res is not None:
        in_specs.append(out_blk)
        args.append(res)
    return pl.pallas_call(
        functools.partial(_mm_kernel, residual=res is not None),
        grid=(M // tm, N // tn),
        in_specs=in_specs,
        out_specs=out_blk,
        out_shape=jax.ShapeDtypeStruct((M, N), out_dtype),
        compiler_params=_cparams("parallel", "arbitrary"),
        name=name,
    )(*args)


def _mm_ws_kernel(a_ref, w_ref, res_ref, o_ref, wbf_ref):
    @pl.when(pl.program_id(1) == 0)
    def _():
        wbf_ref[...] = w_ref[...].astype(BF)

    o_ref[...] = res_ref[...] + _dot(a_ref[...], wbf_ref[...])


def _matmul_ws(a, w, layer, res, *, tm_pref=1024, tn_pref=512, name="matmul_ws"):
    M, K = a.shape
    N = w.shape[-1]
    tm, tn = _tile(M, tm_pref), _tile(N, tn_pref)
    out_blk = pl.BlockSpec((tm, tn), lambda j, i: (i, j))
    return pl.pallas_call(
        _mm_ws_kernel,
        grid=(N // tn, M // tm),
        in_specs=[pl.BlockSpec((tm, K), lambda j, i: (i, 0)),
                  pl.BlockSpec((None, K, tn), lambda j, i: (layer, 0, j)),
                  out_blk],
        out_specs=out_blk,
        out_shape=jax.ShapeDtypeStruct((M, N), res.dtype),
        scratch_shapes=[pltpu.VMEM((K, tn), BF)],
        compiler_params=_cparams("parallel", "arbitrary"),
        name=name,
    )(a, w, res)


def _mm_wst_kernel(a_ref, wt_hbm, o_ref, wf32_ref, wbf_ref, sem, *, layer, tn, n_j, n_head, skip):
    j = pl.program_id(0)

    def fetch(t):
        start = pl.multiple_of(t * tn + jnp.where(t >= n_head, skip, 0), SUBLANES)
        return pltpu.make_async_copy(wt_hbm.at[layer, pl.ds(start, tn)], wf32_ref.at[t % 2],
                                     sem.at[t % 2])

    @pl.when(pl.program_id(1) == 0)
    def _():
        @pl.when(j == 0)
        def _():
            fetch(j).start()

        fetch(j).wait()
        wbf_ref[...] = wf32_ref[j % 2].astype(BF)

        @pl.when(j + 1 < n_j)
        def _():
            fetch(j + 1).start()

    o_ref[...] = _dot_nt(a_ref[...], wbf_ref[...]).astype(o_ref.dtype)


def _matmul_wst(a, wt, layer, n_head, skip, n_out, out_dtype, *, tm_pref=1024, tn_pref=512,
                name="matmul_wst"):
    M, K = a.shape
    tm, tn = _tile(M, tm_pref), _tile(n_out, tn_pref)
    assert n_head % tn == 0 and skip % SUBLANES == 0
    n_j = n_out // tn
    return pl.pallas_call(
        functools.partial(_mm_wst_kernel, layer=layer, tn=tn, n_j=n_j, n_head=n_head // tn, skip=skip),
        grid=(n_j, M // tm),
        in_specs=[pl.BlockSpec((tm, K), lambda j, i: (i, 0)),
                  pl.BlockSpec(memory_space=pl.ANY)],
        out_specs=pl.BlockSpec((tm, tn), lambda j, i: (i, j)),
        out_shape=jax.ShapeDtypeStruct((M, n_out), out_dtype),
        scratch_shapes=[pltpu.VMEM((2, tn, K), F32), pltpu.VMEM((tn, K), BF),
                        pltpu.SemaphoreType.DMA((2,))],
        compiler_params=_cparams("arbitrary", "arbitrary"),
        name=name,
    )(a, wt)


def _split3(x):
    hi = x.astype(BF)
    r1 = x - hi.astype(F32)
    mid = r1.astype(BF)
    lo = (r1 - mid.astype(F32)).astype(BF)
    return hi, mid, lo


def _gate_terms(ba, eb, ea, sel, alog, dtb):
    R = ba.shape[0]
    beta = jax.nn.sigmoid(ba)
    x = ba + dtb
    softplus = jnp.maximum(x, 0.0) + jnp.log1p(jnp.exp(-jnp.abs(x)))
    g = -jnp.exp(alog) * softplus
    ri = lax.broadcasted_iota(jnp.int32, (R, R), 0)
    ci = lax.broadcasted_iota(jnp.int32, (R, R), 1)
    tri = ((ri // DN_CHUNK == ci // DN_CHUNK) & (ri >= ci)).astype(BF)
    gc = sum(_dot(tri, piece) for piece in _split3(g))
    gc_parts = _split3(gc)
    beta_b = sum(_dot(piece, eb) for piece in _split3(beta))
    gc_b = sum(_dot(piece, ea) for piece in gc_parts)
    gc_t = sum(_dot_nt(sel, piece) for piece in gc_parts)
    return beta_b, gc_b, gc_t


def _group_rows(heads_per_group):
    return -(-heads_per_group // SUBLANES) * SUBLANES


def _gate_constants(a_log, dt_bias, n_heads, heads_per_group):
    G, H = heads_per_group, n_heads
    W = H * DN_HEAD_DIM
    lane_head = jnp.arange(W) // DN_HEAD_DIM
    rows = jnp.arange(128)[:, None]
    eb = (rows == lane_head[None, :]).astype(BF)
    ea = (rows == lane_head[None, :] + H).astype(BF)
    rpg = _group_rows(G)
    r = jnp.arange((H // G) * rpg)
    head_of_row = jnp.where(r % rpg < G, (r // rpg) * G + r % rpg, -128)
    sel = (jnp.arange(128)[None, :] == head_of_row[:, None] + H).astype(BF)
    at_a = lambda v: jnp.pad(v.astype(F32), (H, 128 - 2 * H))[None, :]
    return eb, ea, sel, at_a(a_log), at_a(dt_bias)


def _shift_rows(t, halo, k):
    r = pltpu.roll(t, k, 0)
    hr = pltpu.roll(halo, k, 0)[:8]
    rows = lax.broadcasted_iota(jnp.int32, (8, t.shape[1]), 0)
    first = jnp.where(rows < k, hr, r[:8])
    return jnp.concatenate([first, r[8:]], axis=0)


def _causal_conv(t, halo, w):
    K = w.shape[0]
    y = t * w[K - 1:K]
    for d in range(1, K):
        y = y + _shift_rows(t, halo, d) * w[K - 1 - d:K - d]
    return y


def _halo_spec(rows_per_tile, width, col_block, n_lead):
    per = rows_per_tile // HALO
    if n_lead == 1:
        return pl.BlockSpec((HALO, width), lambda i: (jnp.maximum(i * per - 1, 0), col_block))
    return pl.BlockSpec((HALO, width), lambda i, g: (jnp.maximum(i * per - 1, 0), col_block(g)))


def _conv_a_kernel(cx_ref, cc_ref, cb_ref, hx_ref, hc_ref, w_ref, o_ref, *, tiles_per_seq):
    first = (pl.program_id(0) % tiles_per_seq == 0)
    keep = jnp.where(first, 0.0, 1.0)
    t = cx_ref[...].astype(F32) * cc_ref[...].astype(F32)
    halo = hx_ref[...].astype(F32) * hc_ref[...].astype(F32) * keep
    y = _causal_conv(t, halo, w_ref[...])
    o_ref[...] = (cb_ref[...].astype(F32) * y).astype(o_ref.dtype)


def _conv_a(p, conv_w, seq, width, tr_pref=256):
    M = p.shape[0]
    tr = _tile(seq, tr_pref)
    blk = lambda c: pl.BlockSpec((tr, width), lambda i: (i, c))
    return pl.pallas_call(
        functools.partial(_conv_a_kernel, tiles_per_seq=seq // tr),
        grid=(M // tr,),
        in_specs=[blk(0), blk(1), blk(2), _halo_spec(tr, width, 0, 1), _halo_spec(tr, width, 1, 1),
                  pl.BlockSpec(conv_w.shape, lambda i: (0, 0))],
        out_specs=pl.BlockSpec((tr, width), lambda i: (i, 0)),
        out_shape=jax.ShapeDtypeStruct((M, width), BF),
        compiler_params=_cparams("parallel"),
        name="conv_a",
    )(p, p, p, p, p, conv_w)


def _dn_prep_kernel(q_ref, k_ref, v_ref, hq_ref, hk_ref, hv_ref, wq_ref, wk_ref, wv_ref,
                    ba_ref, eb_ref, ea_ref, sel_ref, alog_ref, dtb_ref,
                    qeff_ref, wdec_ref, ktail_ref, ubase_ref, obase_ref, glast_ref,
                    *, tiles_per_seq, heads, wave):
    R = q_ref.shape[0]
    C, DK = DN_CHUNK, DN_HEAD_DIM
    first = (pl.program_id(0) % tiles_per_seq == 0)
    keep = jnp.where(first, 0.0, 1.0)
    beta_b, gc_b, gc_t = _gate_terms(ba_ref[...], eb_ref[...], ea_ref[...], sel_ref[...],
                                     alog_ref[...], dtb_ref[...])
    for c in range(R // C):
        glast_ref[c] = jnp.exp(gc_b[c * C + C - 1:c * C + C, :])
    ri = lax.broadcasted_iota(jnp.int32, (R, R), 0)
    ci = lax.broadcasted_iota(jnp.int32, (R, R), 1)
    same = (ri // C) == (ci // C)
    incl = same & (ri >= ci)
    strict = same & (ri > ci)

    def l2n(x):
        return x * lax.rsqrt(jnp.sum(x * x, axis=-1, keepdims=True) + EPS)

    def head_inputs(hh):
        cols = slice(hh * DK, (hh + 1) * DK)

        def conv_silu(t_ref, h_ref, w_ref):
            y = _causal_conv(t_ref[:, cols].astype(F32), h_ref[:, cols].astype(F32) * keep,
                             w_ref[:, cols])
            return _silu(y)

        q = l2n(conv_silu(q_ref, hq_ref, wq_ref)) * (DK ** -0.5)
        k = l2n(conv_silu(k_ref, hk_ref, wk_ref))
        v = conv_silu(v_ref, hv_ref, wv_ref)
        beta = beta_b[:, cols]
        gc = gc_b[:, cols]
        g_row = jnp.broadcast_to(gc_t[hh:hh + 1, :], (R, R))
        g_col = jnp.concatenate([gc] * (R // DK), axis=1)
        decay = jnp.where(incl, jnp.exp(jnp.where(incl, g_col - g_row, 0.0)), 0.0)
        kb = k * beta
        k16 = k.astype(BF)
        lmat = jnp.where(strict, _dot_nt(kb.astype(BF), k16) * decay, 0.0)
        aqk = jnp.where(incl, _dot_nt(q.astype(BF), k16) * decay, 0.0).astype(BF)
        rhs = jnp.concatenate([v * beta, kb * jnp.exp(gc)], axis=1)
        return dict(cols=cols, q=q, k=k, gc=gc, aqk=aqk, rhs=rhs, m=-lmat)

    n_fac = C.bit_length() - 1
    waves = [list(range(w0, min(w0 + wave, heads))) for w0 in range(0, heads, wave)]
    hs = [head_inputs(hh) for hh in waves[0]]
    for wi in range(len(waves)):
        todo = list(waves[wi + 1]) if wi + 1 < len(waves) else []
        hs_next = []
        for j in range(n_fac):
            for d in hs:
                m16 = d["m"].astype(BF)
                if j == 0:
                    d["qm"] = d["m"]
                    d["m"] = _dot(m16, m16)
                elif j < n_fac - 1:
                    r = _dot(jnp.concatenate([m16, d["qm"].astype(BF)], axis=0), m16)
                    d["qm"] = d["qm"] + d["m"] + r[R:]
                    d["m"] = r[:R]
                else:
                    d["qm"] = d["qm"] + d["m"] + _dot(d["qm"].astype(BF), m16)
            n_emit = -(-len(todo) // (n_fac - j))
            hs_next += [head_inputs(hh) for hh in todo[:n_emit]]
            todo = todo[n_emit:]
        for d in hs:
            d["sol"] = d["rhs"] + _dot(d["qm"].astype(BF), d["rhs"].astype(BF))
        for d in hs:
            cols, sol, gc, q, k = d["cols"], d["sol"], d["gc"], d["q"], d["k"]
            ubase, wdec = sol[:, :DK], sol[:, DK:]
            x = _dot(d["aqk"], jnp.concatenate([wdec, ubase], axis=1).astype(BF))
            gc_last = jnp.concatenate(
                [jnp.broadcast_to(gc[c * C + C - 1:c * C + C, :], (C, DK)) for c in range(R // C)],
                axis=0)
            qeff_ref[:, cols] = (q * jnp.exp(gc) - x[:, :DK]).astype(qeff_ref.dtype)
            obase_ref[:, cols] = x[:, DK:]
            ubase_ref[:, cols] = ubase
            wdec_ref[:, cols] = wdec.astype(wdec_ref.dtype)
            ktail_ref[:, cols] = (k * jnp.exp(gc_last - gc)).astype(ktail_ref.dtype)
        hs = hs_next


def _dn_prep(p, qkv_conv_w, ba, a_log, dt_bias, seq, n_heads, heads_per_group, q_off):
    M = p.shape[0]
    G = heads_per_group
    GW = G * DN_HEAD_DIM
    W = n_heads * DN_HEAD_DIM
    n_groups = n_heads // G
    R = _tile(seq, 256)
    rpg = _group_rows(G)
    assert q_off % GW == 0 and R % DN_HEAD_DIM == 0
    eb, ea, sel, alog_c, dtb_c = _gate_constants(a_log, dt_bias, n_heads, G)
    cols = [lambda g, o=o: q_off // GW + o * n_groups + g for o in range(3)]
    blk = lambda c: pl.BlockSpec((R, GW), lambda i, g: (i, c(g)))
    wblk = lambda o: pl.BlockSpec((qkv_conv_w.shape[0], GW), lambda i, g: (0, o * n_groups + g))
    head = pl.BlockSpec((R, GW), lambda i, g: (i, g))
    lanes = pl.BlockSpec((1, 128), lambda i, g: (0, 0))
    n_ch = R // DN_CHUNK
    return pl.pallas_call(
        functools.partial(_dn_prep_kernel, tiles_per_seq=seq // R, heads=G, wave=min(G, 4)),
        grid=(M // R, n_groups),
        in_specs=[blk(cols[0]), blk(cols[1]), blk(cols[2]),
                  _halo_spec(R, GW, cols[0], 2), _halo_spec(R, GW, cols[1], 2),
                  _halo_spec(R, GW, cols[2], 2),
                  wblk(0), wblk(1), wblk(2),
                  pl.BlockSpec((R, 128), lambda i, g: (i, 0)),
                  pl.BlockSpec((128, GW), lambda i, g: (0, g)),
                  pl.BlockSpec((128, GW), lambda i, g: (0, g)),
                  pl.BlockSpec((rpg, 128), lambda i, g: (g, 0)), lanes, lanes],
        out_specs=[head] * 5 + [pl.BlockSpec((n_ch, 1, GW), lambda i, g: (i, 0, g))],
        out_shape=[jax.ShapeDtypeStruct((M, W), BF)] * 3 + [jax.ShapeDtypeStruct((M, W), F32)] * 2
        + [jax.ShapeDtypeStruct((M // DN_CHUNK, 1, W), F32)],
        compiler_params=_cparams("parallel", "parallel"),
        name="dn_prep",
    )(p, p, p, p, p, p, qkv_conv_w, qkv_conv_w, qkv_conv_w, ba, eb, ea, sel, alog_c, dtb_c)


def _dn_scan_kernel(qeff_ref, wdec_ref, ktail_ref, ubase_ref, obase_ref, glast_ref, z_ref, nw_ref,
                    o_ref, state_ref, *, heads):
    rows_per_step = qeff_ref.shape[0]
    C, DK = DN_CHUNK, DN_HEAD_DIM

    @pl.when(pl.program_id(2) == 0)
    def _():
        state_ref[...] = jnp.zeros_like(state_ref)

    nw = nw_ref[...]

    def chunk(c, carry):
        r0 = pl.multiple_of(c * C, C)
        rows = pl.ds(r0, C)
        g_last_all = glast_ref[c]
        for h in range(heads):
            cols = slice(h * DK, (h + 1) * DK)
            st = state_ref[h]
            lhs = jnp.concatenate([qeff_ref[rows, cols], wdec_ref[rows, cols]], axis=0)
            r = _dot(lhs, st.astype(BF))
            o = obase_ref[rows, cols] + r[:C]
            u = ubase_ref[rows, cols] - r[C:]
            g_last = g_last_all[:, cols]
            upd =lax.dot_general(ktail_ref[rows, cols], u.astype(BF), (((0,), (0,)), ((), ())),
                                  preferred_element_type=F32)
            state_ref[h] = st * g_last + upd
            zf = z_ref[rows, cols].astype(F32)
            on = o * lax.rsqrt(jnp.mean(o * o, axis=-1, keepdims=True) + EPS) * nw * _silu(zf)
            o_ref[rows, cols] = on.astype(o_ref.dtype)
        return carry

    lax.fori_loop(0, rows_per_step // C, chunk, 0)


def _dn_scan(qeff, wdec, ktail, ubase, obase, glast, p, dn_out_norm, seq, n_heads, heads_per_group,
             z_off, ts_pref=512):
    M, W = qeff.shape
    G = heads_per_group
    GW = G * DN_HEAD_DIM
    ts = _tile(seq, ts_pref)
    per = seq // ts
    assert z_off % GW == 0
    zb = z_off // GW
    blk = pl.BlockSpec((ts, GW), lambda b, g, t: (b * per + t, g))
    return pl.pallas_call(
        functools.partial(_dn_scan_kernel, heads=G),
        grid=(M // seq, n_heads // G, per),
        in_specs=[blk] * 5
        + [pl.BlockSpec((ts // DN_CHUNK, 1, GW), lambda b, g, t: (b * per + t, 0, g)),
           pl.BlockSpec((ts, GW), lambda b, g, t: (b * per + t, zb + g)),
           pl.BlockSpec((1, DN_HEAD_DIM), lambda b, g, t: (0, 0))],
        out_specs=blk,
        out_shape=jax.ShapeDtypeStruct((M, W), BF),
        scratch_shapes=[pltpu.VMEM((G, DN_HEAD_DIM, DN_HEAD_DIM), F32)],
        compiler_params=_cparams("parallel", "parallel", "arbitrary"),
        name="dn_scan",
    )(qeff, wdec, ktail, ubase, obase, glast, p, dn_out_norm)


def _merge_kernel(ya_ref, yb_ref, wa_ref, wb_ref, ga_ref, gb_ref, o_ref):
    ya = _dot(ya_ref[...], wa_ref[...])
    yb = _dot(yb_ref[...], wb_ref[...])
    m = jax.nn.sigmoid(ga_ref[...].astype(F32)) * ya + jax.nn.sigmoid(gb_ref[...].astype(F32)) * yb
    o_ref[...] = m.astype(o_ref.dtype)


def _merge(ya, yb, wa, wb, p, gate_off, tm_pref=1024, tn_pref=512):
    M, KA = ya.shape
    KB = yb.shape[1]
    N = wa.shape[1]
    tm, tn = _tile(M, tm_pref), _tile(N, tn_pref)
    while gate_off % tn:
        tn //= 2
    nb = N // tn
    g0 = gate_off // tn
    return pl.pallas_call(
        _merge_kernel,
        grid=(M // tm, nb),
        in_specs=[pl.BlockSpec((tm, KA), lambda i, j: (i, 0)),
                  pl.BlockSpec((tm, KB), lambda i, j: (i, 0)),
                  pl.BlockSpec((KA, tn), lambda i, j: (0, j)),
                  pl.BlockSpec((KB, tn), lambda i, j: (0, j)),
                  pl.BlockSpec((tm, tn), lambda i, j: (i, g0 + j)),
                  pl.BlockSpec((tm, tn), lambda i, j: (i, g0 + nb + j))],
        out_specs=pl.BlockSpec((tm, tn), lambda i, j: (i, j)),
        out_shape=jax.ShapeDtypeStruct((M, N), BF),
        compiler_params=_cparams("parallel", "arbitrary"),
        name="merge",
    )(ya, yb, wa, wb, p, p)


def _mem_kv_kernel(m_ref, nw_ref, wk_ref, wv_ref, k_ref, v_ref):
    mn = _rms(m_ref[...], nw_ref[...]).astype(BF)
    k_ref[...] = _dot(mn, wk_ref[...]).astype(k_ref.dtype)
    v_ref[...] = _dot(mn, wv_ref[...]).astype(v_ref.dtype)


def _mem_kv(mem, nw, wk, wv, tr_pref=256):
    Mm, D = mem.shape
    N = wk.shape[1]
    tr = _tile(Mm, tr_pref)
    full = lambda shape: pl.BlockSpec(shape, lambda i: (0, 0))
    return pl.pallas_call(
        _mem_kv_kernel,
        grid=(Mm // tr,),
        in_specs=[pl.BlockSpec((tr, D), lambda i: (i, 0)), full((1, D)), full((D, N)), full((D, N))],
        out_specs=[pl.BlockSpec((tr, N), lambda i: (i, 0))] * 2,
        out_shape=[jax.ShapeDtypeStruct((Mm, N), BF)] * 2,
        compiler_params=_cparams("parallel"),
        name="mem_kv",
    )(mem, nw, wk, wv)


def _xattn_kernel(h_ref, nw_ref, wq_ref, k_ref, v_ref, wo_ref, nw2_ref, o_ref, on_ref, *, n_heads):
    DH = XATTN_HEAD_DIM
    h = h_ref[...]
    q = _dot(_rms(h, nw_ref[...]).astype(BF), wq_ref[...])
    outs = []
    for hd in range(n_heads):
        cols = slice(hd * DH, (hd + 1) * DH)
        s = _dot_nt(q[:, cols].astype(BF), k_ref[:, cols]) * (DH ** -0.5)
        s = s - jnp.max(s, axis=-1, keepdims=True)
        e = jnp.exp(s)
        pr = e / jnp.sum(e, axis=-1, keepdims=True)
        outs.append(_dot(pr.astype(BF), v_ref[:, cols]))
    o = jnp.concatenate(outs, axis=1).astype(BF)
    h_out = h + _dot(o, wo_ref[...])
    o_ref[...] = h_out
    on_ref[...] = _rms(h_out, nw2_ref[...]).astype(on_ref.dtype)


def _xattn(h, nw, wq, kh, vh, wo, nw2, seq, n_mem, tm_pref=256):
    M, D = h.shape
    N = wq.shape[1]
    tm = _tile(seq, tm_pref)
    per = seq // tm
    full = lambda shape: pl.BlockSpec(shape, lambda i: (0, 0))
    kv = pl.BlockSpec((n_mem, N), lambda i: (i // per, 0))
    row = pl.BlockSpec((tm, D), lambda i: (i, 0))
    return pl.pallas_call(
        functools.partial(_xattn_kernel, n_heads=N // XATTN_HEAD_DIM),
        grid=(M // tm,),
        in_specs=[row, full((1, D)), full((D, N)), kv, kv, full((N, D)), full((1, D))],
        out_specs=[row, row],
        out_shape=[jax.ShapeDtypeStruct((M, D), F32), jax.ShapeDtypeStruct((M, D), BF)],
        compiler_params=_cparams("parallel"),
        name="xattn",
    )(h, nw, wq, kh, vh, wo, nw2)


def kernel(x, mem, ffn1_norm, ffn1_w_gate, ffn1_w_up, ffn1_w_down, mix_norm, w_in, conv_w, qkv_conv_w, a_log, dt_bias, dn_out_norm, w_out_conv, w_out_delta, w_o, xattn_norm, mem_norm, xattn_wq, xattn_wk, xattn_wv, xattn_wo, ffn2_norm, ffn2_w_gate, ffn2_w_up, ffn2_w_down, final_norm):
    B, S, D = x.shape
    depth = ffn1_norm.shape[0]
    n_mem = mem.shape[1]
    CW = conv_w.shape[-1]
    H = a_log.shape[-1]
    DW = H * DN_HEAD_DIM
    G = min(H, 16)
    M = B * S
    assert S % DN_CHUNK == 0 and 2 * H <= 128 and H % G == 0
    o_qkv = 3 * CW
    o_z = o_qkv + 3 * DW
    o_b = o_z + DW
    o_ga = o_b + 2 * H
    assert w_in.shape[-1] == o_ga + 2 * D
    vec = lambda v: v.reshape(1, -1).astype(F32)
    bf = lambda w: w.astype(BF)

    h = x.reshape(M, D)
    mem2 = mem.reshape(B * n_mem, D)
    for l in range(depth):
        h = _ffn(_norm(h, vec(ffn1_norm[l]))[0], h, ffn1_w_gate, ffn1_w_up, ffn1_w_down,
                 vec(mix_norm[l]), l, norm_out=False, name="ffn1")

        w_in_t = jnp.swapaxes(w_in, 1, 2)
        w_ba = bf(jnp.pad(w_in_t[l, o_b:o_ga], ((0, 128 - 2 * H), (0, 0))))
        un, ba = _norm(h, vec(mix_norm[l]), w_ba)
        p = _matmul_wst(un, w_in_t, l, o_b, o_ga - o_b, o_b + 2 * D, BF, name="in_proj")

        ya = _conv_a(p, conv_w[l].astype(F32), S, CW)
        qeff, wdec, ktail, ubase, obase, glast = _dn_prep(
            p, qkv_conv_w[l].astype(F32), ba, a_log[l], dt_bias[l], S, H, G, o_qkv)
        yb = _dn_scan(qeff, wdec, ktail, ubase, obase, glast, p, vec(dn_out_norm[l]), S, H, G, o_z)

        merged = _merge(ya, yb, bf(w_out_conv[l]), bf(w_out_delta[l]), p, o_b)
        h = _matmul_ws(merged, w_o, l, h, name="oproj")

        kh, vh = _mem_kv(mem2, vec(mem_norm[l]), bf(xattn_wk[l]), bf(xattn_wv[l]))
        h, xn = _xattn(h, vec(xattn_norm[l]), bf(xattn_wq[l]), kh, vh, bf(xattn_wo[l]),
                       vec(ffn2_norm[l]), S, n_mem)

        h = _ffn(xn, h, ffn2_w_gate, ffn2_w_up, ffn2_w_down, vec(final_norm), l,
                 norm_out=(l == depth - 1), name="ffn2")
    return h.reshape(B, S, D)
```

```python
import functools

import jax
import jax.numpy as jnp
from jax import lax
from jax.experimental import pallas as pl
from jax.experimental.pallas import tpu as pltpu

EPS = 1e-6
BF = jnp.bfloat16
F32 = jnp.float32
HI = lax.Precision.HIGHEST

DN_HEAD_DIM = 128
DN_CHUNK = 64
XATTN_HEAD_DIM = 128
HALO = 16
SUBLANES = 8
VMEM_BYTES_V7X = 64 * 1024 * 1024
VMEM_LIMIT = VMEM_BYTES_V7X - 4 * 1024 * 1024


def _cparams(*sem):
    return pltpu.CompilerParams(dimension_semantics=sem, vmem_limit_bytes=VMEM_LIMIT)


def _tile(dim, pref):
    t = min(pref, dim)
    while dim % t:
        t //= 2
    return t


def _rms(x, w):
    return x * lax.rsqrt(jnp.mean(x * x, axis=-1, keepdims=True) + EPS) * w


def _silu(x):
    return x * jax.nn.sigmoid(x)


def _dot(a, b):
    return jnp.dot(a, b, preferred_element_type=F32)


def _dot_nt(a, b, precision=None):
    return lax.dot_general(a, b, (((1,), (1,)), ((), ())), preferred_element_type=F32,
                           precision=precision)


def _ffn_kernel(xn_ref, x_hbm, wg_ref, wu_ref, wd_ref, ow_ref, o_hbm, acc_ref, sem_in, sem_out,
                *, n_i, n_f, tm, col_chunk, row_chunk, norm_out):
    i = pl.program_id(0)
    f = pl.program_id(1)
    D = acc_ref.shape[1]
    n_cc, n_rc = D // col_chunk, tm // row_chunk
    n_pieces = n_rc if norm_out else n_cc

    def rows(t):
        return pl.ds(pl.multiple_of(t * tm, tm), tm)

    def load(t):
        return pltpu.make_async_copy(x_hbm.at[rows(t)], acc_ref, sem_in)

    def store(t, k):
        if norm_out:
            src = acc_ref.at[pl.ds(k * row_chunk, row_chunk)]
            dst = o_hbm.at[pl.ds(pl.multiple_of(t * tm + k * row_chunk, row_chunk), row_chunk)]
        else:
            src = acc_ref.at[:, pl.ds(k * col_chunk, col_chunk)]
            dst = o_hbm.at[rows(t), pl.ds(k * col_chunk, col_chunk)]
        return pltpu.make_async_copy(src, dst, sem_out.at[k])

    @pl.when(f == 0)
    def _():
        @pl.when(i > 0)
        def _():
            for k in range(n_pieces):
                store(i - 1, k).wait()

        load(i).start()

    xn = xn_ref[...]
    g = _dot(xn, wg_ref[...].astype(BF))
    u = _dot(xn, wu_ref[...].astype(BF))
    a = (0.5 * _silu(g) * u).astype(BF)

    @pl.when(f == 0)
    def _():
        load(i).wait()

    def down(last):
        for c in range(n_cc):
            cs = slice(c * col_chunk, (c + 1) * col_chunk)
            acc_ref[:, cs] += _dot(a, wd_ref[:, cs].astype(BF))
            if last and not norm_out:
                store(i, c).start()
        if last and norm_out:
            for r in range(n_rc):
                rs = slice(r * row_chunk, (r + 1) * row_chunk)
                acc_ref[rs, :] = _rms(acc_ref[rs, :], ow_ref[...])
                store(i, r).start()
        if last:
            @pl.when(i == n_i - 1)
            def _():
                for k in range(n_pieces):
                    store(i, k).wait()

    @pl.when(f < n_f - 1)
    def _():
        down(False)

    @pl.when(f == n_f - 1)
    def _():
        down(True)


def _ffn(xn, x, wg, wu, wd, ow, layer, *, norm_out, tm_pref=1024, tf_pref=256, name="ffn"):
    M, D = x.shape
    F = wg.shape[-1]
    tm, tf = _tile(M, tm_pref), _tile(F, tf_pref)
    n_i, n_f = M // tm, F // tf
    col_chunk, row_chunk = _tile(D, 1024), _tile(tm, 256)
    n_pieces = tm // row_chunk if norm_out else D // col_chunk
    return pl.pallas_call(
        functools.partial(_ffn_kernel, n_i=n_i, n_f=n_f, tm=tm, col_chunk=col_chunk,
                          row_chunk=row_chunk, norm_out=norm_out),
        grid=(n_i, n_f),
        in_specs=[
            pl.BlockSpec((tm, D), lambda i, f: (i, 0), pipeline_mode=pl.Buffered(1)),
            pl.BlockSpec(memory_space=pl.ANY),
            pl.BlockSpec((None, D, tf), lambda i, f: (layer, 0, f)),
            pl.BlockSpec((None, D, tf), lambda i, f: (layer, 0, f)),
            pl.BlockSpec((None, tf, D), lambda i, f: (layer, f, 0)),
            pl.BlockSpec((1, D), lambda i, f: (0, 0)),
        ],
        out_specs=pl.BlockSpec(memory_space=pl.ANY),
        out_shape=jax.ShapeDtypeStruct((M, D), F32),
        scratch_shapes=[pltpu.VMEM((tm, D), F32), pltpu.SemaphoreType.DMA(()),
                        pltpu.SemaphoreType.DMA((n_pieces,))],
        compiler_params=_cparams("arbitrary", "arbitrary"),
        name=name,
    )(xn, x, wg, wu, wd, ow)


def _norm_kernel(x_ref, w_ref, *refs, with_proj):
    xn = _rms(x_ref[...], w_ref[...]).astype(BF)
    if with_proj:
        wp_ref, o_ref, p_ref = refs
        p_ref[...] = _dot_nt(xn, wp_ref[...])
    else:
        (o_ref,) = refs
    o_ref[...] = xn


def _norm(x, w, w_proj=None, tr_pref=512):
    M, D = x.shape
    tr = _tile(M, tr_pref)
    in_specs = [pl.BlockSpec((tr, D), lambda i: (i, 0)), pl.BlockSpec((1, D), lambda i: (0, 0))]
    out_specs = [pl.BlockSpec((tr, D), lambda i: (i, 0))]
    out_shape = [jax.ShapeDtypeStruct((M, D), BF)]
    args = [x, w]
    if w_proj is not None:
        N = w_proj.shape[0]
        in_specs.append(pl.BlockSpec((N, D), lambda i: (0, 0)))
        out_specs.append(pl.BlockSpec((tr, N), lambda i: (i, 0)))
        out_shape.append(jax.ShapeDtypeStruct((M, N), F32))
        args.append(w_proj)
    return pl.pallas_call(
        functools.partial(_norm_kernel, with_proj=w_proj is not None),
        grid=(M // tr,),
        in_specs=in_specs,
        out_specs=out_specs,
        out_shape=out_shape,
        compiler_params=_cparams("parallel"),
        name="norm_proj" if w_proj is not None else "norm",
    )(*args)


def _mm_kernel(a_ref, w_ref, *refs, residual):
    if residual:
        res_ref, o_ref = refs
        o_ref[...] = res_ref[...] + _dot(a_ref[...], w_ref[...])
    else:
        (o_ref,) = refs
        o_ref[...] = _dot(a_ref[...], w_ref[...]).astype(o_ref.dtype)


def _matmul(a, w, out_dtype, *, res=None, tm_pref=1024, tn_pref=512, name="matmul"):
    M, K = a.shape
    N = w.shape[1]
    tm, tn = _tile(M, tm_pref), _tile(N, tn_pref)
    out_blk = pl.BlockSpec((tm, tn), lambda i, j: (i, j))
    in_specs = [pl.BlockSpec((tm, K), lambda i, j: (i, 0)),
                pl.BlockSpec((K, tn), lambda i, j: (0, j))]
    args = [a, w]
    if res is not None:
        in_specs.append(out_blk)
        args.append(res)
    return pl.pallas_call(
        functools.partial(_mm_kernel, residual=res is not None),
        grid=(M // tm, N // tn),
        in_specs=in_specs,
        out_specs=out_blk,
        out_shape=jax.ShapeDtypeStruct((M, N), out_dtype),
        compiler_params=_cparams("parallel", "arbitrary"),
        name=name,
    )(*args)


def _mm_ws_kernel(a_ref, w_ref, res_ref, o_ref, wbf_ref):
    @pl.when(pl.program_id(1) == 0)
    def _():
        wbf_ref[...] = w_ref[...].astype(BF)

    o_ref[...] = res_ref[...] + _dot(a_ref[...], wbf_ref[...])


def _matmul_ws(a, w, layer, res, *, tm_pref=1024, tn_pref=512, name="matmul_ws"):
    M, K = a.shape
    N = w.shape[-1]
    tm, tn = _tile(M, tm_pref), _tile(N, tn_pref)
    out_blk = pl.BlockSpec((tm, tn), lambda j, i: (i, j))
    return pl.pallas_call(
        _mm_ws_kernel,
        grid=(N // tn, M // tm),
        in_specs=[pl.BlockSpec((tm, K), lambda j, i: (i, 0)),
                  pl.BlockSpec((None, K, tn), lambda j, i: (layer, 0, j)),
                  out_blk],
        out_specs=out_blk,
        out_shape=jax.ShapeDtypeStruct((M, N), res.dtype),
        scratch_shapes=[pltpu.VMEM((K, tn), BF)],
        compiler_params=_cparams("parallel", "arbitrary"),
        name=name,
    )(a, w, res)


def _mm_wst_kernel(a_ref, wt_hbm, o_ref, wf32_ref, wbf_ref, sem, *, layer, tn, n_j, n_head, skip):
    j = pl.program_id(0)

    def fetch(t):
        start = pl.multiple_of(t * tn + jnp.where(t >= n_head, skip, 0), SUBLANES)
        return pltpu.make_async_copy(wt_hbm.at[layer, pl.ds(start, tn)], wf32_ref.at[t % 2],
                                     sem.at[t % 2])

    @pl.when(pl.program_id(1) == 0)
    def _():
        @pl.when(j == 0)
        def _():
            fetch(j).start()

        fetch(j).wait()
        wbf_ref[...] = wf32_ref[j % 2].astype(BF)

        @pl.when(j + 1 < n_j)
        def _():
            fetch(j + 1).start()

    o_ref[...] = _dot_nt(a_ref[...], wbf_ref[...]).astype(o_ref.dtype)


def _matmul_wst(a, wt, layer, n_head, skip, n_out, out_dtype, *, tm_pref=1024, tn_pref=512,
                name="matmul_wst"):
    M, K = a.shape
    tm, tn = _tile(M, tm_pref), _tile(n_out, tn_pref)
    assert n_head % tn == 0 and skip % SUBLANES == 0
    n_j = n_out // tn
    return pl.pallas_call(
        functools.partial(_mm_wst_kernel, layer=layer, tn=tn, n_j=n_j, n_head=n_head // tn, skip=skip),
        grid=(n_j, M // tm),
        in_specs=[pl.BlockSpec((tm, K), lambda j, i: (i, 0)),
                  pl.BlockSpec(memory_space=pl.ANY)],
        out_specs=pl.BlockSpec((tm, tn), lambda j, i: (i, j)),
        out_shape=jax.ShapeDtypeStruct((M, n_out), out_dtype),
        scratch_shapes=[pltpu.VMEM((2, tn, K), F32), pltpu.VMEM((tn, K), BF),
                        pltpu.SemaphoreType.DMA((2,))],
        compiler_params=_cparams("arbitrary", "arbitrary"),
        name=name,
    )(a, wt)


def _split3(x):
    hi = x.astype(BF)
    r1 = x - hi.astype(F32)
    mid = r1.astype(BF)
    lo = (r1 - mid.astype(F32)).astype(BF)
    return hi, mid, lo


def _gate_terms(ba, eb, ea, sel, alog, dtb):
    R = ba.shape[0]
    beta = jax.nn.sigmoid(ba)
    x = ba + dtb
    softplus = jnp.maximum(x, 0.0) + jnp.log1p(jnp.exp(-jnp.abs(x)))
    g = -jnp.exp(alog) * softplus
    ri = lax.broadcasted_iota(jnp.int32, (R, R), 0)
    ci = lax.broadcasted_iota(jnp.int32, (R, R), 1)
    tri = ((ri // DN_CHUNK == ci // DN_CHUNK) & (ri >= ci)).astype(BF)
    gc = sum(_dot(tri, piece) for piece in _split3(g))
    gc_parts = _split3(gc)
    beta_b = sum(_dot(piece, eb) for piece in _split3(beta))
    gc_b = sum(_dot(piece, ea) for piece in gc_parts)
    gc_t = sum(_dot_nt(sel, piece) for piece in gc_parts)
    return beta_b, gc_b, gc_t


def _group_rows(heads_per_group):
    return -(-heads_per_group // SUBLANES) * SUBLANES


def _gate_constants(a_log, dt_bias, n_heads, heads_per_group):
    G, H = heads_per_group, n_heads
    W = H * DN_HEAD_DIM
    lane_head = jnp.arange(W) // DN_HEAD_DIM
    rows = jnp.arange(128)[:, None]
    eb = (rows == lane_head[None, :]).astype(BF)
    ea = (rows == lane_head[None, :] + H).astype(BF)
    rpg = _group_rows(G)
    r = jnp.arange((H // G) * rpg)
    head_of_row = jnp.where(r % rpg < G, (r // rpg) * G + r % rpg, -128)
    sel = (jnp.arange(128)[None, :] == head_of_row[:, None] + H).astype(BF)
    at_a = lambda v: jnp.pad(v.astype(F32), (H, 128 - 2 * H))[None, :]
    return eb, ea, sel, at_a(a_log), at_a(dt_bias)


def _shift_rows(t, halo, k):
    r = pltpu.roll(t, k, 0)
    hr = pltpu.roll(halo, k, 0)[:8]
    rows = lax.broadcasted_iota(jnp.int32, (8, t.shape[1]), 0)
    first = jnp.where(rows < k, hr, r[:8])
    return jnp.concatenate([first, r[8:]], axis=0)


def _causal_conv(t, halo, w):
    K = w.shape[0]
    y = t * w[K - 1:K]
    for d in range(1, K):
        y = y + _shift_rows(t, halo, d) * w[K - 1 - d:K - d]
    return y


def _halo_spec(rows_per_tile, width, col_block, n_lead):
    per = rows_per_tile // HALO
    if n_lead == 1:
        return pl.BlockSpec((HALO, width), lambda i: (jnp.maximum(i * per - 1, 0), col_block))
    return pl.BlockSpec((HALO, width), lambda i, g: (jnp.maximum(i * per - 1, 0), col_block(g)))


def _conv_a_kernel(cx_ref, cc_ref, cb_ref, hx_ref, hc_ref, w_ref, o_ref, *, tiles_per_seq):
    first = (pl.program_id(0) % tiles_per_seq == 0)
    keep = jnp.where(first, 0.0, 1.0)
    t = cx_ref[...].astype(F32) * cc_ref[...].astype(F32)
    halo = hx_ref[...].astype(F32) * hc_ref[...].astype(F32) * keep
    y = _causal_conv(t, halo, w_ref[...])
    o_ref[...] = (cb_ref[...].astype(F32) * y).astype(o_ref.dtype)


def _conv_a(p, conv_w, seq, width, tr_pref=256):
    M = p.shape[0]
    tr = _tile(seq, tr_pref)
    blk = lambda c: pl.BlockSpec((tr, width), lambda i: (i, c))
    return pl.pallas_call(
        functools.partial(_conv_a_kernel, tiles_per_seq=seq // tr),
        grid=(M // tr,),
        in_specs=[blk(0), blk(1), blk(2), _halo_spec(tr, width, 0, 1), _halo_spec(tr, width, 1, 1),
                  pl.BlockSpec(conv_w.shape, lambda i: (0, 0))],
        out_specs=pl.BlockSpec((tr, width), lambda i: (i, 0)),
        out_shape=jax.ShapeDtypeStruct((M, width), BF),
        compiler_params=_cparams("parallel"),
        name="conv_a",
    )(p, p, p, p, p, conv_w)


def _dn_prep_kernel(q_ref, k_ref, v_ref, hq_ref, hk_ref, hv_ref, wq_ref, wk_ref, wv_ref,
                    ba_ref, eb_ref, ea_ref, sel_ref, alog_ref, dtb_ref,
                    qeff_ref, wdec_ref, ktail_ref, ubase_ref, obase_ref, glast_ref,
                    *, tiles_per_seq, heads, wave):
    R = q_ref.shape[0]
    C, DK = DN_CHUNK, DN_HEAD_DIM
    first = (pl.program_id(0) % tiles_per_seq == 0)
    keep = jnp.where(first, 0.0, 1.0)
    beta_b, gc_b, gc_t = _gate_terms(ba_ref[...], eb_ref[...], ea_ref[...], sel_ref[...],
                                     alog_ref[...], dtb_ref[...])
    for c in range(R // C):
        glast_ref[c] = jnp.exp(gc_b[c * C + C - 1:c * C + C, :])
    ri = lax.broadcasted_iota(jnp.int32, (R, R), 0)
    ci = lax.broadcasted_iota(jnp.int32, (R, R), 1)
    same = (ri // C) == (ci // C)
    incl = same & (ri >= ci)
    strict = same & (ri > ci)

    def l2n(x):
        return x * lax.rsqrt(jnp.sum(x * x, axis=-1, keepdims=True) + EPS)

    def head_inputs(hh):
        cols = slice(hh * DK, (hh + 1) * DK)

        def conv_silu(t_ref, h_ref, w_ref):
            y = _causal_conv(t_ref[:, cols].astype(F32), h_ref[:, cols].astype(F32) * keep,
                             w_ref[:, cols])
            return _silu(y)

        q = l2n(conv_silu(q_ref, hq_ref, wq_ref)) * (DK ** -0.5)
        k = l2n(conv_silu(k_ref, hk_ref, wk_ref))
        v = conv_silu(v_ref, hv_ref, wv_ref)
        beta = beta_b[:, cols]
        gc = gc_b[:, cols]
        g_row = jnp.broadcast_to(gc_t[hh:hh + 1, :], (R, R))
        g_col = jnp.concatenate([gc] * (R // DK), axis=1)
        e = jnp.exp(jnp.where(incl, g_col - g_row, 0.0))
        decay_incl = jnp.where(incl, e, 0.0)
        decay_strict = jnp.where(strict, e, 0.0)
        kb = k * beta
        k16 = k.astype(BF)
        m0 = _dot_nt((-kb).astype(BF), k16) * decay_strict
        aqk = (_dot_nt(q.astype(BF), k16) * decay_incl).astype(BF)
        rhs = jnp.concatenate([v * beta, kb * jnp.exp(gc)], axis=1)
        return dict(cols=cols, q=q, k=k, gc=gc, aqk=aqk, rhs=rhs, m=m0)

    n_fac = C.bit_length() - 1
    waves = [list(range(w0, min(w0 + wave, heads))) for w0 in range(0, heads, wave)]
    hs = [head_inputs(hh) for hh in waves[0]]
    for wi in range(len(waves)):
        todo = list(waves[wi + 1]) if wi + 1 < len(waves) else []
        hs_next = []
        for j in range(n_fac):
            for d in hs:
                m16 = d["m"].astype(BF)
                if j == 0:
                    d["qm"] = d["m"]
                    d["m"] = _dot(m16, m16)
                elif j < n_fac - 1:
                    r = _dot(jnp.concatenate([m16, d["qm"].astype(BF)], axis=0), m16)
                    d["qm"] = d["qm"] + d["m"] + r[R:]
                    d["m"] = r[:R]
                else:
                    d["qm"] = d["qm"] + d["m"] + _dot(d["qm"].astype(BF), m16)
            n_emit = -(-len(todo) // (n_fac - j))
            hs_next += [head_inputs(hh) for hh in todo[:n_emit]]
            todo = todo[n_emit:]
        for d in hs:
            d["sol"] = d["rhs"] + _dot(d["qm"].astype(BF), d["rhs"].astype(BF))
        for d in hs:
            cols, sol, gc, q, k = d["cols"], d["sol"], d["gc"], d["q"], d["k"]
            ubase, wdec = sol[:, :DK], sol[:, DK:]
            x = _dot(d["aqk"], jnp.concatenate([wdec, ubase], axis=1).astype(BF))
            gc_last = jnp.concatenate(
                [jnp.broadcast_to(gc[c * C + C - 1:c * C + C, :], (C, DK)) for c in range(R // C)],
                axis=0)
            qeff_ref[:, cols] = (q * jnp.exp(gc) - x[:, :DK]).astype(qeff_ref.dtype)
            obase_ref[:, cols] = x[:, DK:]
            ubase_ref[:, cols] = ubase
            wdec_ref[:, cols] = wdec.astype(wdec_ref.dtype)
            ktail_ref[:, cols] = (k * jnp.exp(gc_last - gc)).astype(ktail_ref.dtype)
        hs = hs_next


def _dn_prep(p, qkv_conv_w, ba, a_log, dt_bias, seq, n_heads, heads_per_group, q_off):
    M = p.shape[0]
    G = heads_per_group
    GW = G * DN_HEAD_DIM
    W = n_heads * DN_HEAD_DIM
    n_groups = n_heads // G
    R = _tile(seq, 256)
    rpg = _group_rows(G)
    assert q_off % GW == 0 and R % DN_HEAD_DIM == 0
    eb, ea, sel, alog_c, dtb_c = _gate_constants(a_log, dt_bias, n_heads, G)
    cols = [lambda g, o=o: q_off // GW + o * n_groups + g for o in range(3)]
    blk = lambda c: pl.BlockSpec((R, GW), lambda i, g: (i, c(g)))
    wblk = lambda o: pl.BlockSpec((qkv_conv_w.shape[0], GW), lambda i, g: (0, o * n_groups + g))
    head = pl.BlockSpec((R, GW), lambda i, g: (i, g))
    lanes = pl.BlockSpec((1, 128), lambda i, g: (0, 0))
    n_ch = R // DN_CHUNK
    return pl.pallas_call(
        functools.partial(_dn_prep_kernel, tiles_per_seq=seq // R, heads=G, wave=min(G, 4)),
        grid=(M // R, n_groups),
        in_specs=[blk(cols[0]), blk(cols[1]), blk(cols[2]),
                  _halo_spec(R, GW, cols[0], 2), _halo_spec(R, GW, cols[1], 2),
                  _halo_spec(R, GW, cols[2], 2),
                  wblk(0), wblk(1), wblk(2),
                  pl.BlockSpec((R, 128), lambda i, g: (i, 0)),
                  pl.BlockSpec((128, GW), lambda i, g: (0, g)),
                  pl.BlockSpec((128, GW), lambda i, g: (0, g)),
                  pl.BlockSpec((rpg, 128), lambda i, g: (g, 0)), lanes, lanes],
        out_specs=[head] * 5 + [pl.BlockSpec((n_ch, 1, GW), lambda i, g: (i, 0, g))],
        out_shape=[jax.ShapeDtypeStruct((M, W), BF)] * 3 + [jax.ShapeDtypeStruct((M, W), F32)] * 2
        + [jax.ShapeDtypeStruct((M // DN_CHUNK, 1, W), F32)],
        compiler_params=_cparams("parallel", "parallel"),
        name="dn_prep",
    )(p, p, p, p, p, p, qkv_conv_w, qkv_conv_w, qkv_conv_w, ba, eb, ea, sel, alog_c, dtb_c)


def _dn_scan_kernel(qeff_ref, wdec_ref, ktail_ref, ubase_ref, obase_ref, glast_ref, z_ref, nw_ref,
                    o_ref, state_ref, *, heads):
    rows_per_step = qeff_ref.shape[0]
    C, DK = DN_CHUNK, DN_HEAD_DIM

    @pl.when(pl.program_id(2) == 0)
    def _():
        state_ref[...] = jnp.zeros_like(state_ref)

    nw = nw_ref[...]

    def chunk(c, carry):
        r0 = pl.multiple_of(c * C, C)
        rows = pl.ds(r0, C)
        g_last_all = glast_ref[c]
        for h in range(heads):
            cols = slice(h * DK, (h + 1) * DK)
            st = state_ref[h]
            lhs = jnp.concatenate([qeff_ref[rows, cols], wdec_ref[rows, cols]], axis=0)
            r = _dot(lhs, st.astype(BF))
            o = obase_ref[rows, cols] + r[:C]
            u = ubase_ref[rows, cols] - r[C:]
            g_last = g_last_all[:, cols]
            upd =lax.dot_general(ktail_ref[rows, cols], u.astype(BF), (((0,), (0,)), ((), ())),
                                  preferred_element_type=F32)
            state_ref[h] = st * g_last + upd
            zf = z_ref[rows, cols].astype(F32)
            on = o * lax.rsqrt(jnp.mean(o * o, axis=-1, keepdims=True) + EPS) * nw * _silu(zf)
            o_ref[rows, cols] = on.astype(o_ref.dtype)
        return carry

    lax.fori_loop(0, rows_per_step // C, chunk, 0)


def _dn_scan(qeff, wdec, ktail, ubase, obase, glast, p, dn_out_norm, seq, n_heads, heads_per_group,
             z_off, ts_pref=512):
    M, W = qeff.shape
    G = heads_per_group
    GW = G * DN_HEAD_DIM
    ts = _tile(seq, ts_pref)
    per = seq // ts
    assert z_off % GW == 0
    zb = z_off // GW
    blk = pl.BlockSpec((ts, GW), lambda b, g, t: (b * per + t, g))
    return pl.pallas_call(
        functools.partial(_dn_scan_kernel, heads=G),
        grid=(M // seq, n_heads // G, per),
        in_specs=[blk] * 5
        + [pl.BlockSpec((ts // DN_CHUNK, 1, GW), lambda b, g, t: (b * per + t, 0, g)),
           pl.BlockSpec((ts, GW), lambda b, g, t: (b * per + t, zb + g)),
           pl.BlockSpec((1, DN_HEAD_DIM), lambda b, g, t: (0, 0))],
        out_specs=blk,
        out_shape=jax.ShapeDtypeStruct((M, W), BF),
        scratch_shapes=[pltpu.VMEM((G, DN_HEAD_DIM, DN_HEAD_DIM), F32)],
        compiler_params=_cparams("parallel", "parallel", "arbitrary"),
        name="dn_scan",
    )(qeff, wdec, ktail, ubase, obase, glast, p, dn_out_norm)


def _merge_kernel(ya_ref, yb_ref, wa_ref, wb_ref, ga_ref, gb_ref, o_ref, wa16_ref, wb16_ref):
    @pl.when(pl.program_id(1) == 0)
    def _():
        wa16_ref[...] = wa_ref[...].astype(BF)
        wb16_ref[...] = wb_ref[...].astype(BF)

    ya = _dot(ya_ref[...], wa16_ref[...])
    yb = _dot(yb_ref[...], wb16_ref[...])
    m = jax.nn.sigmoid(ga_ref[...].astype(F32)) * ya + jax.nn.sigmoid(gb_ref[...].astype(F32)) * yb
    o_ref[...] = m.astype(o_ref.dtype)


def _merge(ya, yb, wa, wb, layer, p, gate_off, tm_pref=1024, tn_pref=512):
    M, KA = ya.shape
    KB = yb.shape[1]
    N = wa.shape[-1]
    tm, tn = _tile(M, tm_pref), _tile(N, tn_pref)
    while gate_off % tn:
        tn //= 2
    nb = N // tn
    g0 = gate_off // tn
    return pl.pallas_call(
        _merge_kernel,
        grid=(nb, M // tm),
        in_specs=[pl.BlockSpec((tm, KA), lambda j, i: (i, 0)),
                  pl.BlockSpec((tm, KB), lambda j, i: (i, 0)),
                  pl.BlockSpec((None, KA, tn), lambda j, i: (layer, 0, j)),
                  pl.BlockSpec((None, KB, tn), lambda j, i: (layer, 0, j)),
                  pl.BlockSpec((tm, tn), lambda j, i: (i, g0 + j)),
                  pl.BlockSpec((tm, tn), lambda j, i: (i, g0 + nb + j))],
        out_specs=pl.BlockSpec((tm, tn), lambda j, i: (i, j)),
        out_shape=jax.ShapeDtypeStruct((M, N), BF),
        scratch_shapes=[pltpu.VMEM((KA, tn), BF), pltpu.VMEM((KB, tn), BF)],
        compiler_params=_cparams("parallel", "arbitrary"),
        name="merge",
    )(ya, yb, wa, wb, p, p)


def _mem_kv_kernel(m_ref, nw_ref, wk_ref, wv_ref, k_ref, v_ref):
    mn = _rms(m_ref[...], nw_ref[...]).astype(BF)
    k_ref[...] = _dot(mn, wk_ref[...].astype(BF)).astype(k_ref.dtype)
    v_ref[...] = _dot(mn, wv_ref[...].astype(BF)).astype(v_ref.dtype)


def _mem_kv(mem, nw, wk, wv, tr_pref=256):
    Mm, D = mem.shape
    N = wk.shape[1]
    tr = _tile(Mm, tr_pref)
    full = lambda shape: pl.BlockSpec(shape, lambda i: (0, 0))
    return pl.pallas_call(
        _mem_kv_kernel,
        grid=(Mm // tr,),
        in_specs=[pl.BlockSpec((tr, D), lambda i: (i, 0)), full((1, D)), full((D, N)), full((D, N))],
        out_specs=[pl.BlockSpec((tr, N), lambda i: (i, 0))] * 2,
        out_shape=[jax.ShapeDtypeStruct((Mm, N), BF)] * 2,
        compiler_params=_cparams("parallel"),
        name="mem_kv",
    )(mem, nw, wk, wv)


def _xattn_kernel(h_ref, nw_ref, wq_ref, k_ref, v_ref, wo_ref, nw2_ref, o_ref, on_ref, *, n_heads):
    DH = XATTN_HEAD_DIM
    h = h_ref[...]
    q = _dot(_rms(h, nw_ref[...]).astype(BF), wq_ref[...])
    outs = []
    for hd in range(n_heads):
        cols = slice(hd * DH, (hd + 1) * DH)
        s = _dot_nt(q[:, cols].astype(BF), k_ref[:, cols]) * (DH ** -0.5)
        s = s - jnp.max(s, axis=-1, keepdims=True)
        e = jnp.exp(s)
        pr = e / jnp.sum(e, axis=-1, keepdims=True)
        outs.append(_dot(pr.astype(BF), v_ref[:, cols]))
    o = jnp.concatenate(outs, axis=1).astype(BF)
    h_out = h + _dot(o, wo_ref[...])
    o_ref[...] = h_out
    on_ref[...] = _rms(h_out, nw2_ref[...]).astype(on_ref.dtype)


def _xattn(h, nw, wq, kh, vh, wo, nw2, seq, n_mem, tm_pref=512):
    M, D = h.shape
    N = wq.shape[1]
    tm = _tile(seq, tm_pref)
    per = seq // tm
    full = lambda shape: pl.BlockSpec(shape, lambda i: (0, 0))
    kv = pl.BlockSpec((n_mem, N), lambda i: (i // per, 0))
    row = pl.BlockSpec((tm, D), lambda i: (i, 0))
    return pl.pallas_call(
        functools.partial(_xattn_kernel, n_heads=N // XATTN_HEAD_DIM),
        grid=(M // tm,),
        in_specs=[row, full((1, D)), full((D, N)), kv, kv, full((N, D)), full((1, D))],
        out_specs=[row, row],
        out_shape=[jax.ShapeDtypeStruct((M, D), F32), jax.ShapeDtypeStruct((M, D), BF)],
        compiler_params=_cparams("parallel"),
        name="xattn",
    )(h, nw, wq, kh, vh, wo, nw2)


def kernel(x, mem, ffn1_norm, ffn1_w_gate, ffn1_w_up, ffn1_w_down, mix_norm, w_in, conv_w, qkv_conv_w, a_log, dt_bias, dn_out_norm, w_out_conv, w_out_delta, w_o, xattn_norm, mem_norm, xattn_wq, xattn_wk, xattn_wv, xattn_wo, ffn2_norm, ffn2_w_gate, ffn2_w_up, ffn2_w_down, final_norm):
    B, S, D = x.shape
    depth = ffn1_norm.shape[0]
    n_mem = mem.shape[1]
    CW = conv_w.shape[-1]
    H = a_log.shape[-1]
    DW = H * DN_HEAD_DIM
    G = min(H, 16)
    M = B * S
    assert S % DN_CHUNK == 0 and 2 * H <= 128 and H % G == 0
    o_qkv = 3 * CW
    o_z = o_qkv + 3 * DW
    o_b = o_z + DW
    o_ga = o_b + 2 * H
    assert w_in.shape[-1] == o_ga + 2 * D
    vec = lambda v: v.reshape(1, -1).astype(F32)
    bf = lambda w: w.astype(BF)

    h = x.reshape(M, D)
    mem2 = mem.reshape(B * n_mem, D)
    for l in range(depth):
        h = _ffn(_norm(h, vec(ffn1_norm[l]))[0], h, ffn1_w_gate, ffn1_w_up, ffn1_w_down,
                 vec(mix_norm[l]), l, norm_out=False, name="ffn1")

        w_in_t = jnp.swapaxes(w_in, 1, 2)
        w_ba = bf(jnp.pad(w_in_t[l, o_b:o_ga], ((0, 128 - 2 * H), (0, 0))))
        un, ba = _norm(h, vec(mix_norm[l]), w_ba)
        p = _matmul_wst(un, w_in_t, l, o_b, o_ga - o_b, o_b + 2 * D, BF, name="in_proj")

        ya = _conv_a(p, conv_w[l].astype(F32), S, CW)
        qeff, wdec, ktail, ubase, obase, glast = _dn_prep(
            p, qkv_conv_w[l].astype(F32), ba, a_log[l], dt_bias[l], S, H, G, o_qkv)
        yb = _dn_scan(qeff, wdec, ktail, ubase, obase, glast, p, vec(dn_out_norm[l]), S, H, G, o_z)

        merged = _merge(ya, yb, w_out_conv, w_out_delta, l, p, o_b)
        h = _matmul_ws(merged, w_o, l, h, name="oproj")

        kh, vh = _mem_kv(mem2, vec(mem_norm[l]), xattn_wk[l], xattn_wv[l])
        h, xn = _xattn(h, vec(xattn_norm[l]), bf(xattn_wq[l]), kh, vh, bf(xattn_wo[l]),
                       vec(ffn2_norm[l]), S, n_mem)

        h = _ffn(xn, h, ffn2_w_gate, ffn2_w_up, ffn2_w_down, vec(final_norm), l,
                 norm_out=(l == depth - 1), name="ffn2")
    return h.reshape(B, S, D)
```

```python
import functools

import jax
import jax.numpy as jnp
from jax import lax
from jax.experimental import pallas as pl
from jax.experimental.pallas import tpu as pltpu

EPS = 1e-6
BF = jnp.bfloat16
F32 = jnp.float32

LANES = 128
DN_HEAD_DIM = 128
DN_CHUNK = 64
DN_HEADS_PER_STEP = 16
DN_WAVE = 4
XATTN_HEAD_DIM = 128
HALO = 16
SUBLANES = 8
VMEM_BYTES_V7X = 64 * 1024 * 1024
VMEM_LIMIT = VMEM_BYTES_V7X - 4 * 1024 * 1024


def _cparams(*sem):
    return pltpu.CompilerParams(dimension_semantics=sem, vmem_limit_bytes=VMEM_LIMIT)


def _tile(dim, pref):
    t = min(pref, dim)
    while dim % t:
        t //= 2
    return t


def _rms(x, w):
    return x * lax.rsqrt(jnp.mean(x * x, axis=-1, keepdims=True) + EPS) * w


def _silu(x):
    return x * jax.nn.sigmoid(x)


def _dot(a, b):
    return jnp.dot(a, b, preferred_element_type=F32)


def _dot_nt(a, b):
    return lax.dot_general(a, b, (((1,), (1,)), ((), ())), preferred_element_type=F32)


def _ffn_kernel(xn_ref, x_hbm, wg_ref, wu_ref, wd_ref, ow_ref, o_hbm, acc_ref, sem_in, sem_out,
                *, n_i, n_f, tm, col_chunk, row_chunk, norm_out):
    i = pl.program_id(0)
    f = pl.program_id(1)
    D = acc_ref.shape[1]
    n_cc, n_rc = D // col_chunk, tm // row_chunk
    n_pieces = n_rc if norm_out else n_cc

    def rows(t):
        return pl.ds(pl.multiple_of(t * tm, tm), tm)

    def load(t):
        return pltpu.make_async_copy(x_hbm.at[rows(t)], acc_ref, sem_in)

    def store(t, k):
        if norm_out:
            src = acc_ref.at[pl.ds(k * row_chunk, row_chunk)]
            dst = o_hbm.at[pl.ds(pl.multiple_of(t * tm + k * row_chunk, row_chunk), row_chunk)]
        else:
            src = acc_ref.at[:, pl.ds(k * col_chunk, col_chunk)]
            dst = o_hbm.at[rows(t), pl.ds(k * col_chunk, col_chunk)]
        return pltpu.make_async_copy(src, dst, sem_out.at[k])

    @pl.when(f == 0)
    def _():
        @pl.when(i > 0)
        def _():
            for k in range(n_pieces):
                store(i - 1, k).wait()

        load(i).start()

    xn = xn_ref[...]
    g = _dot(xn, wg_ref[...].astype(BF))
    u = _dot(xn, wu_ref[...].astype(BF))
    a = (0.5 * _silu(g) * u).astype(BF)

    @pl.when(f == 0)
    def _():
        load(i).wait()

    def down(last):
        for c in range(n_cc):
            cs = slice(c * col_chunk, (c + 1) * col_chunk)
            acc_ref[:, cs] += _dot(a, wd_ref[:, cs].astype(BF))
            if last and not norm_out:
                store(i, c).start()
        if last and norm_out:
            for r in range(n_rc):
                rs = slice(r * row_chunk, (r + 1) * row_chunk)
                acc_ref[rs, :] = _rms(acc_ref[rs, :], ow_ref[...])
                store(i, r).start()
        if last:
            @pl.when(i == n_i - 1)
            def _():
                for k in range(n_pieces):
                    store(i, k).wait()

    @pl.when(f < n_f - 1)
    def _():
        down(False)

    @pl.when(f == n_f - 1)
    def _():
        down(True)


def _ffn(xn, x, wg, wu, wd, ow, layer, *, norm_out, tm_pref=1024, tf_pref=256, name="ffn"):
    M, D = x.shape
    F = wg.shape[-1]
    tm, tf = _tile(M, tm_pref), _tile(F, tf_pref)
    n_i, n_f = M // tm, F // tf
    col_chunk, row_chunk = _tile(D, 1024), _tile(tm, 256)
    n_pieces = tm // row_chunk if norm_out else D // col_chunk
    return pl.pallas_call(
        functools.partial(_ffn_kernel, n_i=n_i, n_f=n_f, tm=tm, col_chunk=col_chunk,
                          row_chunk=row_chunk, norm_out=norm_out),
        grid=(n_i, n_f),
        in_specs=[
            pl.BlockSpec((tm, D), lambda i, f: (i, 0), pipeline_mode=pl.Buffered(1)),
            pl.BlockSpec(memory_space=pl.ANY),
            pl.BlockSpec((None, D, tf), lambda i, f: (layer, 0, f)),
            pl.BlockSpec((None, D, tf), lambda i, f: (layer, 0, f)),
            pl.BlockSpec((None, tf, D), lambda i, f: (layer, f, 0)),
            pl.BlockSpec((1, D), lambda i, f: (0, 0)),
        ],
        out_specs=pl.BlockSpec(memory_space=pl.ANY),
        out_shape=jax.ShapeDtypeStruct((M, D), F32),
        scratch_shapes=[pltpu.VMEM((tm, D), F32), pltpu.SemaphoreType.DMA(()),
                        pltpu.SemaphoreType.DMA((n_pieces,))],
        compiler_params=_cparams("arbitrary", "arbitrary"),
        name=name,
    )(xn, x, wg, wu, wd, ow)


def _norm_kernel(x_ref, w_ref, *refs, with_proj):
    xn = _rms(x_ref[...], w_ref[...]).astype(BF)
    if with_proj:
        wp_ref, o_ref, p_ref = refs
        p_ref[...] = _dot_nt(xn, wp_ref[...])
    else:
        (o_ref,) = refs
    o_ref[...] = xn


def _norm(x, w, w_proj=None, tr_pref=512):
    M, D = x.shape
    tr = _tile(M, tr_pref)
    in_specs = [pl.BlockSpec((tr, D), lambda i: (i, 0)), pl.BlockSpec((1, D), lambda i: (0, 0))]
    out_specs = [pl.BlockSpec((tr, D), lambda i: (i, 0))]
    out_shape = [jax.ShapeDtypeStruct((M, D), BF)]
    args = [x, w]
    if w_proj is not None:
        N = w_proj.shape[0]
        in_specs.append(pl.BlockSpec((N, D), lambda i: (0, 0)))
        out_specs.append(pl.BlockSpec((tr, N), lambda i: (i, 0)))
        out_shape.append(jax.ShapeDtypeStruct((M, N), F32))
        args.append(w_proj)
    return pl.pallas_call(
        functools.partial(_norm_kernel, with_proj=w_proj is not None),
        grid=(M // tr,),
        in_specs=in_specs,
        out_specs=out_specs,
        out_shape=out_shape,
        compiler_params=_cparams("parallel"),
        name="norm_proj" if w_proj is not None else "norm",
    )(*args)


def _mm_ws_kernel(a_ref, w_ref, res_ref, o_ref, wbf_ref):
    @pl.when(pl.program_id(1) == 0)
    def _():
        wbf_ref[...] = w_ref[...].astype(BF)

    o_ref[...] = res_ref[...] + _dot(a_ref[...], wbf_ref[...])


def _matmul_ws(a, w, layer, res, *, tm_pref=1024, tn_pref=512, name="matmul_ws"):
    M, K = a.shape
    N = w.shape[-1]
    tm, tn = _tile(M, tm_pref), _tile(N, tn_pref)
    out_blk = pl.BlockSpec((tm, tn), lambda j, i: (i, j))
    return pl.pallas_call(
        _mm_ws_kernel,
        grid=(N // tn, M // tm),
        in_specs=[pl.BlockSpec((tm, K), lambda j, i: (i, 0)),
                  pl.BlockSpec((None, K, tn), lambda j, i: (layer, 0, j)),
                  out_blk],
        out_specs=out_blk,
        out_shape=jax.ShapeDtypeStruct((M, N), res.dtype),
        scratch_shapes=[pltpu.VMEM((K, tn), BF)],
        compiler_params=_cparams("parallel", "arbitrary"),
        name=name,
    )(a, w, res)


def _mm_wst_kernel(a_ref, wt_hbm, o_ref, wf32_ref, wbf_ref, sem, *, layer, tn, n_j, n_head, skip):
    j = pl.program_id(0)

    def fetch(t):
        start = pl.multiple_of(t * tn + jnp.where(t >= n_head, skip, 0), SUBLANES)
        return pltpu.make_async_copy(wt_hbm.at[layer, pl.ds(start, tn)], wf32_ref.at[t % 2],
                                     sem.at[t % 2])

    @pl.when(pl.program_id(1) == 0)
    def _():
        @pl.when(j == 0)
        def _():
            fetch(j).start()

        fetch(j).wait()
        wbf_ref[...] = wf32_ref[j % 2].astype(BF)

        @pl.when(j + 1 < n_j)
        def _():
            fetch(j + 1).start()

    o_ref[...] = _dot_nt(a_ref[...], wbf_ref[...]).astype(o_ref.dtype)


def _matmul_wst(a, wt, layer, n_head, skip, n_out, out_dtype, *, tm_pref=1024, tn_pref=512,
                name="matmul_wst"):
    M, K = a.shape
    tm, tn = _tile(M, tm_pref), _tile(n_out, tn_pref)
    assert n_head % tn == 0 and skip % SUBLANES == 0
    n_j = n_out // tn
    return pl.pallas_call(
        functools.partial(_mm_wst_kernel, layer=layer, tn=tn, n_j=n_j, n_head=n_head // tn, skip=skip),
        grid=(n_j, M // tm),
        in_specs=[pl.BlockSpec((tm, K), lambda j, i: (i, 0)),
                  pl.BlockSpec(memory_space=pl.ANY)],
        out_specs=pl.BlockSpec((tm, tn), lambda j, i: (i, j)),
        out_shape=jax.ShapeDtypeStruct((M, n_out), out_dtype),
        scratch_shapes=[pltpu.VMEM((2, tn, K), F32), pltpu.VMEM((tn, K), BF),
                        pltpu.SemaphoreType.DMA((2,))],
        compiler_params=_cparams("arbitrary", "arbitrary"),
        name=name,
    )(a, wt)


def _split3(x):
    hi = x.astype(BF)
    r1 = x - hi.astype(F32)
    mid = r1.astype(BF)
    lo = (r1 - mid.astype(F32)).astype(BF)
    return hi, mid, lo


def _gate_terms(ba, eb, ea, sel, alog, dtb):
    R = ba.shape[0]
    beta = jax.nn.sigmoid(ba)
    x = ba + dtb
    softplus = jnp.maximum(x, 0.0) + jnp.log1p(jnp.exp(-jnp.abs(x)))
    g = -jnp.exp(alog) * softplus
    ri = lax.broadcasted_iota(jnp.int32, (R, R), 0)
    ci = lax.broadcasted_iota(jnp.int32, (R, R), 1)
    tri = ((ri // DN_CHUNK == ci // DN_CHUNK) & (ri >= ci)).astype(BF)
    gc = sum(_dot(tri, piece) for piece in _split3(g))
    gc_parts = _split3(gc)
    beta_b = sum(_dot(piece, eb) for piece in _split3(beta))
    gc_b = sum(_dot(piece, ea) for piece in gc_parts)
    gc_t = sum(_dot_nt(sel, piece) for piece in gc_parts)
    return beta_b, gc_b, gc_t


def _group_rows(heads_per_group):
    return -(-heads_per_group // SUBLANES) * SUBLANES


def _gate_constants(a_log, dt_bias, n_heads, heads_per_group):
    G, H = heads_per_group, n_heads
    W = H * DN_HEAD_DIM
    lane_head = jnp.arange(W) // DN_HEAD_DIM
    rows = jnp.arange(LANES)[:, None]
    eb = (rows == lane_head[None, :]).astype(BF)
    ea = (rows == lane_head[None, :] + H).astype(BF)
    rpg = _group_rows(G)
    r = jnp.arange((H // G) * rpg)
    head_of_row = jnp.where(r % rpg < G, (r // rpg) * G + r % rpg, -LANES)
    sel = (jnp.arange(LANES)[None, :] == head_of_row[:, None] + H).astype(BF)
    at_a = lambda v: jnp.pad(v.astype(F32), (H, LANES - 2 * H))[None, :]
    return eb, ea, sel, at_a(a_log), at_a(dt_bias)


def _shift_rows(t, halo, k):
    r = pltpu.roll(t, k, 0)
    hr = pltpu.roll(halo, k, 0)[:8]
    rows = lax.broadcasted_iota(jnp.int32, (8, t.shape[1]), 0)
    first = jnp.where(rows < k, hr, r[:8])
    return jnp.concatenate([first, r[8:]], axis=0)


def _causal_conv(t, halo, w):
    K = w.shape[0]
    y = t * w[K - 1:K]
    for d in range(1, K):
        y = y + _shift_rows(t, halo, d) * w[K - 1 - d:K - d]
    return y


def _halo_spec(rows_per_tile, width, col_block, n_lead):
    per = rows_per_tile // HALO
    if n_lead == 1:
        return pl.BlockSpec((HALO, width), lambda i: (jnp.maximum(i * per - 1, 0), col_block))
    return pl.BlockSpec((HALO, width), lambda i, g: (jnp.maximum(i * per - 1, 0), col_block(g)))


def _conv_a_kernel(cx_ref, cc_ref, cb_ref, hx_ref, hc_ref, w_ref, o_ref, *, tiles_per_seq):
    first = (pl.program_id(0) % tiles_per_seq == 0)
    keep = jnp.where(first, 0.0, 1.0)
    t = cx_ref[...].astype(F32) * cc_ref[...].astype(F32)
    halo = hx_ref[...].astype(F32) * hc_ref[...].astype(F32) * keep
    y = _causal_conv(t, halo, w_ref[...])
    o_ref[...] = (cb_ref[...].astype(F32) * y).astype(o_ref.dtype)


def _conv_a(p, conv_w, seq, width, tr_pref=256):
    M = p.shape[0]
    tr = _tile(seq, tr_pref)
    blk = lambda c: pl.BlockSpec((tr, width), lambda i: (i, c))
    return pl.pallas_call(
        functools.partial(_conv_a_kernel, tiles_per_seq=seq // tr),
        grid=(M // tr,),
        in_specs=[blk(0), blk(1), blk(2), _halo_spec(tr, width, 0, 1), _halo_spec(tr, width, 1, 1),
                  pl.BlockSpec(conv_w.shape, lambda i: (0, 0))],
        out_specs=pl.BlockSpec((tr, width), lambda i: (i, 0)),
        out_shape=jax.ShapeDtypeStruct((M, width), BF),
        compiler_params=_cparams("parallel"),
        name="conv_a",
    )(p, p, p, p, p, conv_w)


def _dn_prep_kernel(q_ref, k_ref, v_ref, hq_ref, hk_ref, hv_ref, wq_ref, wk_ref, wv_ref,
                    ba_ref, eb_ref, ea_ref, sel_ref, alog_ref, dtb_ref,
                    qeff_ref, wdec_ref, ktail_ref, ubase_ref, obase_ref, glast_ref,
                    *, tiles_per_seq, heads, wave):
    R = q_ref.shape[0]
    C, DK = DN_CHUNK, DN_HEAD_DIM
    first = (pl.program_id(0) % tiles_per_seq == 0)
    keep = jnp.where(first, 0.0, 1.0)
    beta_b, gc_b, gc_t = _gate_terms(ba_ref[...], eb_ref[...], ea_ref[...], sel_ref[...],
                                     alog_ref[...], dtb_ref[...])
    for c in range(R // C):
        glast_ref[c] = jnp.exp(gc_b[c * C + C - 1:c * C + C, :])
    ri = lax.broadcasted_iota(jnp.int32, (R, R), 0)
    ci = lax.broadcasted_iota(jnp.int32, (R, R), 1)
    same = (ri // C) == (ci // C)
    incl = same & (ri >= ci)
    strict = same & (ri > ci)

    def l2n(x):
        return x * lax.rsqrt(jnp.sum(x * x, axis=-1, keepdims=True) + EPS)

    def head_inputs(hh):
        cols = slice(hh * DK, (hh + 1) * DK)

        def conv_silu(t_ref, h_ref, w_ref):
            y = _causal_conv(t_ref[:, cols].astype(F32), h_ref[:, cols].astype(F32) * keep,
                             w_ref[:, cols])
            return _silu(y)

        q = l2n(conv_silu(q_ref, hq_ref, wq_ref)) * (DK ** -0.5)
        k = l2n(conv_silu(k_ref, hk_ref, wk_ref))
        v = conv_silu(v_ref, hv_ref, wv_ref)
        beta = beta_b[:, cols]
        gc = gc_b[:, cols]
        g_row = jnp.broadcast_to(gc_t[hh:hh + 1, :], (R, R))
        g_col = jnp.concatenate([gc] * (R // DK), axis=1)
        e = jnp.exp(jnp.where(incl, g_col - g_row, 0.0))
        decay_incl = jnp.where(incl, e, 0.0)
        decay_strict = jnp.where(strict, e, 0.0)
        kb = k * beta
        k16 = k.astype(BF)
        m0 = _dot_nt((-kb).astype(BF), k16) * decay_strict
        aqk = (_dot_nt(q.astype(BF), k16) * decay_incl).astype(BF)
        rhs = jnp.concatenate([v * beta, kb * jnp.exp(gc)], axis=1)
        return dict(cols=cols, q=q, k=k, gc=gc, aqk=aqk, rhs=rhs, m=m0)

    n_fac = C.bit_length() - 1
    waves = [list(range(w0, min(w0 + wave, heads))) for w0 in range(0, heads, wave)]
    hs = [head_inputs(hh) for hh in waves[0]]
    for wi in range(len(waves)):
        todo = list(waves[wi + 1]) if wi + 1 < len(waves) else []
        hs_next = []
        for j in range(n_fac):
            for d in hs:
                m16 = d["m"].astype(BF)
                if j == 0:
                    d["qm"] = d["m"]
                    d["m"] = _dot(m16, m16)
                elif j < n_fac - 1:
                    r = _dot(jnp.concatenate([m16, d["qm"].astype(BF)], axis=0), m16)
                    d["qm"] = d["qm"] + d["m"] + r[R:]
                    d["m"] = r[:R]
                else:
                    d["qm"] = d["qm"] + d["m"] + _dot(d["qm"].astype(BF), m16)
            n_emit = -(-len(todo) // (n_fac - j))
            hs_next += [head_inputs(hh) for hh in todo[:n_emit]]
            todo = todo[n_emit:]
        for d in hs:
            d["sol"] = d["rhs"] + _dot(d["qm"].astype(BF), d["rhs"].astype(BF))
        for d in hs:
            cols, sol, gc, q, k = d["cols"], d["sol"], d["gc"], d["q"], d["k"]
            ubase, wdec = sol[:, :DK], sol[:, DK:]
            x = _dot(d["aqk"], jnp.concatenate([wdec, ubase], axis=1).astype(BF))
            gc_last = jnp.concatenate(
                [jnp.broadcast_to(gc[c * C + C - 1:c * C + C, :], (C, DK)) for c in range(R // C)],
                axis=0)
            qeff_ref[:, cols] = (q * jnp.exp(gc) - x[:, :DK]).astype(qeff_ref.dtype)
            obase_ref[:, cols] = x[:, DK:]
            ubase_ref[:, cols] = ubase
            wdec_ref[:, cols] = wdec.astype(wdec_ref.dtype)
            ktail_ref[:, cols] = (k * jnp.exp(gc_last - gc)).astype(ktail_ref.dtype)
        hs = hs_next


def _dn_prep(p, qkv_conv_w, ba, a_log, dt_bias, seq, n_heads, heads_per_group, q_off):
    M = p.shape[0]
    G = heads_per_group
    GW = G * DN_HEAD_DIM
    W = n_heads * DN_HEAD_DIM
    n_groups = n_heads // G
    R = _tile(seq, 256)
    rpg = _group_rows(G)
    assert q_off % GW == 0 and R % DN_HEAD_DIM == 0
    eb, ea, sel, alog_c, dtb_c = _gate_constants(a_log, dt_bias, n_heads, G)
    cols = [lambda g, o=o: q_off // GW + o * n_groups + g for o in range(3)]
    blk = lambda c: pl.BlockSpec((R, GW), lambda i, g: (i, c(g)))
    wblk = lambda o: pl.BlockSpec((qkv_conv_w.shape[0], GW), lambda i, g: (0, o * n_groups + g))
    head = pl.BlockSpec((R, GW), lambda i, g: (i, g))
    lanes = pl.BlockSpec((1, LANES), lambda i, g: (0, 0))
    n_ch = R // DN_CHUNK
    return pl.pallas_call(
        functools.partial(_dn_prep_kernel, tiles_per_seq=seq // R, heads=G, wave=min(G, DN_WAVE)),
        grid=(M // R, n_groups),
        in_specs=[blk(cols[0]), blk(cols[1]), blk(cols[2]),
                  _halo_spec(R, GW, cols[0], 2), _halo_spec(R, GW, cols[1], 2),
                  _halo_spec(R, GW, cols[2], 2),
                  wblk(0), wblk(1), wblk(2),
                  pl.BlockSpec((R, LANES), lambda i, g: (i, 0)),
                  pl.BlockSpec((LANES, GW), lambda i, g: (0, g)),
                  pl.BlockSpec((LANES, GW), lambda i, g: (0, g)),
                  pl.BlockSpec((rpg, LANES), lambda i, g: (g, 0)), lanes, lanes],
        out_specs=[head] * 5 + [pl.BlockSpec((n_ch, 1, GW), lambda i, g: (i, 0, g))],
        out_shape=[jax.ShapeDtypeStruct((M, W), BF)] * 3 + [jax.ShapeDtypeStruct((M, W), F32)] * 2
        + [jax.ShapeDtypeStruct((M // DN_CHUNK, 1, W), F32)],
        compiler_params=_cparams("parallel", "parallel"),
        name="dn_prep",
    )(p, p, p, p, p, p, qkv_conv_w, qkv_conv_w, qkv_conv_w, ba, eb, ea, sel, alog_c, dtb_c)


def _dn_scan_kernel(qeff_ref, wdec_ref, ktail_ref, ubase_ref, obase_ref, glast_ref, z_ref, nw_ref,
                    o_ref, state_ref, *, heads):
    rows_per_step = qeff_ref.shape[0]
    C, DK = DN_CHUNK, DN_HEAD_DIM

    @pl.when(pl.program_id(2) == 0)
    def _():
        state_ref[...] = jnp.zeros_like(state_ref)

    nw = nw_ref[...]

    def chunk(c, carry):
        r0 = pl.multiple_of(c * C, C)
        rows = pl.ds(r0, C)
        g_last_all = glast_ref[c]
        for h in range(heads):
            cols = slice(h * DK, (h + 1) * DK)
            st = state_ref[h]
            lhs = jnp.concatenate([qeff_ref[rows, cols], wdec_ref[rows, cols]], axis=0)
            r = _dot(lhs, st.astype(BF))
            o = obase_ref[rows, cols] + r[:C]
            u = ubase_ref[rows, cols] - r[C:]
            g_last = g_last_all[:, cols]
            upd = lax.dot_general(ktail_ref[rows, cols], u.astype(BF), (((0,), (0,)), ((), ())),
                                  preferred_element_type=F32)
            state_ref[h] = st * g_last + upd
            zf = z_ref[rows, cols].astype(F32)
            on = o * lax.rsqrt(jnp.mean(o * o, axis=-1, keepdims=True) + EPS) * nw * _silu(zf)
            o_ref[rows, cols] = on.astype(o_ref.dtype)
        return carry

    lax.fori_loop(0, rows_per_step // C, chunk, 0)


def _dn_scan(qeff, wdec, ktail, ubase, obase, glast, p, dn_out_norm, seq, n_heads, heads_per_group,
             z_off, ts_pref=512):
    M, W = qeff.shape
    G = heads_per_group
    GW = G * DN_HEAD_DIM
    ts = _tile(seq, ts_pref)
    per = seq // ts
    assert z_off % GW == 0
    zb = z_off // GW
    blk = pl.BlockSpec((ts, GW), lambda b, g, t: (b * per + t, g))
    return pl.pallas_call(
        functools.partial(_dn_scan_kernel, heads=G),
        grid=(M // seq, n_heads // G, per),
        in_specs=[blk] * 5
        + [pl.BlockSpec((ts // DN_CHUNK, 1, GW), lambda b, g, t: (b * per + t, 0, g)),
           pl.BlockSpec((ts, GW), lambda b, g, t: (b * per + t, zb + g)),
           pl.BlockSpec((1, DN_HEAD_DIM), lambda b, g, t: (0, 0))],
        out_specs=blk,
        out_shape=jax.ShapeDtypeStruct((M, W), BF),
        scratch_shapes=[pltpu.VMEM((G, DN_HEAD_DIM, DN_HEAD_DIM), F32)],
        compiler_params=_cparams("parallel", "parallel", "arbitrary"),
        name="dn_scan",
    )(qeff, wdec, ktail, ubase, obase, glast, p, dn_out_norm)


def _merge_kernel(ya_ref, yb_ref, wa_ref, wb_ref, ga_ref, gb_ref, o_ref, wa16_ref, wb16_ref):
    @pl.when(pl.program_id(1) == 0)
    def _():
        wa16_ref[...] = wa_ref[...].astype(BF)
        wb16_ref[...] = wb_ref[...].astype(BF)

    ya = _dot(ya_ref[...], wa16_ref[...])
    yb = _dot(yb_ref[...], wb16_ref[...])
    m = jax.nn.sigmoid(ga_ref[...].astype(F32)) * ya + jax.nn.sigmoid(gb_ref[...].astype(F32)) * yb
    o_ref[...] = m.astype(o_ref.dtype)


def _merge(ya, yb, wa, wb, layer, p, gate_off, tm_pref=1024, tn_pref=512):
    M, KA = ya.shape
    KB = yb.shape[1]
    N = wa.shape[-1]
    tm, tn = _tile(M, tm_pref), _tile(N, tn_pref)
    while gate_off % tn:
        tn //= 2
    nb = N // tn
    g0 = gate_off // tn
    return pl.pallas_call(
        _merge_kernel,
        grid=(nb, M // tm),
        in_specs=[pl.BlockSpec((tm, KA), lambda j, i: (i, 0)),
                  pl.BlockSpec((tm, KB), lambda j, i: (i, 0)),
                  pl.BlockSpec((None, KA, tn), lambda j, i: (layer, 0, j)),
                  pl.BlockSpec((None, KB, tn), lambda j, i: (layer, 0, j)),
                  pl.BlockSpec((tm, tn), lambda j, i: (i, g0 + j)),
                  pl.BlockSpec((tm, tn), lambda j, i: (i, g0 + nb + j))],
        out_specs=pl.BlockSpec((tm, tn), lambda j, i: (i, j)),
        out_shape=jax.ShapeDtypeStruct((M, N), BF),
        scratch_shapes=[pltpu.VMEM((KA, tn), BF), pltpu.VMEM((KB, tn), BF)],
        compiler_params=_cparams("parallel", "arbitrary"),
        name="merge",
    )(ya, yb, wa, wb, p, p)


def _mem_kv_kernel(m_ref, nw_ref, wk_ref, wv_ref, k_ref, v_ref):
    mn = _rms(m_ref[...], nw_ref[...]).astype(BF)
    k_ref[...] = _dot(mn, wk_ref[...].astype(BF)).astype(k_ref.dtype)
    v_ref[...] = _dot(mn, wv_ref[...].astype(BF)).astype(v_ref.dtype)


def _mem_kv(mem, nw, wk, wv, tr_pref=256):
    Mm, D = mem.shape
    N = wk.shape[1]
    tr = _tile(Mm, tr_pref)
    full = lambda shape: pl.BlockSpec(shape, lambda i: (0, 0))
    return pl.pallas_call(
        _mem_kv_kernel,
        grid=(Mm // tr,),
        in_specs=[pl.BlockSpec((tr, D), lambda i: (i, 0)), full((1, D)), full((D, N)), full((D, N))],
        out_specs=[pl.BlockSpec((tr, N), lambda i: (i, 0))] * 2,
        out_shape=[jax.ShapeDtypeStruct((Mm, N), BF)] * 2,
        compiler_params=_cparams("parallel"),
        name="mem_kv",
    )(mem, nw, wk, wv)


def _xattn_kernel(h_ref, nw_ref, wq_ref, k_ref, v_ref, wo_ref, nw2_ref, o_ref, on_ref, *, n_heads):
    DH = XATTN_HEAD_DIM
    h = h_ref[...]
    q = _dot(_rms(h, nw_ref[...]).astype(BF), wq_ref[...])
    outs = []
    for hd in range(n_heads):
        cols = slice(hd * DH, (hd + 1) * DH)
        s = _dot_nt(q[:, cols].astype(BF), k_ref[:, cols]) * (DH ** -0.5)
        s = s - jnp.max(s, axis=-1, keepdims=True)
        e = jnp.exp(s)
        pr = e / jnp.sum(e, axis=-1, keepdims=True)
        outs.append(_dot(pr.astype(BF), v_ref[:, cols]))
    o = jnp.concatenate(outs, axis=1).astype(BF)
    h_out = h + _dot(o, wo_ref[...])
    o_ref[...] = h_out
    on_ref[...] = _rms(h_out, nw2_ref[...]).astype(on_ref.dtype)


def _xattn(h, nw, wq, kh, vh, wo, nw2, seq, n_mem, tm_pref=512):
    M, D = h.shape
    N = wq.shape[1]
    tm = _tile(seq, tm_pref)
    per = seq // tm
    full = lambda shape: pl.BlockSpec(shape, lambda i: (0, 0))
    kv = pl.BlockSpec((n_mem, N), lambda i: (i // per, 0))
    row = pl.BlockSpec((tm, D), lambda i: (i, 0))
    return pl.pallas_call(
        functools.partial(_xattn_kernel, n_heads=N // XATTN_HEAD_DIM),
        grid=(M // tm,),
        in_specs=[row, full((1, D)), full((D, N)), kv, kv, full((N, D)), full((1, D))],
        out_specs=[row, row],
        out_shape=[jax.ShapeDtypeStruct((M, D), F32), jax.ShapeDtypeStruct((M, D), BF)],
        compiler_params=_cparams("parallel"),
        name="xattn",
    )(h, nw, wq, kh, vh, wo, nw2)


def kernel(x, mem, ffn1_norm, ffn1_w_gate, ffn1_w_up, ffn1_w_down, mix_norm, w_in, conv_w, qkv_conv_w, a_log, dt_bias, dn_out_norm, w_out_conv, w_out_delta, w_o, xattn_norm, mem_norm, xattn_wq, xattn_wk, xattn_wv, xattn_wo, ffn2_norm, ffn2_w_gate, ffn2_w_up, ffn2_w_down, final_norm):
    B, S, D = x.shape
    depth = ffn1_norm.shape[0]
    n_mem = mem.shape[1]
    CW = conv_w.shape[-1]
    H = a_log.shape[-1]
    DW = H * DN_HEAD_DIM
    G = min(H, DN_HEADS_PER_STEP)
    M = B * S
    assert S % DN_CHUNK == 0 and 2 * H <= LANES and H % G == 0
    o_qkv = 3 * CW
    o_z = o_qkv + 3 * DW
    o_b = o_z + DW
    o_ga = o_b + 2 * H
    assert w_in.shape[-1] == o_ga + 2 * D
    vec = lambda v: v.reshape(1, -1).astype(F32)
    bf = lambda w: w.astype(BF)

    h = x.reshape(M, D)
    mem2 = mem.reshape(B * n_mem, D)
    for l in range(depth):
        h = _ffn(_norm(h, vec(ffn1_norm[l]))[0], h, ffn1_w_gate, ffn1_w_up, ffn1_w_down,
                 vec(mix_norm[l]), l, norm_out=False, name="ffn1")

        w_in_t = jnp.swapaxes(w_in, 1, 2)
        w_ba = bf(jnp.pad(w_in_t[l, o_b:o_ga], ((0, LANES - 2 * H), (0, 0))))
        un, ba = _norm(h, vec(mix_norm[l]), w_ba)
        p = _matmul_wst(un, w_in_t, l, o_b, o_ga - o_b, o_b + 2 * D, BF, name="in_proj")

        ya = _conv_a(p, conv_w[l].astype(F32), S, CW)
        qeff, wdec, ktail, ubase, obase, glast = _dn_prep(
            p, qkv_conv_w[l].astype(F32), ba, a_log[l], dt_bias[l], S, H, G, o_qkv)
        yb = _dn_scan(qeff, wdec, ktail, ubase, obase, glast, p, vec(dn_out_norm[l]), S, H, G, o_z)

        merged = _merge(ya, yb, w_out_conv, w_out_delta, l, p, o_b)
        h = _matmul_ws(merged, w_o, l, h, name="oproj")

        kh, vh = _mem_kv(mem2, vec(mem_norm[l]), xattn_wk[l], xattn_wv[l])
        h, xn = _xattn(h, vec(xattn_norm[l]), bf(xattn_wq[l]), kh, vh, bf(xattn_wo[l]),
                       vec(ffn2_norm[l]), S, n_mem)

        h = _ffn(xn, h, ffn2_w_gate, ffn2_w_up, ffn2_w_down, vec(final_norm), l,
                 norm_out=(l == depth - 1), name="ffn2")
    return h.reshape(B, S, D)
```

```python
import functools

import jax
import jax.numpy as jnp
from jax import lax
from jax.experimental import pallas as pl
from jax.experimental.pallas import tpu as pltpu

EPS = 1e-6
BF = jnp.bfloat16
F32 = jnp.float32

LANES = 128
DN_HEAD_DIM = 128
DN_CHUNK = 64
DN_HEADS_PER_STEP = 16
DN_WAVE = 4
XATTN_HEAD_DIM = 128
HALO = 16
SUBLANES = 8
VMEM_BYTES_V7X = 64 * 1024 * 1024
VMEM_LIMIT = VMEM_BYTES_V7X - 4 * 1024 * 1024


def _cparams(*sem):
    return pltpu.CompilerParams(dimension_semantics=sem, vmem_limit_bytes=VMEM_LIMIT)


def _tile(dim, pref):
    t = min(pref, dim)
    while dim % t:
        t //= 2
    return t


def _rms(x, w):
    return x * lax.rsqrt(jnp.mean(x * x, axis=-1, keepdims=True) + EPS) * w


def _silu(x):
    return x * jax.nn.sigmoid(x)


def _dot(a, b):
    return jnp.dot(a, b, preferred_element_type=F32)


def _dot_nt(a, b):
    return lax.dot_general(a, b, (((1,), (1,)), ((), ())), preferred_element_type=F32)


def _ffn_kernel(xn_ref, x_hbm, wg_ref, wu_ref, wd_ref, ow_ref, o_hbm, acc_ref, sem_in, sem_out,
                *, n_i, n_f, tm, col_chunk, row_chunk, norm_out):
    i = pl.program_id(0)
    f = pl.program_id(1)
    D = acc_ref.shape[1]
    n_cc, n_rc = D // col_chunk, tm // row_chunk
    n_pieces = n_rc if norm_out else n_cc

    def rows(t):
        return pl.ds(pl.multiple_of(t * tm, tm), tm)

    def load(t):
        return pltpu.make_async_copy(x_hbm.at[rows(t)], acc_ref, sem_in)

    def store(t, k):
        if norm_out:
            src = acc_ref.at[pl.ds(k * row_chunk, row_chunk)]
            dst = o_hbm.at[pl.ds(pl.multiple_of(t * tm + k * row_chunk, row_chunk), row_chunk)]
        else:
            src = acc_ref.at[:, pl.ds(k * col_chunk, col_chunk)]
            dst = o_hbm.at[rows(t), pl.ds(k * col_chunk, col_chunk)]
        return pltpu.make_async_copy(src, dst, sem_out.at[k])

    @pl.when(f == 0)
    def _():
        @pl.when(i > 0)
        def _():
            for k in range(n_pieces):
                store(i - 1, k).wait()

        load(i).start()

    xn = xn_ref[...]
    g = _dot(xn, wg_ref[...].astype(BF))
    u = _dot(xn, wu_ref[...].astype(BF))
    a = (0.5 * _silu(g) * u).astype(BF)

    @pl.when(f == 0)
    def _():
        load(i).wait()

    def down(last):
        for c in range(n_cc):
            cs = slice(c * col_chunk, (c + 1) * col_chunk)
            acc_ref[:, cs] += _dot(a, wd_ref[:, cs].astype(BF))
            if last and not norm_out:
                store(i, c).start()
        if last and norm_out:
            for r in range(n_rc):
                rs = slice(r * row_chunk, (r + 1) * row_chunk)
                acc_ref[rs, :] = _rms(acc_ref[rs, :], ow_ref[...])
                store(i, r).start()
        if last:
            @pl.when(i == n_i - 1)
            def _():
                for k in range(n_pieces):
                    store(i, k).wait()

    @pl.when(f < n_f - 1)
    def _():
        down(False)

    @pl.when(f == n_f - 1)
    def _():
        down(True)


def _ffn(xn, x, wg, wu, wd, ow, layer, *, norm_out, tm_pref=1024, tf_pref=256, name="ffn"):
    M, D = x.shape
    F = wg.shape[-1]
    tm, tf = _tile(M, tm_pref), _tile(F, tf_pref)
    n_i, n_f = M // tm, F // tf
    col_chunk, row_chunk = _tile(D, 2048), _tile(tm, 256)
    n_pieces = tm // row_chunk if norm_out else D // col_chunk
    return pl.pallas_call(
        functools.partial(_ffn_kernel, n_i=n_i, n_f=n_f, tm=tm, col_chunk=col_chunk,
                          row_chunk=row_chunk, norm_out=norm_out),
        grid=(n_i, n_f),
        in_specs=[
            pl.BlockSpec((tm, D), lambda i, f: (i, 0), pipeline_mode=pl.Buffered(1)),
            pl.BlockSpec(memory_space=pl.ANY),
            pl.BlockSpec((None, D, tf), lambda i, f: (layer, 0, f)),
            pl.BlockSpec((None, D, tf), lambda i, f: (layer, 0, f)),
            pl.BlockSpec((None, tf, D), lambda i, f: (layer, f, 0)),
            pl.BlockSpec((1, D), lambda i, f: (0, 0)),
        ],
        out_specs=pl.BlockSpec(memory_space=pl.ANY),
        out_shape=jax.ShapeDtypeStruct((M, D), F32),
        scratch_shapes=[pltpu.VMEM((tm, D), F32), pltpu.SemaphoreType.DMA(()),
                        pltpu.SemaphoreType.DMA((n_pieces,))],
        compiler_params=_cparams("arbitrary", "arbitrary"),
        name=name,
    )(xn, x, wg, wu, wd, ow)


def _norm_kernel(x_ref, w_ref, *refs, with_proj):
    xn = _rms(x_ref[...], w_ref[...]).astype(BF)
    if with_proj:
        wp_ref, o_ref, p_ref = refs
        p_ref[...] = _dot_nt(xn, wp_ref[...])
    else:
        (o_ref,) = refs
    o_ref[...] = xn


def _norm(x, w, w_proj=None, tr_pref=512):
    M, D = x.shape
    tr = _tile(M, tr_pref)
    in_specs = [pl.BlockSpec((tr, D), lambda i: (i, 0)), pl.BlockSpec((1, D), lambda i: (0, 0))]
    out_specs = [pl.BlockSpec((tr, D), lambda i: (i, 0))]
    out_shape = [jax.ShapeDtypeStruct((M, D), BF)]
    args = [x, w]
    if w_proj is not None:
        N = w_proj.shape[0]
        in_specs.append(pl.BlockSpec((N, D), lambda i: (0, 0)))
        out_specs.append(pl.BlockSpec((tr, N), lambda i: (i, 0)))
        out_shape.append(jax.ShapeDtypeStruct((M, N), F32))
        args.append(w_proj)
    return pl.pallas_call(
        functools.partial(_norm_kernel, with_proj=w_proj is not None),
        grid=(M // tr,),
        in_specs=in_specs,
        out_specs=out_specs,
        out_shape=out_shape,
        compiler_params=_cparams("parallel"),
        name="norm_proj" if w_proj is not None else "norm",
    )(*args)


def _mm_ws_kernel(a_ref, w_ref, res_ref, o_ref, wbf_ref):
    @pl.when(pl.program_id(1) == 0)
    def _():
        wbf_ref[...] = w_ref[...].astype(BF)

    o_ref[...] = res_ref[...] + _dot(a_ref[...], wbf_ref[...])


def _matmul_ws(a, w, layer, res, *, tm_pref=1024, tn_pref=512, name="matmul_ws"):
    M, K = a.shape
    N = w.shape[-1]
    tm, tn = _tile(M, tm_pref), _tile(N, tn_pref)
    out_blk = pl.BlockSpec((tm, tn), lambda j, i: (i, j))
    return pl.pallas_call(
        _mm_ws_kernel,
        grid=(N // tn, M // tm),
        in_specs=[pl.BlockSpec((tm, K), lambda j, i: (i, 0)),
                  pl.BlockSpec((None, K, tn), lambda j, i: (layer, 0, j)),
                  out_blk],
        out_specs=out_blk,
        out_shape=jax.ShapeDtypeStruct((M, N), res.dtype),
        scratch_shapes=[pltpu.VMEM((K, tn), BF)],
        compiler_params=_cparams("parallel", "arbitrary"),
        name=name,
    )(a, w, res)


def _mm_wst_kernel(a_ref, wt_hbm, o_ref, wf32_ref, wbf_ref, sem, *, layer, tn, n_j, n_head, skip):
    j = pl.program_id(0)

    def fetch(t):
        start = pl.multiple_of(t * tn + jnp.where(t >= n_head, skip, 0), SUBLANES)
        return pltpu.make_async_copy(wt_hbm.at[layer, pl.ds(start, tn)], wf32_ref.at[t % 2],
                                     sem.at[t % 2])

    @pl.when(pl.program_id(1) == 0)
    def _():
        @pl.when(j == 0)
        def _():
            fetch(j).start()

        fetch(j).wait()
        wbf_ref[...] = wf32_ref[j % 2].astype(BF)

        @pl.when(j + 1 < n_j)
        def _():
            fetch(j + 1).start()

    o_ref[...] = _dot_nt(a_ref[...], wbf_ref[...]).astype(o_ref.dtype)


def _matmul_wst(a, wt, layer, n_head, skip, n_out, out_dtype, *, tm_pref=1024, tn_pref=512,
                name="matmul_wst"):
    M, K = a.shape
    tm, tn = _tile(M, tm_pref), _tile(n_out, tn_pref)
    assert n_head % tn == 0 and skip % SUBLANES == 0
    n_j = n_out // tn
    return pl.pallas_call(
        functools.partial(_mm_wst_kernel, layer=layer, tn=tn, n_j=n_j, n_head=n_head // tn, skip=skip),
        grid=(n_j, M // tm),
        in_specs=[pl.BlockSpec((tm, K), lambda j, i: (i, 0)),
                  pl.BlockSpec(memory_space=pl.ANY)],
        out_specs=pl.BlockSpec((tm, tn), lambda j, i: (i, j)),
        out_shape=jax.ShapeDtypeStruct((M, n_out), out_dtype),
        scratch_shapes=[pltpu.VMEM((2, tn, K), F32), pltpu.VMEM((tn, K), BF),
                        pltpu.SemaphoreType.DMA((2,))],
        compiler_params=_cparams("arbitrary", "arbitrary"),
        name=name,
    )(a, wt)


def _split3(x):
    hi = x.astype(BF)
    r1 = x - hi.astype(F32)
    mid = r1.astype(BF)
    lo = (r1 - mid.astype(F32)).astype(BF)
    return hi, mid, lo


def _gate_terms(ba, eb, ea, sel, alog, dtb):
    R = ba.shape[0]
    beta = jax.nn.sigmoid(ba)
    x = ba + dtb
    softplus = jnp.maximum(x, 0.0) + jnp.log1p(jnp.exp(-jnp.abs(x)))
    g = -jnp.exp(alog) * softplus
    ri = lax.broadcasted_iota(jnp.int32, (R, R), 0)
    ci = lax.broadcasted_iota(jnp.int32, (R, R), 1)
    tri = ((ri // DN_CHUNK == ci // DN_CHUNK) & (ri >= ci)).astype(BF)
    gc = sum(_dot(tri, piece) for piece in _split3(g))
    gc_parts = _split3(gc)
    beta_b = sum(_dot(piece, eb) for piece in _split3(beta))
    gc_b = sum(_dot(piece, ea) for piece in gc_parts)
    gc_t = sum(_dot_nt(sel, piece) for piece in gc_parts)
    return beta_b, gc_b, gc_t


def _group_rows(heads_per_group):
    return -(-heads_per_group // SUBLANES) * SUBLANES


def _gate_constants(a_log, dt_bias, n_heads, heads_per_group):
    G, H = heads_per_group, n_heads
    W = H * DN_HEAD_DIM
    lane_head = jnp.arange(W) // DN_HEAD_DIM
    rows = jnp.arange(LANES)[:, None]
    eb = (rows == lane_head[None, :]).astype(BF)
    ea = (rows == lane_head[None, :] + H).astype(BF)
    rpg = _group_rows(G)
    r = jnp.arange((H // G) * rpg)
    head_of_row = jnp.where(r % rpg < G, (r // rpg) * G + r % rpg, -LANES)
    sel = (jnp.arange(LANES)[None, :] == head_of_row[:, None] + H).astype(BF)
    at_a = lambda v: jnp.pad(v.astype(F32), (H, LANES - 2 * H))[None, :]
    return eb, ea, sel, at_a(a_log), at_a(dt_bias)


def _shift_rows(t, halo, k):
    r = pltpu.roll(t, k, 0)
    hr = pltpu.roll(halo, k, 0)[:8]
    rows = lax.broadcasted_iota(jnp.int32, (8, t.shape[1]), 0)
    first = jnp.where(rows < k, hr, r[:8])
    return jnp.concatenate([first, r[8:]], axis=0)


def _causal_conv(t, halo, w):
    K = w.shape[0]
    y = t * w[K - 1:K]
    for d in range(1, K):
        y = y + _shift_rows(t, halo, d) * w[K - 1 - d:K - d]
    return y


def _halo_spec(rows_per_tile, width, col_block, n_lead):
    per = rows_per_tile // HALO
    if n_lead == 1:
        return pl.BlockSpec((HALO, width), lambda i: (jnp.maximum(i * per - 1, 0), col_block))
    return pl.BlockSpec((HALO, width), lambda i, g: (jnp.maximum(i * per - 1, 0), col_block(g)))


def _conv_a_kernel(cx_ref, cc_ref, cb_ref, hx_ref, hc_ref, w_ref, o_ref, *, tiles_per_seq):
    first = (pl.program_id(0) % tiles_per_seq == 0)
    keep = jnp.where(first, 0.0, 1.0)
    t = cx_ref[...].astype(F32) * cc_ref[...].astype(F32)
    halo = hx_ref[...].astype(F32) * hc_ref[...].astype(F32) * keep
    y = _causal_conv(t, halo, w_ref[...])
    o_ref[...] = (cb_ref[...].astype(F32) * y).astype(o_ref.dtype)


def _conv_a(p, conv_w, seq, width, tr_pref=256):
    M = p.shape[0]
    tr = _tile(seq, tr_pref)
    blk = lambda c: pl.BlockSpec((tr, width), lambda i: (i, c))
    return pl.pallas_call(
        functools.partial(_conv_a_kernel, tiles_per_seq=seq // tr),
        grid=(M // tr,),
        in_specs=[blk(0), blk(1), blk(2), _halo_spec(tr, width, 0, 1), _halo_spec(tr, width, 1, 1),
                  pl.BlockSpec(conv_w.shape, lambda i: (0, 0))],
        out_specs=pl.BlockSpec((tr, width), lambda i: (i, 0)),
        out_shape=jax.ShapeDtypeStruct((M, width), BF),
        compiler_params=_cparams("parallel"),
        name="conv_a",
    )(p, p, p, p, p, conv_w)


def _dn_prep_kernel(q_ref, k_ref, v_ref, hq_ref, hk_ref, hv_ref, wq_ref, wk_ref, wv_ref,
                    ba_ref, eb_ref, ea_ref, sel_ref, alog_ref, dtb_ref,
                    qeff_ref, wdec_ref, ktail_ref, ubase_ref, obase_ref, glast_ref,
                    *, tiles_per_seq, heads, wave):
    R = q_ref.shape[0]
    C, DK = DN_CHUNK, DN_HEAD_DIM
    first = (pl.program_id(0) % tiles_per_seq == 0)
    keep = jnp.where(first, 0.0, 1.0)
    beta_b, gc_b, gc_t = _gate_terms(ba_ref[...], eb_ref[...], ea_ref[...], sel_ref[...],
                                     alog_ref[...], dtb_ref[...])
    for c in range(R // C):
        glast_ref[c] = jnp.exp(gc_b[c * C + C - 1:c * C + C, :])
    ri = lax.broadcasted_iota(jnp.int32, (R, R), 0)
    ci = lax.broadcasted_iota(jnp.int32, (R, R), 1)
    same = (ri // C) == (ci // C)
    incl = same & (ri >= ci)
    strict = same & (ri > ci)

    def l2n(x):
        return x * lax.rsqrt(jnp.sum(x * x, axis=-1, keepdims=True) + EPS)

    def head_inputs(hh):
        cols = slice(hh * DK, (hh + 1) * DK)

        def conv_silu(t_ref, h_ref, w_ref):
            y = _causal_conv(t_ref[:, cols].astype(F32), h_ref[:, cols].astype(F32) * keep,
                             w_ref[:, cols])
            return _silu(y)

        q = l2n(conv_silu(q_ref, hq_ref, wq_ref)) * (DK ** -0.5)
        k = l2n(conv_silu(k_ref, hk_ref, wk_ref))
        v = conv_silu(v_ref, hv_ref, wv_ref)
        beta = beta_b[:, cols]
        gc = gc_b[:, cols]
        g_row = jnp.broadcast_to(gc_t[hh:hh + 1, :], (R, R))
        g_col = jnp.concatenate([gc] * (R // DK), axis=1)
        e = jnp.exp(jnp.where(incl, g_col - g_row, 0.0))
        decay_incl = jnp.where(incl, e, 0.0)
        decay_strict = jnp.where(strict, e, 0.0)
        kb = k * beta
        k16 = k.astype(BF)
        m0 = _dot_nt((-kb).astype(BF), k16) * decay_strict
        aqk = (_dot_nt(q.astype(BF), k16) * decay_incl).astype(BF)
        rhs = jnp.concatenate([v * beta, kb * jnp.exp(gc)], axis=1)
        return dict(cols=cols, q=q, k=k, gc=gc, aqk=aqk, rhs=rhs, m=m0)

    n_fac = C.bit_length() - 1
    waves = [list(range(w0, min(w0 + wave, heads))) for w0 in range(0, heads, wave)]
    hs = [head_inputs(hh) for hh in waves[0]]
    for wi in range(len(waves)):
        todo = list(waves[wi + 1]) if wi + 1 < len(waves) else []
        hs_next = []
        for j in range(n_fac):
            for d in hs:
                m16 = d["m"].astype(BF)
                if j == 0:
                    d["qm"] = d["m"]
                    d["m"] = _dot(m16, m16)
                elif j < n_fac - 1:
                    r = _dot(jnp.concatenate([m16, d["qm"].astype(BF)], axis=0), m16)
                    d["qm"] = d["qm"] + d["m"] + r[R:]
                    d["m"] = r[:R]
                else:
                    d["qm"] = d["qm"] + d["m"] + _dot(d["qm"].astype(BF), m16)
            n_emit = -(-len(todo) // (n_fac - j))
            hs_next += [head_inputs(hh) for hh in todo[:n_emit]]
            todo = todo[n_emit:]
        for d in hs:
            d["sol"] = d["rhs"] + _dot(d["qm"].astype(BF), d["rhs"].astype(BF))
        for d in hs:
            cols, sol, gc, q, k = d["cols"], d["sol"], d["gc"], d["q"], d["k"]
            ubase, wdec = sol[:, :DK], sol[:, DK:]
            x = _dot(d["aqk"], jnp.concatenate([wdec, ubase], axis=1).astype(BF))
            gc_last = jnp.concatenate(
                [jnp.broadcast_to(gc[c * C + C - 1:c * C + C, :], (C, DK)) for c in range(R // C)],
                axis=0)
            qeff_ref[:, cols] = (q * jnp.exp(gc) - x[:, :DK]).astype(qeff_ref.dtype)
            obase_ref[:, cols] = x[:, DK:]
            ubase_ref[:, cols] = ubase
            wdec_ref[:, cols] = wdec.astype(wdec_ref.dtype)
            ktail_ref[:, cols] = (k * jnp.exp(gc_last - gc)).astype(ktail_ref.dtype)
        hs = hs_next


def _dn_prep(p, qkv_conv_w, ba, a_log, dt_bias, seq, n_heads, heads_per_group, q_off):
    M = p.shape[0]
    G = heads_per_group
    GW = G * DN_HEAD_DIM
    W = n_heads * DN_HEAD_DIM
    n_groups = n_heads // G
    R = _tile(seq, 256)
    rpg = _group_rows(G)
    assert q_off % GW == 0 and R % DN_HEAD_DIM == 0
    eb, ea, sel, alog_c, dtb_c = _gate_constants(a_log, dt_bias, n_heads, G)
    cols = [lambda g, o=o: q_off // GW + o * n_groups + g for o in range(3)]
    blk = lambda c: pl.BlockSpec((R, GW), lambda i, g: (i, c(g)))
    wblk = lambda o: pl.BlockSpec((qkv_conv_w.shape[0], GW), lambda i, g: (0, o * n_groups + g))
    head = pl.BlockSpec((R, GW), lambda i, g: (i, g))
    lanes = pl.BlockSpec((1, LANES), lambda i, g: (0, 0))
    n_ch = R // DN_CHUNK
    return pl.pallas_call(
        functools.partial(_dn_prep_kernel, tiles_per_seq=seq // R, heads=G, wave=min(G, DN_WAVE)),
        grid=(M // R, n_groups),
        in_specs=[blk(cols[0]), blk(cols[1]), blk(cols[2]),
                  _halo_spec(R, GW, cols[0], 2), _halo_spec(R, GW, cols[1], 2),
                  _halo_spec(R, GW, cols[2], 2),
                  wblk(0), wblk(1), wblk(2),
                  pl.BlockSpec((R, LANES), lambda i, g: (i, 0)),
                  pl.BlockSpec((LANES, GW), lambda i, g: (0, g)),
                  pl.BlockSpec((LANES, GW), lambda i, g: (0, g)),
                  pl.BlockSpec((rpg, LANES), lambda i, g: (g, 0)), lanes, lanes],
        out_specs=[head] * 5 + [pl.BlockSpec((n_ch, 1, GW), lambda i, g: (i, 0, g))],
        out_shape=[jax.ShapeDtypeStruct((M, W), BF)] * 3 + [jax.ShapeDtypeStruct((M, W), F32)] * 2
        + [jax.ShapeDtypeStruct((M // DN_CHUNK, 1, W), F32)],
        compiler_params=_cparams("parallel", "parallel"),
        name="dn_prep",
    )(p, p, p, p, p, p, qkv_conv_w, qkv_conv_w, qkv_conv_w, ba, eb, ea, sel, alog_c, dtb_c)


def _dn_scan_kernel(qeff_ref, wdec_ref, ktail_ref, ubase_ref, obase_ref, glast_ref, z_ref, nw_ref,
                    o_ref, state_ref, *, heads):
    rows_per_step = qeff_ref.shape[0]
    C, DK = DN_CHUNK, DN_HEAD_DIM

    @pl.when(pl.program_id(2) == 0)
    def _():
        state_ref[...] = jnp.zeros_like(state_ref)

    nw = nw_ref[...]

    def chunk(c, carry):
        r0 = pl.multiple_of(c * C, C)
        rows = pl.ds(r0, C)
        g_last_all = glast_ref[c]
        for h in range(heads):
            cols = slice(h * DK, (h + 1) * DK)
            st = state_ref[h]
            lhs = jnp.concatenate([qeff_ref[rows, cols], wdec_ref[rows, cols]], axis=0)
            r = _dot(lhs, st.astype(BF))
            o = obase_ref[rows, cols] + r[:C]
            u = ubase_ref[rows, cols] - r[C:]
            g_last = g_last_all[:, cols]
            upd = lax.dot_general(ktail_ref[rows, cols], u.astype(BF), (((0,), (0,)), ((), ())),
                                  preferred_element_type=F32)
            state_ref[h] = st * g_last + upd
            zf = z_ref[rows, cols].astype(F32)
            on = o * lax.rsqrt(jnp.mean(o * o, axis=-1, keepdims=True) + EPS) * nw * _silu(zf)
            o_ref[rows, cols] = on.astype(o_ref.dtype)
        return carry

    lax.fori_loop(0, rows_per_step // C, chunk, 0)


def _dn_scan(qeff, wdec, ktail, ubase, obase, glast, p, dn_out_norm, seq, n_heads, heads_per_group,
             z_off, ts_pref=512):
    M, W = qeff.shape
    G = heads_per_group
    GW = G * DN_HEAD_DIM
    ts = _tile(seq, ts_pref)
    per = seq // ts
    assert z_off % GW == 0
    zb = z_off // GW
    blk = pl.BlockSpec((ts, GW), lambda b, g, t: (b * per + t, g))
    return pl.pallas_call(
        functools.partial(_dn_scan_kernel, heads=G),
        grid=(M // seq, n_heads // G, per),
        in_specs=[blk] * 5
        + [pl.BlockSpec((ts // DN_CHUNK, 1, GW), lambda b, g, t: (b * per + t, 0, g)),
           pl.BlockSpec((ts, GW), lambda b, g, t: (b * per + t, zb + g)),
           pl.BlockSpec((1, DN_HEAD_DIM), lambda b, g, t: (0, 0))],
        out_specs=blk,
        out_shape=jax.ShapeDtypeStruct((M, W), BF),
        scratch_shapes=[pltpu.VMEM((G, DN_HEAD_DIM, DN_HEAD_DIM), F32)],
        compiler_params=_cparams("parallel", "parallel", "arbitrary"),
        name="dn_scan",
    )(qeff, wdec, ktail, ubase, obase, glast, p, dn_out_norm)


def _merge_kernel(ya_ref, yb_ref, wa_ref, wb_ref, ga_ref, gb_ref, o_ref, wa16_ref, wb16_ref):
    @pl.when(pl.program_id(1) == 0)
    def _():
        wa16_ref[...] = wa_ref[...].astype(BF)
        wb16_ref[...] = wb_ref[...].astype(BF)

    ya = _dot(ya_ref[...], wa16_ref[...])
    yb = _dot(yb_ref[...], wb16_ref[...])
    m = jax.nn.sigmoid(ga_ref[...].astype(F32)) * ya + jax.nn.sigmoid(gb_ref[...].astype(F32)) * yb
    o_ref[...] = m.astype(o_ref.dtype)


def _merge(ya, yb, wa, wb, layer, p, gate_off, tm_pref=1024, tn_pref=512):
    M, KA = ya.shape
    KB = yb.shape[1]
    N = wa.shape[-1]
    tm, tn = _tile(M, tm_pref), _tile(N, tn_pref)
    while gate_off % tn:
        tn //= 2
    nb = N // tn
    g0 = gate_off // tn
    return pl.pallas_call(
        _merge_kernel,
        grid=(nb, M // tm),
        in_specs=[pl.BlockSpec((tm, KA), lambda j, i: (i, 0)),
                  pl.BlockSpec((tm, KB), lambda j, i: (i, 0)),
                  pl.BlockSpec((None, KA, tn), lambda j, i: (layer, 0, j)),
                  pl.BlockSpec((None, KB, tn), lambda j, i: (layer, 0, j)),
                  pl.BlockSpec((tm, tn), lambda j, i: (i, g0 + j)),
                  pl.BlockSpec((tm, tn), lambda j, i: (i, g0 + nb + j))],
        out_specs=pl.BlockSpec((tm, tn), lambda j, i: (i, j)),
        out_shape=jax.ShapeDtypeStruct((M, N), BF),
        scratch_shapes=[pltpu.VMEM((KA, tn), BF), pltpu.VMEM((KB, tn), BF)],
        compiler_params=_cparams("parallel", "arbitrary"),
        name="merge",
    )(ya, yb, wa, wb, p, p)


def _mem_kv_kernel(m_ref, nw_ref, wk_ref, wv_ref, k_ref, v_ref):
    mn = _rms(m_ref[...], nw_ref[...]).astype(BF)
    k_ref[...] = _dot(mn, wk_ref[...].astype(BF)).astype(k_ref.dtype)
    v_ref[...] = _dot(mn, wv_ref[...].astype(BF)).astype(v_ref.dtype)


def _mem_kv(mem, nw, wk, wv, tr_pref=256):
    Mm, D = mem.shape
    N = wk.shape[1]
    tr = _tile(Mm, tr_pref)
    full = lambda shape: pl.BlockSpec(shape, lambda i: (0, 0))
    return pl.pallas_call(
        _mem_kv_kernel,
        grid=(Mm // tr,),
        in_specs=[pl.BlockSpec((tr, D), lambda i: (i, 0)), full((1, D)), full((D, N)), full((D, N))],
        out_specs=[pl.BlockSpec((tr, N), lambda i: (i, 0))] * 2,
        out_shape=[jax.ShapeDtypeStruct((Mm, N), BF)] * 2,
        compiler_params=_cparams("parallel"),
        name="mem_kv",
    )(mem, nw, wk, wv)


def _xattn_kernel(h_ref, nw_ref, wq_ref, k_ref, v_ref, wo_ref, nw2_ref, o_ref, on_ref, *, n_heads):
    DH = XATTN_HEAD_DIM
    h = h_ref[...]
    q = _dot(_rms(h, nw_ref[...]).astype(BF), wq_ref[...])
    outs = []
    for hd in range(n_heads):
        cols = slice(hd * DH, (hd + 1) * DH)
        s = _dot_nt(q[:, cols].astype(BF), k_ref[:, cols]) * (DH ** -0.5)
        s = s - jnp.max(s, axis=-1, keepdims=True)
        e = jnp.exp(s)
        pr = e / jnp.sum(e, axis=-1, keepdims=True)
        outs.append(_dot(pr.astype(BF), v_ref[:, cols]))
    o = jnp.concatenate(outs, axis=1).astype(BF)
    h_out = h + _dot(o, wo_ref[...])
    o_ref[...] = h_out
    on_ref[...] = _rms(h_out, nw2_ref[...]).astype(on_ref.dtype)


def _xattn(h, nw, wq, kh, vh, wo, nw2, seq, n_mem, tm_pref=512):
    M, D = h.shape
    N = wq.shape[1]
    tm = _tile(seq, tm_pref)
    per = seq // tm
    full = lambda shape: pl.BlockSpec(shape, lambda i: (0, 0))
    kv = pl.BlockSpec((n_mem, N), lambda i: (i // per, 0))
    row = pl.BlockSpec((tm, D), lambda i: (i, 0))
    return pl.pallas_call(
        functools.partial(_xattn_kernel, n_heads=N // XATTN_HEAD_DIM),
        grid=(M // tm,),
        in_specs=[row, full((1, D)), full((D, N)), kv, kv, full((N, D)), full((1, D))],
        out_specs=[row, row],
        out_shape=[jax.ShapeDtypeStruct((M, D), F32), jax.ShapeDtypeStruct((M, D), BF)],
        compiler_params=_cparams("parallel"),
        name="xattn",
    )(h, nw, wq, kh, vh, wo, nw2)


def kernel(x, mem, ffn1_norm, ffn1_w_gate, ffn1_w_up, ffn1_w_down, mix_norm, w_in, conv_w, qkv_conv_w, a_log, dt_bias, dn_out_norm, w_out_conv, w_out_delta, w_o, xattn_norm, mem_norm, xattn_wq, xattn_wk, xattn_wv, xattn_wo, ffn2_norm, ffn2_w_gate, ffn2_w_up, ffn2_w_down, final_norm):
    B, S, D = x.shape
    depth = ffn1_norm.shape[0]
    n_mem = mem.shape[1]
    CW = conv_w.shape[-1]
    H = a_log.shape[-1]
    DW = H * DN_HEAD_DIM
    G = min(H, DN_HEADS_PER_STEP)
    M = B * S
    assert S % DN_CHUNK == 0 and 2 * H <= LANES and H % G == 0
    o_qkv = 3 * CW
    o_z = o_qkv + 3 * DW
    o_b = o_z + DW
    o_ga = o_b + 2 * H
    assert w_in.shape[-1] == o_ga + 2 * D
    vec = lambda v: v.reshape(1, -1).astype(F32)
    bf = lambda w: w.astype(BF)

    h = x.reshape(M, D)
    mem2 = mem.reshape(B * n_mem, D)
    for l in range(depth):
        h = _ffn(_norm(h, vec(ffn1_norm[l]))[0], h, ffn1_w_gate, ffn1_w_up, ffn1_w_down,
                 vec(mix_norm[l]), l, norm_out=False, name="ffn1")

        w_in_t = jnp.swapaxes(w_in, 1, 2)
        w_ba = bf(jnp.pad(w_in_t[l, o_b:o_ga], ((0, LANES - 2 * H), (0, 0))))
        un, ba = _norm(h, vec(mix_norm[l]), w_ba)
        p = _matmul_wst(un, w_in_t, l, o_b, o_ga - o_b, o_b + 2 * D, BF, name="in_proj")

        ya = _conv_a(p, conv_w[l].astype(F32), S, CW)
        qeff, wdec, ktail, ubase, obase, glast = _dn_prep(
            p, qkv_conv_w[l].astype(F32), ba, a_log[l], dt_bias[l], S, H, G, o_qkv)
        yb = _dn_scan(qeff, wdec, ktail, ubase, obase, glast, p, vec(dn_out_norm[l]), S, H, G, o_z)

        merged = _merge(ya, yb, w_out_conv, w_out_delta, l, p, o_b)
        h = _matmul_ws(merged, w_o, l, h, name="oproj")

        kh, vh = _mem_kv(mem2, vec(mem_norm[l]), xattn_wk[l], xattn_wv[l])
        h, xn = _xattn(h, vec(xattn_norm[l]), bf(xattn_wq[l]), kh, vh, bf(xattn_wo[l]),
                       vec(ffn2_norm[l]), S, n_mem)

        h = _ffn(xn, h, ffn2_w_gate, ffn2_w_up, ffn2_w_down, vec(final_norm), l,
                 norm_out=(l == depth - 1), name="ffn2")
    return h.reshape(B, S, D)
```

```python
import functools

import jax
import jax.numpy as jnp
from jax import lax
from jax.experimental import pallas as pl
from jax.experimental.pallas import tpu as pltpu

EPS = 1e-6
BF = jnp.bfloat16
F32 = jnp.float32

LANES = 128
DN_HEAD_DIM = 128
DN_CHUNK = 64
DN_HEADS_PER_STEP = 16
DN_WAVE = 4
XATTN_HEAD_DIM = 128
HALO = 16
SUBLANES = 8
VMEM_BYTES_V7X = 64 * 1024 * 1024
VMEM_LIMIT = VMEM_BYTES_V7X - 4 * 1024 * 1024


def _cparams(*sem):
    return pltpu.CompilerParams(dimension_semantics=sem, vmem_limit_bytes=VMEM_LIMIT)


def _tile(dim, pref):
    t = min(pref, dim)
    while dim % t:
        t //= 2
    return t


def _rms(x, w):
    return x * lax.rsqrt(jnp.mean(x * x, axis=-1, keepdims=True) + EPS) * w


def _silu(x):
    return x * jax.nn.sigmoid(x)


def _dot(a, b):
    return jnp.dot(a, b, preferred_element_type=F32)


def _dot_nt(a, b):
    return lax.dot_general(a, b, (((1,), (1,)), ((), ())), preferred_element_type=F32)


def _ffn_kernel(xn_ref, x_hbm, wg_ref, wu_ref, wd_ref, ow_ref, o_hbm, acc_ref, sem_in, sem_out,
                *, n_i, n_f, tm, col_chunk, row_chunk, norm_out):
    i = pl.program_id(0)
    f = pl.program_id(1)
    D = acc_ref.shape[1]
    n_cc, n_rc = D // col_chunk, tm // row_chunk
    n_pieces = n_rc if norm_out else n_cc

    def rows(t):
        return pl.ds(pl.multiple_of(t * tm, tm), tm)

    def load(t):
        return pltpu.make_async_copy(x_hbm.at[rows(t)], acc_ref, sem_in)

    def store(t, k):
        if norm_out:
            src = acc_ref.at[pl.ds(k * row_chunk, row_chunk)]
            dst = o_hbm.at[pl.ds(pl.multiple_of(t * tm + k * row_chunk, row_chunk), row_chunk)]
        else:
            src = acc_ref.at[:, pl.ds(k * col_chunk, col_chunk)]
            dst = o_hbm.at[rows(t), pl.ds(k * col_chunk, col_chunk)]
        return pltpu.make_async_copy(src, dst, sem_out.at[k])

    @pl.when(f == 0)
    def _():
        @pl.when(i > 0)
        def _():
            for k in range(n_pieces):
                store(i - 1, k).wait()

        load(i).start()

    xn = xn_ref[...]
    g = _dot(xn, wg_ref[...].astype(BF))
    u = _dot(xn, wu_ref[...].astype(BF))
    a = (0.5 * _silu(g) * u).astype(BF)

    @pl.when(f == 0)
    def _():
        load(i).wait()

    def down(last):
        for c in range(n_cc):
            cs = slice(c * col_chunk, (c + 1) * col_chunk)
            acc_ref[:, cs] += _dot(a, wd_ref[:, cs].astype(BF))
            if last and not norm_out:
                store(i, c).start()
        if last and norm_out:
            for r in range(n_rc):
                rs = slice(r * row_chunk, (r + 1) * row_chunk)
                acc_ref[rs, :] = _rms(acc_ref[rs, :], ow_ref[...])
                store(i, r).start()
        if last:
            @pl.when(i == n_i - 1)
            def _():
                for k in range(n_pieces):
                    store(i, k).wait()

    @pl.when(f < n_f - 1)
    def _():
        down(False)

    @pl.when(f == n_f - 1)
    def _():
        down(True)


def _ffn(xn, x, wg, wu, wd, ow, layer, *, norm_out, tm_pref=1024, tf_pref=256, name="ffn"):
    M, D = x.shape
    F = wg.shape[-1]
    tm, tf = _tile(M, tm_pref), _tile(F, tf_pref)
    n_i, n_f = M // tm, F // tf
    col_chunk, row_chunk = _tile(D, 2048), _tile(tm, 256)
    n_pieces = tm // row_chunk if norm_out else D // col_chunk
    return pl.pallas_call(
        functools.partial(_ffn_kernel, n_i=n_i, n_f=n_f, tm=tm, col_chunk=col_chunk,
                          row_chunk=row_chunk, norm_out=norm_out),
        grid=(n_i, n_f),
        in_specs=[
            pl.BlockSpec((tm, D), lambda i, f: (i, 0), pipeline_mode=pl.Buffered(1)),
            pl.BlockSpec(memory_space=pl.ANY),
            pl.BlockSpec((None, D, tf), lambda i, f: (layer, 0, f)),
            pl.BlockSpec((None, D, tf), lambda i, f: (layer, 0, f)),
            pl.BlockSpec((None, tf, D), lambda i, f: (layer, f, 0)),
            pl.BlockSpec((1, D), lambda i, f: (0, 0)),
        ],
        out_specs=pl.BlockSpec(memory_space=pl.ANY),
        out_shape=jax.ShapeDtypeStruct((M, D), F32),
        scratch_shapes=[pltpu.VMEM((tm, D), F32), pltpu.SemaphoreType.DMA(()),
                        pltpu.SemaphoreType.DMA((n_pieces,))],
        compiler_params=_cparams("arbitrary", "arbitrary"),
        name=name,
    )(xn, x, wg, wu, wd, ow)


def _norm_kernel(x_ref, w_ref, *refs, with_proj):
    xn = _rms(x_ref[...], w_ref[...]).astype(BF)
    if with_proj:
        wp_ref, o_ref, p_ref = refs
        p_ref[...] = _dot_nt(xn, wp_ref[...])
    else:
        (o_ref,) = refs
    o_ref[...] = xn


def _norm(x, w, w_proj=None, tr_pref=512):
    M, D = x.shape
    tr = _tile(M, tr_pref)
    in_specs = [pl.BlockSpec((tr, D), lambda i: (i, 0)), pl.BlockSpec((1, D), lambda i: (0, 0))]
    out_specs = [pl.BlockSpec((tr, D), lambda i: (i, 0))]
    out_shape = [jax.ShapeDtypeStruct((M, D), BF)]
    args = [x, w]
    if w_proj is not None:
        N = w_proj.shape[0]
        in_specs.append(pl.BlockSpec((N, D), lambda i: (0, 0)))
        out_specs.append(pl.BlockSpec((tr, N), lambda i: (i, 0)))
        out_shape.append(jax.ShapeDtypeStruct((M, N), F32))
        args.append(w_proj)
    return pl.pallas_call(
        functools.partial(_norm_kernel, with_proj=w_proj is not None),
        grid=(M // tr,),
        in_specs=in_specs,
        out_specs=out_specs,
        out_shape=out_shape,
        compiler_params=_cparams("parallel"),
        name="norm_proj" if w_proj is not None else "norm",
    )(*args)


def _mm_ws_kernel(a_ref, w_ref, res_ref, o_ref, wbf_ref):
    @pl.when(pl.program_id(1) == 0)
    def _():
        wbf_ref[...] = w_ref[...].astype(BF)

    o_ref[...] = res_ref[...] + _dot(a_ref[...], wbf_ref[...])


def _matmul_ws(a, w, layer, res, *, tm_pref=1024, tn_pref=512, name="matmul_ws"):
    M, K = a.shape
    N = w.shape[-1]
    tm, tn = _tile(M, tm_pref), _tile(N, tn_pref)
    out_blk = pl.BlockSpec((tm, tn), lambda j, i: (i, j))
    return pl.pallas_call(
        _mm_ws_kernel,
        grid=(N // tn, M // tm),
        in_specs=[pl.BlockSpec((tm, K), lambda j, i: (i, 0)),
                  pl.BlockSpec((None, K, tn), lambda j, i: (layer, 0, j)),
                  out_blk],
        out_specs=out_blk,
        out_shape=jax.ShapeDtypeStruct((M, N), res.dtype),
        scratch_shapes=[pltpu.VMEM((K, tn), BF)],
        compiler_params=_cparams("parallel", "arbitrary"),
        name=name,
    )(a, w, res)


def _mm_wst_kernel(a_ref, wt_hbm, o_ref, wf32_ref, wbf_ref, sem, *, layer, tn, n_j, n_head, skip):
    j = pl.program_id(0)

    def fetch(t):
        start = pl.multiple_of(t * tn + jnp.where(t >= n_head, skip, 0), SUBLANES)
        return pltpu.make_async_copy(wt_hbm.at[layer, pl.ds(start, tn)], wf32_ref.at[t % 2],
                                     sem.at[t % 2])

    @pl.when(pl.program_id(1) == 0)
    def _():
        @pl.when(j == 0)
        def _():
            fetch(j).start()

        fetch(j).wait()
        wbf_ref[...] = wf32_ref[j % 2].astype(BF)

        @pl.when(j + 1 < n_j)
        def _():
            fetch(j + 1).start()

    o_ref[...] = _dot_nt(a_ref[...], wbf_ref[...]).astype(o_ref.dtype)


def _matmul_wst(a, wt, layer, n_head, skip, n_out, out_dtype, *, tm_pref=1024, tn_pref=512,
                name="matmul_wst"):
    M, K = a.shape
    tm, tn = _tile(M, tm_pref), _tile(n_out, tn_pref)
    assert n_head % tn == 0 and skip % SUBLANES == 0
    n_j = n_out // tn
    return pl.pallas_call(
        functools.partial(_mm_wst_kernel, layer=layer, tn=tn, n_j=n_j, n_head=n_head // tn, skip=skip),
        grid=(n_j, M // tm),
        in_specs=[pl.BlockSpec((tm, K), lambda j, i: (i, 0)),
                  pl.BlockSpec(memory_space=pl.ANY)],
        out_specs=pl.BlockSpec((tm, tn), lambda j, i: (i, j)),
        out_shape=jax.ShapeDtypeStruct((M, n_out), out_dtype),
        scratch_shapes=[pltpu.VMEM((2, tn, K), F32), pltpu.VMEM((tn, K), BF),
                        pltpu.SemaphoreType.DMA((2,))],
        compiler_params=_cparams("arbitrary", "arbitrary"),
        name=name,
    )(a, wt)


def _split3(x):
    hi = x.astype(BF)
    r1 = x - hi.astype(F32)
    mid = r1.astype(BF)
    lo = (r1 - mid.astype(F32)).astype(BF)
    return hi, mid, lo


def _gate_terms(ba, eb, ea, sel, alog, dtb):
    R = ba.shape[0]
    beta = jax.nn.sigmoid(ba)
    x = ba + dtb
    softplus = jnp.maximum(x, 0.0) + jnp.log1p(jnp.exp(-jnp.abs(x)))
    g = -jnp.exp(alog) * softplus
    ri = lax.broadcasted_iota(jnp.int32, (R, R), 0)
    ci = lax.broadcasted_iota(jnp.int32, (R, R), 1)
    tri = ((ri // DN_CHUNK == ci // DN_CHUNK) & (ri >= ci)).astype(BF)
    gc = sum(_dot(tri, piece) for piece in _split3(g))
    gc_parts = _split3(gc)
    beta_b = sum(_dot(piece, eb) for piece in _split3(beta))
    gc_b = sum(_dot(piece, ea) for piece in gc_parts)
    gc_t = sum(_dot_nt(sel, piece) for piece in gc_parts)
    return beta_b, gc_b, gc_t


def _group_rows(heads_per_group):
    return -(-heads_per_group // SUBLANES) * SUBLANES


def _gate_constants(a_log, dt_bias, n_heads, heads_per_group):
    G, H = heads_per_group, n_heads
    W = H * DN_HEAD_DIM
    lane_head = jnp.arange(W) // DN_HEAD_DIM
    rows = jnp.arange(LANES)[:, None]
    eb = (rows == lane_head[None, :]).astype(BF)
    ea = (rows == lane_head[None, :] + H).astype(BF)
    rpg = _group_rows(G)
    r = jnp.arange((H // G) * rpg)
    head_of_row = jnp.where(r % rpg < G, (r // rpg) * G + r % rpg, -LANES)
    sel = (jnp.arange(LANES)[None, :] == head_of_row[:, None] + H).astype(BF)
    at_a = lambda v: jnp.pad(v.astype(F32), (H, LANES - 2 * H))[None, :]
    return eb, ea, sel, at_a(a_log), at_a(dt_bias)


def _shift_rows(t, halo, k):
    r = pltpu.roll(t, k, 0)
    hr = pltpu.roll(halo, k, 0)[:8]
    rows = lax.broadcasted_iota(jnp.int32, (8, t.shape[1]), 0)
    first = jnp.where(rows < k, hr, r[:8])
    return jnp.concatenate([first, r[8:]], axis=0)


def _causal_conv(t, halo, w):
    K = w.shape[0]
    y = t * w[K - 1:K]
    for d in range(1, K):
        y = y + _shift_rows(t, halo, d) * w[K - 1 - d:K - d]
    return y


def _halo_spec(rows_per_tile, width, col_block, n_lead):
    per = rows_per_tile // HALO
    if n_lead == 1:
        return pl.BlockSpec((HALO, width), lambda i: (jnp.maximum(i * per - 1, 0), col_block))
    return pl.BlockSpec((HALO, width), lambda i, g: (jnp.maximum(i * per - 1, 0), col_block(g)))


def _conv_a_kernel(cx_ref, cc_ref, cb_ref, hx_ref, hc_ref, w_ref, o_ref, *, tiles_per_seq):
    first = (pl.program_id(0) % tiles_per_seq == 0)
    keep = jnp.where(first, 0.0, 1.0)
    t = cx_ref[...].astype(F32) * cc_ref[...].astype(F32)
    halo = hx_ref[...].astype(F32) * hc_ref[...].astype(F32) * keep
    y = _causal_conv(t, halo, w_ref[...])
    o_ref[...] = (cb_ref[...].astype(F32) * y).astype(o_ref.dtype)


def _conv_a(p, conv_w, seq, width, tr_pref=256):
    M = p.shape[0]
    tr = _tile(seq, tr_pref)
    blk = lambda c: pl.BlockSpec((tr, width), lambda i: (i, c))
    return pl.pallas_call(
        functools.partial(_conv_a_kernel, tiles_per_seq=seq // tr),
        grid=(M // tr,),
        in_specs=[blk(0), blk(1), blk(2), _halo_spec(tr, width, 0, 1), _halo_spec(tr, width, 1, 1),
                  pl.BlockSpec(conv_w.shape, lambda i: (0, 0))],
        out_specs=pl.BlockSpec((tr, width), lambda i: (i, 0)),
        out_shape=jax.ShapeDtypeStruct((M, width), BF),
        compiler_params=_cparams("parallel"),
        name="conv_a",
    )(p, p, p, p, p, conv_w)


def _dn_prep_kernel(q_ref, k_ref, v_ref, hq_ref, hk_ref, hv_ref, wq_ref, wk_ref, wv_ref,
                    ba_ref, eb_ref, ea_ref, sel_ref, alog_ref, dtb_ref,
                    qeff_ref, wdec_ref, ktail_ref, ubase_ref, obase_ref, glast_ref,
                    *, tiles_per_seq, heads, wave):
    R = q_ref.shape[0]
    C, DK = DN_CHUNK, DN_HEAD_DIM
    first = (pl.program_id(0) % tiles_per_seq == 0)
    keep = jnp.where(first, 0.0, 1.0)
    beta_b, gc_b, gc_t = _gate_terms(ba_ref[...], eb_ref[...], ea_ref[...], sel_ref[...],
                                     alog_ref[...], dtb_ref[...])
    for c in range(R // C):
        glast_ref[c] = jnp.exp(gc_b[c * C + C - 1:c * C + C, :])
    ri = lax.broadcasted_iota(jnp.int32, (R, R), 0)
    ci = lax.broadcasted_iota(jnp.int32, (R, R), 1)
    same = (ri // C) == (ci // C)
    incl = same & (ri >= ci)
    strict = same & (ri > ci)

    def l2n(x):
        return x * lax.rsqrt(jnp.sum(x * x, axis=-1, keepdims=True) + EPS)

    def head_inputs(hh):
        cols = slice(hh * DK, (hh + 1) * DK)

        def conv_silu(t_ref, h_ref, w_ref):
            y = _causal_conv(t_ref[:, cols].astype(F32), h_ref[:, cols].astype(F32) * keep,
                             w_ref[:, cols])
            return _silu(y)

        q = l2n(conv_silu(q_ref, hq_ref, wq_ref)) * (DK ** -0.5)
        k = l2n(conv_silu(k_ref, hk_ref, wk_ref))
        v = conv_silu(v_ref, hv_ref, wv_ref)
        beta = beta_b[:, cols]
        gc = gc_b[:, cols]
        g_row = jnp.broadcast_to(gc_t[hh:hh + 1, :], (R, R))
        g_col = jnp.concatenate([gc] * (R // DK), axis=1)
        e = jnp.exp(jnp.where(incl, g_col - g_row, 0.0))
        decay_incl = jnp.where(incl, e, 0.0)
        decay_strict = jnp.where(strict, e, 0.0)
        kb = k * beta
        k16 = k.astype(BF)
        m0 = _dot_nt((-kb).astype(BF), k16) * decay_strict
        aqk = (_dot_nt(q.astype(BF), k16) * decay_incl).astype(BF)
        rhs = jnp.concatenate([v * beta, kb * jnp.exp(gc)], axis=1)
        return dict(cols=cols, q=q, k=k, gc=gc, aqk=aqk, rhs=rhs, m=m0)

    n_fac = C.bit_length() - 1
    waves = [list(range(w0, min(w0 + wave, heads))) for w0 in range(0, heads, wave)]
    hs = [head_inputs(hh) for hh in waves[0]]
    for wi in range(len(waves)):
        todo = list(waves[wi + 1]) if wi + 1 < len(waves) else []
        hs_next = []
        for j in range(n_fac):
            for d in hs:
                m16 = d["m"].astype(BF)
                if j == 0:
                    d["qm"] = d["m"]
                    d["m"] = _dot(m16, m16)
                elif j < n_fac - 1:
                    r = _dot(jnp.concatenate([m16, d["qm"].astype(BF)], axis=0), m16)
                    d["qm"] = d["qm"] + d["m"] + r[R:]
                    d["m"] = r[:R]
                else:
                    d["qm"] = d["qm"] + d["m"] + _dot(d["qm"].astype(BF), m16)
            n_emit = -(-len(todo) // (n_fac - j))
            hs_next += [head_inputs(hh) for hh in todo[:n_emit]]
            todo = todo[n_emit:]
        for d in hs:
            d["sol"] = d["rhs"] + _dot(d["qm"].astype(BF), d["rhs"].astype(BF))
        for d in hs:
            cols, sol, gc, q, k = d["cols"], d["sol"], d["gc"], d["q"], d["k"]
            ubase, wdec = sol[:, :DK], sol[:, DK:]
            x = _dot(d["aqk"], jnp.concatenate([wdec, ubase], axis=1).astype(BF))
            gc_last = jnp.concatenate(
                [jnp.broadcast_to(gc[c * C + C - 1:c * C + C, :], (C, DK)) for c in range(R // C)],
                axis=0)
            qeff_ref[:, cols] = (q * jnp.exp(gc) - x[:, :DK]).astype(qeff_ref.dtype)
            obase_ref[:, cols] = x[:, DK:]
            ubase_ref[:, cols] = ubase
            wdec_ref[:, cols] = wdec.astype(wdec_ref.dtype)
            ktail_ref[:, cols] = (k * jnp.exp(gc_last - gc)).astype(ktail_ref.dtype)
        hs = hs_next


def _dn_prep(p, qkv_conv_w, ba, a_log, dt_bias, seq, n_heads, heads_per_group, q_off):
    M = p.shape[0]
    G = heads_per_group
    GW = G * DN_HEAD_DIM
    W = n_heads * DN_HEAD_DIM
    n_groups = n_heads // G
    R = _tile(seq, 256)
    rpg = _group_rows(G)
    assert q_off % GW == 0 and R % DN_HEAD_DIM == 0
    eb, ea, sel, alog_c, dtb_c = _gate_constants(a_log, dt_bias, n_heads, G)
    cols = [lambda g, o=o: q_off // GW + o * n_groups + g for o in range(3)]
    blk = lambda c: pl.BlockSpec((R, GW), lambda i, g: (i, c(g)))
    wblk = lambda o: pl.BlockSpec((qkv_conv_w.shape[0], GW), lambda i, g: (0, o * n_groups + g))
    head = pl.BlockSpec((R, GW), lambda i, g: (i, g))
    lanes = pl.BlockSpec((1, LANES), lambda i, g: (0, 0))
    n_ch = R // DN_CHUNK
    return pl.pallas_call(
        functools.partial(_dn_prep_kernel, tiles_per_seq=seq // R, heads=G, wave=min(G, DN_WAVE)),
        grid=(M // R, n_groups),
        in_specs=[blk(cols[0]), blk(cols[1]), blk(cols[2]),
                  _halo_spec(R, GW, cols[0], 2), _halo_spec(R, GW, cols[1], 2),
                  _halo_spec(R, GW, cols[2], 2),
                  wblk(0), wblk(1), wblk(2),
                  pl.BlockSpec((R, LANES), lambda i, g: (i, 0)),
                  pl.BlockSpec((LANES, GW), lambda i, g: (0, g)),
                  pl.BlockSpec((LANES, GW), lambda i, g: (0, g)),
                  pl.BlockSpec((rpg, LANES), lambda i, g: (g, 0)), lanes, lanes],
        out_specs=[head] * 5 + [pl.BlockSpec((n_ch, 1, GW), lambda i, g: (i, 0, g))],
        out_shape=[jax.ShapeDtypeStruct((M, W), BF)] * 3 + [jax.ShapeDtypeStruct((M, W), F32)] * 2
        + [jax.ShapeDtypeStruct((M // DN_CHUNK, 1, W), F32)],
        compiler_params=_cparams("parallel", "parallel"),
        name="dn_prep",
    )(p, p, p, p, p, p, qkv_conv_w, qkv_conv_w, qkv_conv_w, ba, eb, ea, sel, alog_c, dtb_c)


def _dn_scan_kernel(qeff_ref, wdec_ref, ktail_ref, ubase_ref, obase_ref, glast_ref, z_ref, nw_ref,
                    o_ref, state_ref, *, heads):
    rows_per_step = qeff_ref.shape[0]
    C, DK = DN_CHUNK, DN_HEAD_DIM

    @pl.when(pl.program_id(2) == 0)
    def _():
        state_ref[...] = jnp.zeros_like(state_ref)

    nw = nw_ref[...]

    def chunk(c, carry):
        r0 = pl.multiple_of(c * C, C)
        rows = pl.ds(r0, C)
        g_last_all = glast_ref[c]
        for h in range(heads):
            cols = slice(h * DK, (h + 1) * DK)
            st = state_ref[h]
            lhs = jnp.concatenate([qeff_ref[rows, cols], wdec_ref[rows, cols]], axis=0)
            r = _dot(lhs, st.astype(BF))
            o = obase_ref[rows, cols] + r[:C]
            u = ubase_ref[rows, cols] - r[C:]
            g_last = g_last_all[:, cols]
            upd = lax.dot_general(ktail_ref[rows, cols], u.astype(BF), (((0,), (0,)), ((), ())),
                                  preferred_element_type=F32)
            state_ref[h] = st * g_last + upd
            zf = z_ref[rows, cols].astype(F32)
            on = o * lax.rsqrt(jnp.mean(o * o, axis=-1, keepdims=True) + EPS) * nw * _silu(zf)
            o_ref[rows, cols] = on.astype(o_ref.dtype)
        return carry

    lax.fori_loop(0, rows_per_step // C, chunk, 0, unroll=2)


def _dn_scan(qeff, wdec, ktail, ubase, obase, glast, p, dn_out_norm, seq, n_heads, heads_per_group,
             z_off, ts_pref=512):
    M, W = qeff.shape
    G = heads_per_group
    GW = G * DN_HEAD_DIM
    ts = _tile(seq, ts_pref)
    per = seq // ts
    assert z_off % GW == 0
    zb = z_off // GW
    blk = pl.BlockSpec((ts, GW), lambda b, g, t: (b * per + t, g))
    return pl.pallas_call(
        functools.partial(_dn_scan_kernel, heads=G),
        grid=(M // seq, n_heads // G, per),
        in_specs=[blk] * 5
        + [pl.BlockSpec((ts // DN_CHUNK, 1, GW), lambda b, g, t: (b * per + t, 0, g)),
           pl.BlockSpec((ts, GW), lambda b, g, t: (b * per + t, zb + g)),
           pl.BlockSpec((1, DN_HEAD_DIM), lambda b, g, t: (0, 0))],
        out_specs=blk,
        out_shape=jax.ShapeDtypeStruct((M, W), BF),
        scratch_shapes=[pltpu.VMEM((G, DN_HEAD_DIM, DN_HEAD_DIM), F32)],
        compiler_params=_cparams("parallel", "parallel", "arbitrary"),
        name="dn_scan",
    )(qeff, wdec, ktail, ubase, obase, glast, p, dn_out_norm)


def _merge_kernel(ya_ref, yb_ref, wa_ref, wb_ref, ga_ref, gb_ref, o_ref, wa16_ref, wb16_ref):
    @pl.when(pl.program_id(1) == 0)
    def _():
        wa16_ref[...] = wa_ref[...].astype(BF)
        wb16_ref[...] = wb_ref[...].astype(BF)

    ya = _dot(ya_ref[...], wa16_ref[...])
    yb = _dot(yb_ref[...], wb16_ref[...])
    m = jax.nn.sigmoid(ga_ref[...].astype(F32)) * ya + jax.nn.sigmoid(gb_ref[...].astype(F32)) * yb
    o_ref[...] = m.astype(o_ref.dtype)


def _merge(ya, yb, wa, wb, layer, p, gate_off, tm_pref=1024, tn_pref=512):
    M, KA = ya.shape
    KB = yb.shape[1]
    N = wa.shape[-1]
    tm, tn = _tile(M, tm_pref), _tile(N, tn_pref)
    while gate_off % tn:
        tn //= 2
    nb = N // tn
    g0 = gate_off // tn
    return pl.pallas_call(
        _merge_kernel,
        grid=(nb, M // tm),
        in_specs=[pl.BlockSpec((tm, KA), lambda j, i: (i, 0)),
                  pl.BlockSpec((tm, KB), lambda j, i: (i, 0)),
                  pl.BlockSpec((None, KA, tn), lambda j, i: (layer, 0, j)),
                  pl.BlockSpec((None, KB, tn), lambda j, i: (layer, 0, j)),
                  pl.BlockSpec((tm, tn), lambda j, i: (i, g0 + j)),
                  pl.BlockSpec((tm, tn), lambda j, i: (i, g0 + nb + j))],
        out_specs=pl.BlockSpec((tm, tn), lambda j, i: (i, j)),
        out_shape=jax.ShapeDtypeStruct((M, N), BF),
        scratch_shapes=[pltpu.VMEM((KA, tn), BF), pltpu.VMEM((KB, tn), BF)],
        compiler_params=_cparams("parallel", "arbitrary"),
        name="merge",
    )(ya, yb, wa, wb, p, p)


def _mem_kv_kernel(m_ref, nw_ref, wk_ref, wv_ref, k_ref, v_ref):
    mn = _rms(m_ref[...], nw_ref[...]).astype(BF)
    k_ref[...] = _dot(mn, wk_ref[...].astype(BF)).astype(k_ref.dtype)
    v_ref[...] = _dot(mn, wv_ref[...].astype(BF)).astype(v_ref.dtype)


def _mem_kv(mem, nw, wk, wv, tr_pref=256):
    Mm, D = mem.shape
    N = wk.shape[1]
    tr = _tile(Mm, tr_pref)
    full = lambda shape: pl.BlockSpec(shape, lambda i: (0, 0))
    return pl.pallas_call(
        _mem_kv_kernel,
        grid=(Mm // tr,),
        in_specs=[pl.BlockSpec((tr, D), lambda i: (i, 0)), full((1, D)), full((D, N)), full((D, N))],
        out_specs=[pl.BlockSpec((tr, N), lambda i: (i, 0))] * 2,
        out_shape=[jax.ShapeDtypeStruct((Mm, N), BF)] * 2,
        compiler_params=_cparams("parallel"),
        name="mem_kv",
    )(mem, nw, wk, wv)


def _xattn_kernel(h_ref, nw_ref, wq_ref, k_ref, v_ref, wo_ref, nw2_ref, o_ref, on_ref, *, n_heads):
    DH = XATTN_HEAD_DIM
    h = h_ref[...]
    q = _dot(_rms(h, nw_ref[...]).astype(BF), wq_ref[...])
    outs = []
    for hd in range(n_heads):
        cols = slice(hd * DH, (hd + 1) * DH)
        s = _dot_nt(q[:, cols].astype(BF), k_ref[:, cols]) * (DH ** -0.5)
        s = s - jnp.max(s, axis=-1, keepdims=True)
        e = jnp.exp(s)
        pr = e / jnp.sum(e, axis=-1, keepdims=True)
        outs.append(_dot(pr.astype(BF), v_ref[:, cols]))
    o = jnp.concatenate(outs, axis=1).astype(BF)
    h_out = h + _dot(o, wo_ref[...])
    o_ref[...] = h_out
    on_ref[...] = _rms(h_out, nw2_ref[...]).astype(on_ref.dtype)


def _xattn(h, nw, wq, kh, vh, wo, nw2, seq, n_mem, tm_pref=512):
    M, D = h.shape
    N = wq.shape[1]
    tm = _tile(seq, tm_pref)
    per = seq // tm
    full = lambda shape: pl.BlockSpec(shape, lambda i: (0, 0))
    kv = pl.BlockSpec((n_mem, N), lambda i: (i // per, 0))
    row = pl.BlockSpec((tm, D), lambda i: (i, 0))
    return pl.pallas_call(
        functools.partial(_xattn_kernel, n_heads=N // XATTN_HEAD_DIM),
        grid=(M // tm,),
        in_specs=[row, full((1, D)), full((D, N)), kv, kv, full((N, D)), full((1, D))],
        out_specs=[row, row],
        out_shape=[jax.ShapeDtypeStruct((M, D), F32), jax.ShapeDtypeStruct((M, D), BF)],
        compiler_params=_cparams("parallel"),
        name="xattn",
    )(h, nw, wq, kh, vh, wo, nw2)


def kernel(x, mem, ffn1_norm, ffn1_w_gate, ffn1_w_up, ffn1_w_down, mix_norm, w_in, conv_w, qkv_conv_w, a_log, dt_bias, dn_out_norm, w_out_conv, w_out_delta, w_o, xattn_norm, mem_norm, xattn_wq, xattn_wk, xattn_wv, xattn_wo, ffn2_norm, ffn2_w_gate, ffn2_w_up, ffn2_w_down, final_norm):
    B, S, D = x.shape
    depth = ffn1_norm.shape[0]
    n_mem = mem.shape[1]
    CW = conv_w.shape[-1]
    H = a_log.shape[-1]
    DW = H * DN_HEAD_DIM
    G = min(H, DN_HEADS_PER_STEP)
    M = B * S
    assert S % DN_CHUNK == 0 and 2 * H <= LANES and H % G == 0
    o_qkv = 3 * CW
    o_z = o_qkv + 3 * DW
    o_b = o_z + DW
    o_ga = o_b + 2 * H
    assert w_in.shape[-1] == o_ga + 2 * D
    vec = lambda v: v.reshape(1, -1).astype(F32)
    bf = lambda w: w.astype(BF)

    h = x.reshape(M, D)
    mem2 = mem.reshape(B * n_mem, D)
    for l in range(depth):
        h = _ffn(_norm(h, vec(ffn1_norm[l]))[0], h, ffn1_w_gate, ffn1_w_up, ffn1_w_down,
                 vec(mix_norm[l]), l, norm_out=False, name="ffn1")

        w_in_t = jnp.swapaxes(w_in, 1, 2)
        w_ba = bf(jnp.pad(w_in_t[l, o_b:o_ga], ((0, LANES - 2 * H), (0, 0))))
        un, ba = _norm(h, vec(mix_norm[l]), w_ba)
        p = _matmul_wst(un, w_in_t, l, o_b, o_ga - o_b, o_b + 2 * D, BF, name="in_proj")

        ya = _conv_a(p, conv_w[l].astype(F32), S, CW)
        qeff, wdec, ktail, ubase, obase, glast = _dn_prep(
            p, qkv_conv_w[l].astype(F32), ba, a_log[l], dt_bias[l], S, H, G, o_qkv)
        yb = _dn_scan(qeff, wdec, ktail, ubase, obase, glast, p, vec(dn_out_norm[l]), S, H, G, o_z)

        merged = _merge(ya, yb, w_out_conv, w_out_delta, l, p, o_b)
        h = _matmul_ws(merged, w_o, l, h, name="oproj")

        kh, vh = _mem_kv(mem2, vec(mem_norm[l]), xattn_wk[l], xattn_wv[l])
        h, xn = _xattn(h, vec(xattn_norm[l]), bf(xattn_wq[l]), kh, vh, bf(xattn_wo[l]),
                       vec(ffn2_norm[l]), S, n_mem)

        h = _ffn(xn, h, ffn2_w_gate, ffn2_w_up, ffn2_w_down, vec(final_norm), l,
                 norm_out=(l == depth - 1), name="ffn2")
    return h.reshape(B, S, D)
```
